```python
import math
import functools
import jax
import jax.numpy as jnp
from jax import lax
import numpy as np

D_MODEL = 1024
BATCH = 16
SEQ = 2048
DEPTH = 1
DEC_BATCH = 32
DEC_SEQ = 8
PAST_LEN = 16384
PAGE_SIZE = 128

D_FF = 2816
D_INNER = 2 * D_MODEL
M_HEADDIM = 64
M_HEADS = D_INNER // M_HEADDIM
M_GROUPS = 4
M_HPG = M_HEADS // M_GROUPS
D_STATE = 128
D_CONV = 4
CONV_DIM = D_INNER + 2 * M_GROUPS * D_STATE
SSD_CHUNK = 128
HEAD_DIM = 64
Q_HEADS = D_MODEL // HEAD_DIM
KV_HEADS = 4
Q_PER_KV = Q_HEADS // KV_HEADS
D_ATTN = Q_HEADS * HEAD_DIM
CMP_BLOCK = 32
CMP_HIDDEN = 2 * HEAD_DIM
SEL_BLOCK = 64
SEL_TOPK = 16
WINDOW = 512
Q_BLOCK = 16
NUM_BUCKETS = 32
MAX_DISTANCE = 2048
EPS = 1e-6
PROJ_SIZES = (D_INNER, CONV_DIM, M_HEADS, D_ATTN, 6 * KV_HEADS * HEAD_DIM, 3 * Q_HEADS, 2 * D_MODEL)
D_PROJ = sum(PROJ_SIZES)

kernel_name = 'hybrid_ssd_nsa_macaron_step'


def rmsnorm(x, g):
    xf = x.astype(jnp.float32)
    y = xf * lax.rsqrt(jnp.mean(xf * xf, axis=-1, keepdims=True) + EPS)
    return (y * g.astype(jnp.float32)).astype(x.dtype)


def swiglu_ffn(x, w_in, w_out):
    gu = x @ w_in
    return (jax.nn.silu(gu[..., :D_FF]) * gu[..., D_FF:]) @ w_out


def rel_bucket(dist):
    max_exact = NUM_BUCKETS // 2
    d = jnp.maximum(dist, 0)
    df = jnp.maximum(d, 1).astype(jnp.float32)
    large = max_exact + (jnp.log(df / max_exact) / math.log(MAX_DISTANCE / max_exact)
                         * (NUM_BUCKETS - max_exact)).astype(jnp.int32)
    large = jnp.minimum(large, NUM_BUCKETS - 1)
    return jnp.where(d < max_exact, d, large)


def bias_qk(dist, bias_tab):
    return jnp.transpose(bias_tab[rel_bucket(dist)], (0, 2, 3, 1))[None]


def masked_softmax(s, mask):
    s = jnp.where(mask, s.astype(jnp.float32), -jnp.inf)
    m = jnp.max(s, axis=-1, keepdims=True)
    m = jnp.where(jnp.isfinite(m), m, 0.0)
    e = jnp.exp(s - m)
    return e / jnp.maximum(jnp.sum(e, axis=-1, keepdims=True), 1e-30)


def causal_conv(xbc, conv_state, w, b):
    L = xbc.shape[1]
    xp = jnp.concatenate([conv_state.astype(xbc.dtype), xbc], axis=1)
    y = b + sum(xp[:, k:k + L] * w[k] for k in range(D_CONV))
    return jax.nn.silu(y), xp[:, L:]


def segsum(a):
    l = a.shape[-1]
    cs = jnp.cumsum(a, axis=-1)
    diff = cs[..., :, None] - cs[..., None, :]
    return jnp.where(jnp.tril(jnp.ones((l, l), dtype=bool)), diff, -jnp.inf)


def ssd(x, dt, A, Bm, Cm, h0):
    b, L, G, J, P = x.shape
    N = Bm.shape[-1]
    cl = SSD_CHUNK if L % SSD_CHUNK == 0 else L
    nc = L // cl
    f32 = jnp.float32
    xdt = (x.astype(f32) * dt[..., None]).reshape(b, nc, cl, G, J, P)
    Bc = Bm.astype(f32).reshape(b, nc, cl, G, N)
    Cc = Cm.astype(f32).reshape(b, nc, cl, G, N)
    dA = jnp.moveaxis((dt * A).reshape(b, nc, cl, G, J), 2, -1)
    a_cs = jnp.cumsum(dA, axis=-1)
    decay_in = jnp.exp(segsum(dA))
    cb = jnp.einsum('bclgn,bcsgn->bcgls', Cc, Bc)
    y_diag = jnp.einsum('bcgjls,bcsgjp->bclgjp', cb[:, :, :, None] * decay_in, xdt)
    decay_to_end = jnp.moveaxis(jnp.exp(a_cs[..., -1:] - a_cs), -1, 2)
    states = jnp.einsum('bclgn,bclgjp->bcgjpn', Bc, xdt * decay_to_end[..., None])
    chunk_decay = jnp.exp(a_cs[..., -1])

    def step(h, inp):
        s_c, d_c = inp
        return h * d_c[..., None, None] + s_c, h

    h_final, h_start = lax.scan(step, h0, (jnp.moveaxis(states, 1, 0), jnp.moveaxis(chunk_decay, 1, 0)))
    h_start = jnp.moveaxis(h_start, 0, 1)
    decay_from_start = jnp.moveaxis(jnp.exp(a_cs), -1, 2)
    y_off = jnp.einsum('bclgn,bcgjpn->bclgjp', Cc, h_start) * decay_from_start[..., None]
    return (y_diag + y_off).reshape(b, L, G, J, P), h_final


def mamba_mixer(z, xbc, dt_raw, conv_state, ssm_state, conv_w, conv_b, dt_bias, a_log, d_skip, ssm_norm):
    b, L = z.shape[:2]
    f32 = jnp.float32
    xbc_c, new_conv = causal_conv(xbc, conv_state, conv_w, conv_b)
    xs = xbc_c[..., :D_INNER].reshape(b, L, M_GROUPS, M_HPG, M_HEADDIM)
    Bm = xbc_c[..., D_INNER:D_INNER + M_GROUPS * D_STATE].reshape(b, L, M_GROUPS, D_STATE)
    Cm = xbc_c[..., D_INNER + M_GROUPS * D_STATE:].reshape(b, L, M_GROUPS, D_STATE)
    dt = jax.nn.softplus(dt_raw.astype(f32) + dt_bias.astype(f32)).reshape(b, L, M_GROUPS, M_HPG)
    A = -jnp.exp(a_log.astype(f32)).reshape(M_GROUPS, M_HPG)
    h0 = ssm_state.astype(f32).reshape(b, M_GROUPS, M_HPG, M_HEADDIM, D_STATE)
    y, h_final = ssd(xs, dt, A, Bm, Cm, h0)
    y = y + d_skip.astype(f32).reshape(M_GROUPS, M_HPG)[:, :, None] * xs.astype(f32)
    y = y.reshape(b, L, D_INNER) * jax.nn.silu(z.astype(f32))
    yg = y.reshape(b, L, M_GROUPS, D_INNER // M_GROUPS)
    yg = yg * lax.rsqrt(jnp.mean(yg * yg, axis=-1, keepdims=True) + EPS)
    y = yg.reshape(b, L, D_INNER) * ssm_norm.astype(f32)
    new_ssm = h_final.reshape(b, M_HEADS, M_HEADDIM, D_STATE).astype(ssm_state.dtype)
    return y.astype(z.dtype), new_conv, new_ssm


def compress_rows(rows, pe, w1, w2):
    b, T = rows.shape[:2]
    n = T // CMP_BLOCK
    blk = rows[:, :n * CMP_BLOCK].reshape(b, n, CMP_BLOCK, KV_HEADS, HEAD_DIM) + pe[None, None, :, None, :]
    hid = jax.nn.silu(jnp.einsum('bnjhd,jdf->bnhf', blk, w1))
    return jnp.einsum('bnhf,fd->bnhd', hid, w2)


def sel_blocks(rows):
    b, T = rows.shape[:2]
    n_s = -(-T // SEL_BLOCK)
    rows = jnp.pad(rows, ((0, 0), (0, n_s * SEL_BLOCK - T), (0, 0), (0, 0)))
    return jnp.transpose(rows.reshape(b, n_s, SEL_BLOCK, KV_HEADS, HEAD_DIM), (0, 3, 1, 2, 4))


def nsa_block(q, q_pos, gates, kc, vc, kc_end, ksel_h, vsel_h, kw, vw, kw_pos, bias_tab):
    f32 = jnp.float32
    b, t = q.shape[:2]
    dist_c = q_pos[:, None] - kc_end[None, :]
    s_c = jnp.einsum('bthgd,bnhd->bthgn', q, kc, preferred_element_type=f32) + bias_qk(dist_c, bias_tab)
    p_c = masked_softmax(s_c, (dist_c >= 0)[None, :, None, None, :])
    o_c = jnp.einsum('bthgn,bnhd->bthgd', p_c, vc.astype(f32))
    n_s = ksel_h.shape[2]
    ratio = SEL_BLOCK // CMP_BLOCK
    imp = jnp.sum(p_c, axis=3)
    imp = jnp.pad(imp, ((0, 0), (0, 0), (0, 0), (0, n_s * ratio - imp.shape[-1])))
    imp = imp.reshape(b, t, KV_HEADS, n_s, ratio).sum(-1)
    blk = jnp.arange(n_s, dtype=jnp.int32)[None, :]
    cur = (q_pos // SEL_BLOCK)[:, None]
    forced = (blk == 0) | (blk == cur) | (blk == cur - 1)
    causal = blk * SEL_BLOCK <= q_pos[:, None]
    score = jnp.where(forced[None, :, None, :], jnp.inf,
                      jnp.where(causal[None, :, None, :], imp, -jnp.inf))
    _, idx = lax.top_k(score, min(SEL_TOPK, n_s))
    bi = jnp.arange(b)[:, None, None, None]
    hi = jnp.arange(KV_HEADS)[None, None, :, None]
    ks = ksel_h[bi, hi, idx]
    vs = vsel_h[bi, hi, idx]
    key_pos = idx[..., None] * SEL_BLOCK + jnp.arange(SEL_BLOCK, dtype=jnp.int32)
    dist_s = q_pos[None, :, None, None, None] - key_pos
    bias_s = jnp.moveaxis(bias_tab[rel_bucket(dist_s), hi[..., None]], -1, 3)
    s_s = jnp.einsum('bthgd,bthksd->bthgks', q, ks, preferred_element_type=f32) + bias_s
    shp = s_s.shape
    n_k = shp[-2] * shp[-1]
    p_s = masked_softmax(s_s.reshape(b, t, KV_HEADS, Q_PER_KV, n_k),
                         (dist_s >= 0).reshape(b, t, KV_HEADS, 1, n_k)).reshape(shp)
    o_s = jnp.einsum('bthgks,bthksd->bthgd', p_s, vs.astype(f32))
    dist_w = q_pos[:, None] - kw_pos[None, :]
    mask_w = (dist_w >= 0) & (dist_w <= WINDOW) & (kw_pos >= 0)[None, :]
    s_w = jnp.einsum('bthgd,bnhd->bthgn', q, kw, preferred_element_type=f32) + bias_qk(dist_w, bias_tab)
    p_w = masked_softmax(s_w, mask_w[None, :, None, None, :])
    o_w = jnp.einsum('bthgn,bnhd->bthgd', p_w, vw.astype(f32))
    return gates[..., 0:1] * o_c + gates[..., 1:2] * o_s + gates[..., 2:3] * o_w


def nsa_prompt(q, gates, paged_rows, win_rows, compress_kv, bias_tab):
    b, L = q.shape[:2]
    kc, vc = compress_kv(paged_rows[:, :, 0], paged_rows[:, :, 1])
    kc_end = jnp.arange(kc.shape[1], dtype=jnp.int32) * CMP_BLOCK + (CMP_BLOCK - 1)
    ksel_h = sel_blocks(paged_rows[:, :, 2])
    vsel_h = sel_blocks(paged_rows[:, :, 3])
    w_pad = jnp.pad(win_rows, ((0, 0), (WINDOW, 0), (0, 0), (0, 0), (0, 0)))
    w_pos = jnp.arange(-WINDOW, L, dtype=jnp.int32)
    qb = Q_BLOCK if L % Q_BLOCK == 0 else L

    def one_block(i):
        s = i * qb
        w_i = lax.dynamic_slice_in_dim(w_pad, s, qb + WINDOW, axis=1)
        return nsa_block(lax.dynamic_slice_in_dim(q, s, qb, axis=1),
                         s + jnp.arange(qb, dtype=jnp.int32),
                         lax.dynamic_slice_in_dim(gates, s, qb, axis=1),
                         kc, vc, kc_end, ksel_h, vsel_h, w_i[:, :, 0], w_i[:, :, 1],
                         lax.dynamic_slice_in_dim(w_pos, s, qb + WINDOW, axis=0), bias_tab)

    o = lax.map(one_block, jnp.arange(L // qb, dtype=jnp.int32))
    o = jnp.moveaxis(o, 0, 1).reshape(b, L, KV_HEADS, Q_PER_KV, HEAD_DIM)
    return o, win_rows[:, L - min(WINDOW, L):]


def nsa_sample(q, gates, paged_rows, win_rows, compress_kv, bias_tab, cache_kv_l, page_table, cache_win_l):
    b, L = q.shape[:2]
    past_len = page_table.shape[1] * cache_kv_l.shape[1]

    def rows_with_past(c):
        past = cache_kv_l[page_table, :, c].reshape(b, past_len, KV_HEADS, HEAD_DIM)
        return jnp.concatenate([past, paged_rows[:, :, c].astype(past.dtype)], axis=1)

    kc, vc = compress_kv(rows_with_past(0), rows_with_past(1))
    kc_end = jnp.arange(kc.shape[1], dtype=jnp.int32) * CMP_BLOCK + (CMP_BLOCK - 1)
    ksel_h = sel_blocks(rows_with_past(2))
    vsel_h = sel_blocks(rows_with_past(3))
    win_all = jnp.concatenate([cache_win_l, win_rows.astype(cache_win_l.dtype)], axis=1)
    n_buf = cache_win_l.shape[1]
    w_pos = jnp.arange(past_len - n_buf, past_len + L, dtype=jnp.int32)
    o = nsa_block(q, past_len + jnp.arange(L, dtype=jnp.int32), gates, kc, vc, kc_end, ksel_h, vsel_h,
                  win_all[:, :, 0], win_all[:, :, 1], w_pos, bias_tab)
    n_keep = min(WINDOW, n_buf + L)
    return o, win_all[:, n_buf + L - n_keep:]


def layer_forward(x, ssm_state, conv_state, nsa_fn, bias_tab, lw):
    (ffn1_norm, ffn1_w_in, ffn1_w_out, mix_norm, w_in_proj, conv_w, conv_b, dt_bias, a_log, d_skip,
     ssm_norm, qk_norm, cmp_pe, cmp_w1, cmp_w2, w_branch_ssm, w_branch_attn, w_out,
     ffn2_norm, ffn2_w_in, ffn2_w_out) = lw
    f32 = jnp.float32
    b, L = x.shape[:2]
    x = x + 0.5 * swiglu_ffn(rmsnorm(x, ffn1_norm), ffn1_w_in, ffn1_w_out)
    h = rmsnorm(x, mix_norm)
    cuts = [int(c) for c in np.cumsum(PROJ_SIZES)[:-1]]
    z, xbc, dt_raw, q, kv, nsa_g, merge_g = jnp.split(h @ w_in_proj, cuts, axis=-1)
    y_ssm, new_conv, new_ssm = mamba_mixer(z, xbc, dt_raw, conv_state, ssm_state, conv_w, conv_b,
                                           dt_bias, a_log, d_skip, ssm_norm)
    q = rmsnorm(q.reshape(b, L, KV_HEADS, Q_PER_KV, HEAD_DIM), qk_norm[0]) * (HEAD_DIM ** -0.5)
    kv = kv.reshape(b, L, 6, KV_HEADS, HEAD_DIM)
    paged_rows = jnp.stack([kv[:, :, 0], kv[:, :, 1], rmsnorm(kv[:, :, 2], qk_norm[2]), kv[:, :, 3]], axis=2)
    win_rows = jnp.stack([rmsnorm(kv[:, :, 4], qk_norm[3]), kv[:, :, 5]], axis=2)
    gates = jax.nn.sigmoid(nsa_g.astype(f32)).reshape(b, L, KV_HEADS, Q_PER_KV, 3)

    def compress_kv(k_rows, v_rows):
        kc = rmsnorm(compress_rows(k_rows, cmp_pe[0], cmp_w1[0], cmp_w2[0]), qk_norm[1])
        vc = compress_rows(v_rows, cmp_pe[1], cmp_w1[1], cmp_w2[1])
        return kc, vc

    o, new_win = nsa_fn(q, gates, paged_rows, win_rows, compress_kv, bias_tab)
    gm = jax.nn.sigmoid(merge_g.astype(f32))
    y_a = (y_ssm @ w_branch_ssm).astype(f32)
    y_b = (o.reshape(b, L, D_ATTN).astype(x.dtype) @ w_branch_attn).astype(f32)
    mix = (gm[..., :D_MODEL] * y_a + gm[..., D_MODEL:] * y_b).astype(x.dtype) @ w_out
    x = x + mix
    x = x + 0.5 * swiglu_ffn(rmsnorm(x, ffn2_norm), ffn2_w_in, ffn2_w_out)
    return x, paged_rows, new_win, new_ssm, new_conv


def setup_inputs(seed: int = 0) -> dict:
    key = jax.random.key(seed)
    keys = jax.random.split(key, 32)
    f32 = jnp.float32

    def nrm(i, shape, scale):
        return scale * jax.random.normal(keys[i], shape, f32)

    n_pages = PAST_LEN // PAGE_SIZE
    n_used = DEC_BATCH * n_pages
    n_pool = n_used + (n_used + 3) // 4
    w_buf = min(WINDOW, PAST_LEN)
    page_table = jax.random.permutation(keys[6], n_pool)[:n_used].reshape(DEC_BATCH, n_pages).astype(jnp.int32)
    dt0 = jnp.exp(jax.random.uniform(keys[15], (DEPTH, M_HEADS), f32, math.log(1e-3), math.log(1e-1)))
    return {
        'x_prompt': nrm(0, (BATCH, SEQ, D_MODEL), 1.0),
        'x_sample': nrm(1, (DEC_BATCH, DEC_SEQ, D_MODEL), 1.0),
        'cache_kv': nrm(2, (DEPTH, n_pool, PAGE_SIZE, 4, KV_HEADS, HEAD_DIM), 1.0),
        'cache_win': nrm(3, (DEPTH, DEC_BATCH, w_buf, 2, KV_HEADS, HEAD_DIM), 1.0),
        'state_ssm': nrm(4, (DEPTH, DEC_BATCH, M_HEADS, M_HEADDIM, D_STATE), 0.5),
        'state_conv': nrm(5, (DEPTH, DEC_BATCH, D_CONV - 1, CONV_DIM), 1.0),
        'page_table': page_table,
        'rel_bias': nrm(7, (NUM_BUCKETS, Q_HEADS), 0.5),
        'ffn1_norm': 1.0 + nrm(8, (DEPTH, D_MODEL), 0.02),
        'ffn1_w_in': nrm(9, (DEPTH, D_MODEL, 2 * D_FF), D_MODEL ** -0.5),
        'ffn1_w_out': nrm(10, (DEPTH, D_FF, D_MODEL), D_FF ** -0.5),
        'mix_norm': 1.0 + nrm(11, (DEPTH, D_MODEL), 0.02),
        'w_in_proj': nrm(12, (DEPTH, D_MODEL, D_PROJ), D_MODEL ** -0.5),
        'conv_w': nrm(13, (DEPTH, D_CONV, CONV_DIM), D_CONV ** -0.5),
        'conv_b': nrm(14, (DEPTH, CONV_DIM), 0.02),
        'dt_bias': dt0 + jnp.log(-jnp.expm1(-dt0)),
        'a_log': jnp.log(jax.random.uniform(keys[16], (DEPTH, M_HEADS), f32, 1.0, 16.0)),
        'd_skip': 1.0 + nrm(17, (DEPTH, M_HEADS), 0.02),
        'ssm_norm': 1.0 + nrm(18, (DEPTH, D_INNER), 0.02),
        'qk_norm': 1.0 + nrm(19, (DEPTH, 4, HEAD_DIM), 0.02),
        'cmp_pe': nrm(20, (DEPTH, 2, CMP_BLOCK, HEAD_DIM), 0.1),
        'cmp_w1': nrm(21, (DEPTH, 2, CMP_BLOCK, HEAD_DIM, CMP_HIDDEN), (CMP_BLOCK * HEAD_DIM) ** -0.5),
        'cmp_w2': nrm(22, (DEPTH, 2, CMP_HIDDEN, HEAD_DIM), CMP_HIDDEN ** -0.5),
        'w_branch_ssm': nrm(23, (DEPTH, D_INNER, D_MODEL), D_INNER ** -0.5),
        'w_branch_attn': nrm(24, (DEPTH, D_ATTN, D_MODEL), D_ATTN ** -0.5),
        'w_out': nrm(25, (DEPTH, D_MODEL, D_MODEL), D_MODEL ** -0.5),
        'ffn2_norm': 1.0 + nrm(26, (DEPTH, D_MODEL), 0.02),
        'ffn2_w_in': nrm(27, (DEPTH, D_MODEL, 2 * D_FF), D_MODEL ** -0.5),
        'ffn2_w_out': nrm(28, (DEPTH, D_FF, D_MODEL), D_FF ** -0.5),
    }


def reference(x_prompt, x_sample, cache_kv, cache_win, state_ssm, state_conv, page_table, rel_bias,
              ffn1_norm, ffn1_w_in, ffn1_w_out, mix_norm, w_in_proj, conv_w, conv_b, dt_bias, a_log,
              d_skip, ssm_norm, qk_norm, cmp_pe, cmp_w1, cmp_w2, w_branch_ssm, w_branch_attn, w_out,
              ffn2_norm, ffn2_w_in, ffn2_w_out):
    bias_tab = rel_bias.astype(jnp.float32).reshape(NUM_BUCKETS, KV_HEADS, Q_PER_KV)
    xp = x_prompt
    xs = x_sample
    kv_p, win_p, ssm_p, conv_p = [], [], [], []
    kv_s, win_s, ssm_s, conv_s = [], [], [], []
    for l in range(DEPTH):
        lw = (ffn1_norm[l], ffn1_w_in[l], ffn1_w_out[l], mix_norm[l], w_in_proj[l], conv_w[l], conv_b[l],
              dt_bias[l], a_log[l], d_skip[l], ssm_norm[l], qk_norm[l], cmp_pe[l], cmp_w1[l], cmp_w2[l],
              w_branch_ssm[l], w_branch_attn[l], w_out[l], ffn2_norm[l], ffn2_w_in[l], ffn2_w_out[l])
        zero_ssm = jnp.zeros((xp.shape[0], M_HEADS, M_HEADDIM, D_STATE), state_ssm.dtype)
        zero_conv = jnp.zeros((xp.shape[0], D_CONV - 1, CONV_DIM), xp.dtype)
        xp, r_kv, r_win, r_ssm, r_conv = layer_forward(xp, zero_ssm, zero_conv, nsa_prompt, bias_tab, lw)
        kv_p.append(r_kv)
        win_p.append(r_win)
        ssm_p.append(r_ssm)
        conv_p.append(r_conv)
        nsa_s = functools.partial(nsa_sample, cache_kv_l=cache_kv[l], page_table=page_table,
                                  cache_win_l=cache_win[l])
        xs, r_kv, r_win, r_ssm, r_conv = layer_forward(xs, state_ssm[l], state_conv[l], nsa_s, bias_tab, lw)
        kv_s.append(r_kv)
        win_s.append(r_win)
        ssm_s.append(r_ssm)
        conv_s.append(r_conv)
    return (xp, xs, jnp.stack(kv_p), jnp.stack(win_p), jnp.stack(ssm_p), jnp.stack(conv_p),
            jnp.stack(kv_s), jnp.stack(win_s), jnp.stack(ssm_s), jnp.stack(conv_s))
```

```python
import functools
import math

import jax
import jax.numpy as jnp
import numpy as np
from jax import lax
from jax.experimental import pallas as pl
from jax.experimental.pallas import tpu as pltpu

F32 = jnp.float32
BF16 = jnp.bfloat16
HIGHEST = lax.Precision.HIGHEST

D_MODEL = 1024
D_FF = 2816
D_INNER = 2048
M_HEADDIM = 64
M_HEADS = 32
M_GROUPS = 4
D_STATE = 128
D_CONV = 4
CONV_DIM = D_INNER + 2 * M_GROUPS * D_STATE
HEAD_DIM = 64
Q_HEADS = 16
KV_HEADS = 4
Q_PER_KV = 4
D_ATTN = Q_HEADS * HEAD_DIM
KV_W = KV_HEADS * HEAD_DIM
CMP_BLOCK = 32
CMP_HIDDEN = 128
SEL_BLOCK = 64
SEL_TOPK = 16
WINDOW = 512
NUM_BUCKETS = 32
MAX_DISTANCE = 2048
PAGE_SIZE = 128
EPS = 1e-6
NEG = -1e30

LANES = 128
SSD_CL = 128
TQ = 128
TKC = 256
VMEM_LIMIT = 56 * 1024 * 1024
SMALL_W = 128
NSA_G_OFF = M_HEADS


def _cparams(sem):
    return pltpu.CompilerParams(dimension_semantics=sem, vmem_limit_bytes=VMEM_LIMIT)


def _const_spec(shape):
    nd = len(shape)
    return pl.BlockSpec(shape, lambda *a: (0,) * nd, pipeline_mode=pl.Buffered(1))


def _nt(a, b, precision=None):
    return lax.dot_general(a, b, (((1,), (1,)), ((), ())), preferred_element_type=F32, precision=precision)


def _mm(a, b, precision=None):
    return jnp.dot(a, b, preferred_element_type=F32, precision=precision)


def _rms_rows(x, g):
    return x * lax.rsqrt(jnp.mean(x * x, axis=-1, keepdims=True) + EPS) * g


def _silu(x):
    return x * jax.nn.sigmoid(x)


def _head_rms(a, gain2):
    lane = lax.broadcasted_iota(jnp.int32, (1, LANES), 1)
    lo_mask = lane < HEAD_DIM
    outs = []
    for c in range(a.shape[1] // LANES):
        blk = a[:, c * LANES:(c + 1) * LANES]
        sq = blk * blk
        lo = jnp.sum(jnp.where(lo_mask, sq, 0.0), axis=-1, keepdims=True)
        hi = jnp.sum(jnp.where(lo_mask, 0.0, sq), axis=-1, keepdims=True)
        r = jnp.where(lo_mask, lax.rsqrt(lo / HEAD_DIM + EPS), lax.rsqrt(hi / HEAD_DIM + EPS))
        outs.append(blk * r * gain2)
    return outs[0] if len(outs) == 1 else jnp.concatenate(outs, axis=1)


def _ffn_body(x_ref, g_ref, wg_ref, wu_ref, wo_ref, o_ref):
    x = x_ref[...]
    hb = _rms_rows(x, g_ref[...]).astype(BF16)
    gate = _mm(hb, wg_ref[...])
    up = _mm(hb, wu_ref[...])
    act = (_silu(gate) * up).astype(BF16)
    o_ref[...] = x + 0.5 * _mm(act, wo_ref[...])


def _ffn(x2d, norm_g, w_in, w_out, tm):
    n = x2d.shape[0]
    wg = w_in[:, :D_FF].astype(BF16)
    wu = w_in[:, D_FF:].astype(BF16)
    wo = w_out.astype(BF16)
    return pl.pallas_call(
        _ffn_body,
        out_shape=jax.ShapeDtypeStruct((n, D_MODEL), F32),
        grid=(n // tm,),
        in_specs=[pl.BlockSpec((tm, D_MODEL), lambda i: (i, 0)),
                  _const_spec((1, D_MODEL)),
                  _const_spec((D_MODEL, D_FF)), _const_spec((D_MODEL, D_FF)), _const_spec((D_FF, D_MODEL))],
        out_specs=pl.BlockSpec((tm, D_MODEL), lambda i: (i, 0)),
        compiler_params=_cparams(("arbitrary",)),
        name="ffn",
    )(x2d, norm_g.reshape(1, D_MODEL), wg, wu, wo)


def _inproj_body(x_ref, g_ref, w_ref, qkn_ref, z_ref, xbc_ref, q_ref, pg_ref, win_ref, mg_ref, sm_ref):
    hb = _rms_rows(x_ref[...], g_ref[...]).astype(BF16)
    o = 0
    z_ref[...] = _mm(hb, w_ref[:, o:o + D_INNER]); o += D_INNER
    xbc_ref[...] = _mm(hb, w_ref[:, o:o + CONV_DIM]); o += CONV_DIM
    q = _mm(hb, w_ref[:, o:o + D_ATTN]); o += D_ATTN
    q_ref[...] = _head_rms(q, qkn_ref[0:1, :] * (HEAD_DIM ** -0.5))
    kv = _mm(hb, w_ref[:, o:o + 6 * KV_W]); o += 6 * KV_W
    pg_ref[:, 0:2 * KV_W] = kv[:, 0:2 * KV_W]
    pg_ref[:, 2 * KV_W:3 * KV_W] = _head_rms(kv[:, 2 * KV_W:3 * KV_W], qkn_ref[2:3, :])
    pg_ref[:, 3 * KV_W:4 * KV_W] = kv[:, 3 * KV_W:4 * KV_W]
    win_ref[:, 0:KV_W] = _head_rms(kv[:, 4 * KV_W:5 * KV_W], qkn_ref[3:4, :])
    win_ref[:, KV_W:2 * KV_W] = kv[:, 5 * KV_W:6 * KV_W]
    mg_ref[...] = _mm(hb, w_ref[:, o:o + 2 * D_MODEL]); o += 2 * D_MODEL
    sm_ref[...] = _mm(hb, w_ref[:, o:o + SMALL_W])


def _pack_inproj_weight(w):
    c = np.cumsum([0, D_INNER, CONV_DIM, M_HEADS, D_ATTN, 6 * KV_W, 3 * Q_HEADS, 2 * D_MODEL])
    z, xbc, dt, q, kv, ng, mg = [w[:, c[i]:c[i + 1]] for i in range(7)]
    pad = jnp.zeros((D_MODEL, SMALL_W - M_HEADS - 3 * Q_HEADS), w.dtype)
    return jnp.concatenate([z, xbc, q, kv, mg, dt, ng, pad], axis=1).astype(BF16)


def _inproj(x2d, norm_g, w_packed, qk_norm, tm):
    n = x2d.shape[0]
    wtot = w_packed.shape[1]
    qkn2 = jnp.concatenate([qk_norm, qk_norm], axis=1)
    widths = (D_INNER, CONV_DIM, D_ATTN, 4 * KV_W, 2 * KV_W, 2 * D_MODEL, SMALL_W)
    return pl.pallas_call(
        _inproj_body,
        out_shape=[jax.ShapeDtypeStruct((n, w), F32) for w in widths],
        grid=(n // tm,),
        in_specs=[pl.BlockSpec((tm, D_MODEL), lambda i: (i, 0)),
                  _const_spec((1, D_MODEL)), _const_spec((D_MODEL, wtot)), _const_spec((4, LANES))],
        out_specs=[pl.BlockSpec((tm, w), lambda i: (i, 0)) for w in widths],
        compiler_params=_cparams(("arbitrary",)),
        name="inproj",
    )(x2d, norm_g.reshape(1, D_MODEL), w_packed, qkn2)


def _split_bf16(v):
    hi = v.astype(BF16)
    lo = (v - hi.astype(F32)).astype(BF16)
    return hi, lo


def _ssd_body(lv, xbc_ref, z_ref, sm_ref, conv0_ref, ssm0_ref, cw_ref, cb_ref, dtb_ref, alog_ref, dsk_ref,
              nw_ref, y_ref, convo_ref, ssmo_ref, xp_ref, st_ref):
    cl = SSD_CL
    c = pl.program_id(1)
    nchunk = pl.num_programs(1)
    hp = M_HEADS // M_GROUPS * M_HEADDIM

    @pl.when(c == 0)
    def _():
        xp_ref[...] = jnp.zeros(xp_ref.shape, F32)
        xp_ref[8 - (D_CONV - 1):8, :] = conv0_ref[0]
        st_ref[...] = ssm0_ref[0].reshape(M_HEADS * M_HEADDIM, D_STATE).T

    xp_ref[8:8 + lv, :] = xbc_ref[0]
    conv = cb_ref[...]
    for k in range(D_CONV):
        conv = conv + xp_ref[pl.ds(8 - (D_CONV - 1) + k, cl), :] * cw_ref[k:k + 1, :]
    tail = xp_ref[8 + lv - (D_CONV - 1):8 + lv, :]
    convo_ref[0] = tail
    xp_ref[8 - (D_CONV - 1):8, :] = tail
    xc = _silu(conv)
    xs = xc[:, :D_INNER]

    sm = sm_ref[0]
    z = z_ref[0]
    if lv < cl:
        sm = jnp.concatenate([sm, jnp.zeros((cl - lv, SMALL_W), F32)], axis=0)
        z = jnp.concatenate([z, jnp.zeros((cl - lv, D_INNER), F32)], axis=0)
    xdt_in = sm + dtb_ref[...]
    dt = jnp.maximum(xdt_in, 0.0) + jnp.log1p(jnp.exp(-jnp.abs(xdt_in)))
    row = lax.broadcasted_iota(jnp.int32, (cl, 1), 0)
    dt = jnp.where(row < lv, dt, 0.0)
    a_neg = -jnp.exp(alog_ref[...])
    da = dt * a_neg
    ri = lax.broadcasted_iota(jnp.int32, (cl, cl), 0)
    ci = lax.broadcasted_iota(jnp.int32, (cl, cl), 1)
    tril = ri >= ci
    acs = _mm(tril.astype(F32), da, precision=HIGHEST)
    alast = acs[cl - 1:cl, :]
    eye = (lax.broadcasted_iota(jnp.int32, (LANES, LANES), 0)
           == lax.broadcasted_iota(jnp.int32, (LANES, LANES), 1)).astype(F32)
    acs_t = _nt(eye, acs, precision=HIGHEST)

    ej = lax.broadcasted_iota(jnp.int32, (LANES, D_INNER), 0)
    el = lax.broadcasted_iota(jnp.int32, (LANES, D_INNER), 1)
    expand = (ej == el // M_HEADDIM).astype(BF16)
    stack = jnp.concatenate([dt, jnp.exp(acs), dt * jnp.exp(alast - acs)], axis=0)
    s_hi, s_lo = _split_bf16(stack)
    ex = _mm(s_hi, expand) + _mm(s_lo, expand)
    dt_e = ex[0:cl]
    es_e = ex[cl:2 * cl]
    w_e = ex[2 * cl:3 * cl]
    xdt = (xs * dt_e).astype(BF16)
    xw = (xs * w_e).astype(BF16)
    chunk_decay = es_e[cl - 1:cl, :]

    lane = lax.broadcasted_iota(jnp.int32, (1, LANES), 1)
    lo_mask = lane < M_HEADDIM
    eye_b = eye.astype(BF16)
    y_parts = []
    for g in range(M_GROUPS):
        bg = xc[:, D_INNER + g * D_STATE:D_INNER + (g + 1) * D_STATE].astype(BF16)
        cg = xc[:, D_INNER + (M_GROUPS + g) * D_STATE:D_INNER + (M_GROUPS + g + 1) * D_STATE].astype(BF16)
        cb = _nt(cg, bg)
        st_g = st_ref[:, g * hp:(g + 1) * hp]
        y_off = _mm(cg, st_g.astype(BF16)) * es_e[:, g * hp:(g + 1) * hp]
        bg_t = _nt(eye_b, bg).astype(BF16)
        st_ref[:, g * hp:(g + 1) * hp] = (st_g * chunk_decay[:, g * hp:(g + 1) * hp]
                                          + _mm(bg_t, xw[:, g * hp:(g + 1) * hp]))
        for jj in range(hp // LANES):
            col0 = g * hp + jj * LANES
            ms = []
            for j in (2 * (col0 // LANES), 2 * (col0 // LANES) + 1):
                diff = acs[:, j:j + 1] - acs_t[j:j + 1, :]
                dec = jnp.exp(jnp.where(tril, diff, -jnp.inf))
                ms.append((cb * dec).astype(BF16))
            x2 = xdt[:, col0:col0 + LANES]
            zero = jnp.zeros_like(x2)
            rhs = jnp.concatenate([jnp.where(lo_mask, x2, zero), jnp.where(lo_mask, zero, x2)], axis=0)
            y_parts.append(_mm(jnp.concatenate(ms, axis=1), rhs) + y_off[:, jj * LANES:(jj + 1) * LANES])
    y = jnp.concatenate(y_parts, axis=1)
    y = (y + dsk_ref[...] * xs) * _silu(z)
    outs = []
    for g in range(M_GROUPS):
        yg = y[:, g * hp:(g + 1) * hp]
        outs.append(yg * lax.rsqrt(jnp.mean(yg * yg, axis=-1, keepdims=True) + EPS))
    y = jnp.concatenate(outs, axis=1) * nw_ref[...]
    y_ref[0] = y[:lv].astype(BF16)

    @pl.when(c == nchunk - 1)
    def _():
        ssmo_ref[0] = st_ref[...].T.reshape(M_HEADS, M_HEADDIM, D_STATE)


def _ssd(xbc, z, small, conv0, ssm0, conv_w, conv_b, dt_bias, a_log, d_skip, ssm_norm):
    b, L = xbc.shape[:2]
    lv = SSD_CL if L % SSD_CL == 0 else L
    assert lv == SSD_CL or (L < SSD_CL and L % 8 == 0)
    nchunk = L // lv
    pad = lambda v: jnp.pad(v.astype(F32), (0, SMALL_W - M_HEADS)).reshape(1, SMALL_W)
    rep = lambda v: jnp.repeat(v.astype(F32), M_HEADDIM).reshape(1, D_INNER)
    return pl.pallas_call(
        functools.partial(_ssd_body, lv),
        out_shape=[jax.ShapeDtypeStruct((b, L, D_INNER), BF16),
                   jax.ShapeDtypeStruct((b, D_CONV - 1, CONV_DIM), F32),
                   jax.ShapeDtypeStruct((b, M_HEADS, M_HEADDIM, D_STATE), F32)],
        grid=(b, nchunk),
        in_specs=[pl.BlockSpec((1, lv, CONV_DIM), lambda i, c: (i, c, 0)),
                  pl.BlockSpec((1, lv, D_INNER), lambda i, c: (i, c, 0)),
                  pl.BlockSpec((1, lv, SMALL_W), lambda i, c: (i, c, 0)),
                  pl.BlockSpec((1, D_CONV - 1, CONV_DIM), lambda i, c: (i, 0, 0)),
                  pl.BlockSpec((1, M_HEADS, M_HEADDIM, D_STATE), lambda i, c: (i, 0, 0, 0)),
                  _const_spec((D_CONV, CONV_DIM)), _const_spec((1, CONV_DIM)),
                  _const_spec((1, SMALL_W)), _const_spec((1, SMALL_W)),
                  _const_spec((1, D_INNER)), _const_spec((1, D_INNER))],
        out_specs=[pl.BlockSpec((1, lv, D_INNER), lambda i, c: (i, c, 0)),
                   pl.BlockSpec((1, D_CONV - 1, CONV_DIM), lambda i, c: (i, 0, 0)),
                   pl.BlockSpec((1, M_HEADS, M_HEADDIM, D_STATE), lambda i, c: (i, 0, 0, 0))],
        scratch_shapes=[pltpu.VMEM((8 + SSD_CL, CONV_DIM), F32), pltpu.VMEM((D_STATE, D_INNER), F32)],
        compiler_params=_cparams(("arbitrary", "arbitrary")),
        name="ssd",
    )(xbc, z, small, conv0, ssm0, conv_w, conv_b.reshape(1, CONV_DIM), pad(dt_bias), pad(a_log),
      rep(d_skip), ssm_norm.reshape(1, D_INNER))


def _relbias_body(tabt_ref, f_ref):
    nd = f_ref.shape[1]
    d = lax.broadcasted_iota(jnp.int32, (1, nd), 1)
    max_exact = NUM_BUCKETS // 2
    df = jnp.maximum(d, 1).astype(F32)
    large = max_exact + (jnp.log(df / max_exact) / math.log(MAX_DISTANCE / max_exact)
                         * (NUM_BUCKETS - max_exact)).astype(jnp.int32)
    large = jnp.minimum(large, NUM_BUCKETS - 1)
    bucket = jnp.where(d < max_exact, d, large)
    acc = jnp.zeros(f_ref.shape, F32)
    for bkt in range(NUM_BUCKETS):
        acc = acc + jnp.where(bucket == bkt, tabt_ref[:, bkt:bkt + 1], 0.0)
    f_ref[...] = acc


def _relbias(rel_bias, nd):
    return pl.pallas_call(
        _relbias_body,
        out_shape=jax.ShapeDtypeStruct((Q_HEADS, nd), F32),
        name="relbias",
    )(rel_bias.astype(F32).T)


def _prompt_bias_tables(f, L):
    i = np.arange(TQ)[None, :, None]
    j = np.arange(TQ)[None, None, :]
    take = lambda dist: f[:, np.clip(dist, 0, f.shape[1] - 1)]
    dist_s = (np.arange(L // TQ + 1)[:, None, None] - 1) * TQ + i - j
    tab_s = jnp.where(dist_s >= 0, take(dist_s), NEG)
    dist_w = (np.arange(WINDOW // TQ + 3)[:, None, None] - 1) * TQ + i - j
    tab_w = jnp.where((dist_w >= 0) & (dist_w <= WINDOW), take(dist_w), NEG)
    n_c = L // CMP_BLOCK
    dist_c = np.arange(L)[:, None] - (np.arange(n_c)[None, :] * CMP_BLOCK + CMP_BLOCK - 1)
    tab_c = jnp.where(dist_c >= 0, take(dist_c), NEG)
    shp = lambda t: t.reshape((KV_HEADS, Q_PER_KV) + t.shape[1:])
    return shp(tab_s), shp(tab_w), shp(tab_c)


def _compress_weights(cmp_pe, cmp_w1, cmp_w2):
    eye = jnp.eye(KV_HEADS, dtype=F32)
    w1 = jnp.einsum("hg,cjdf->cjhdgf", eye, cmp_w1).reshape(2, CMP_BLOCK, KV_W, KV_HEADS * CMP_HIDDEN)
    w2 = jnp.einsum("hg,cfd->chfgd", eye, cmp_w2).reshape(2, KV_HEADS * CMP_HIDDEN, KV_W)
    pe = jnp.tile(cmp_pe, (1, 1, KV_HEADS))
    return pe.astype(F32), w1.astype(BF16), w2.astype(BF16)


def _compress_rows(lo_ref, hi_ref, nblk, pe_ref, w1_ref, w2_ref, c):
    acc = jnp.zeros((nblk, KV_HEADS * CMP_HIDDEN), F32)
    for j in range(CMP_BLOCK):
        take = pl.ds(j, nblk, stride=CMP_BLOCK)
        xj = jnp.concatenate([lo_ref[take, :], hi_ref[take, :]], axis=1) + pe_ref[c, j:j + 1, :]
        acc = acc + _mm(xj.astype(BF16), w1_ref[c, j])
    return _mm(_silu(acc).astype(BF16), w2_ref[c])


def _compress_prompt_body(k0_ref, k1_ref, v0_ref, v1_ref, pe_ref, w1_ref, w2_ref, kn_ref, kc_ref, vc_ref):
    nblk = kc_ref.shape[1]
    kc_ref[0] = _head_rms(_compress_rows(k0_ref.at[0], k1_ref.at[0], nblk, pe_ref, w1_ref, w2_ref, 0), kn_ref[...])
    vc_ref[0] = _compress_rows(v0_ref.at[0], v1_ref.at[0], nblk, pe_ref, w1_ref, w2_ref, 1)


def _compress_prompt(paged, pe, w1, w2, kn2):
    b, L = paged.shape[:2]
    n_c = L // CMP_BLOCK
    col = lambda c: pl.BlockSpec((1, L, LANES), lambda i: (i, 0, c))
    return pl.pallas_call(
        _compress_prompt_body,
        out_shape=[jax.ShapeDtypeStruct((b, n_c, KV_W), F32)] * 2,
        grid=(b,),
        in_specs=[col(0), col(1), col(2), col(3),
                  _const_spec(pe.shape), _const_spec(w1.shape), _const_spec(w2.shape), _const_spec((1, LANES))],
        out_specs=[pl.BlockSpec((1, n_c, KV_W), lambda i: (i, 0, 0))] * 2,
        compiler_params=_cparams(("arbitrary",)),
        name="compress_prompt",
    )(paged, paged, paged, paged, pe, w1, w2, kn2)


def _online_softmax_step(s, carry, v):
    m, l, acc = carry
    m_new = jnp.maximum(m, jnp.max(s, axis=-1, keepdims=True))
    alpha = jnp.exp(m - m_new)
    p = jnp.exp(s - m_new)
    return (m_new, alpha * l + jnp.sum(p, axis=-1, keepdims=True), alpha * acc + _mm(p.astype(BF16), v))


def _select_blocks(imp, t, n_blk):
    big = 1e30
    j = lax.broadcasted_iota(jnp.int32, (1, imp.shape[1]), 1)
    cur = t // SEL_BLOCK
    forced = (j == 0) | (j == cur) | (j == cur - 1)
    causal = j * SEL_BLOCK <= t
    score = jnp.where(forced, big, jnp.where(causal, imp, -big))
    rank = jnp.zeros(imp.shape, F32)
    for i in range(n_blk):
        ci = score[:, i:i + 1]
        ahead = (ci > score) | ((ci == score) & (i < j))
        rank = rank + jnp.where(ahead, 1.0, 0.0)
    return jnp.where(rank < SEL_TOPK, 1.0, 0.0)


def _nsa_prompt_body(n_sel, q_ref, ks_ref, vs_ref, kw_ref, vw_ref, kc_ref, vc_ref, sm_ref, tabs_ref, tabw_ref,
                     tabc_ref, o_ref):
    h = pl.program_id(1)
    qi = pl.program_id(2)
    rows = Q_PER_KV * TQ
    lane = lax.broadcasted_iota(jnp.int32, (1, LANES), 1)
    mine = (lane // HEAD_DIM) == (h % 2)
    qb = q_ref[0]
    q4 = jnp.concatenate([qb[:, g * HEAD_DIM:(g + 1) * HEAD_DIM] for g in range(Q_PER_KV)], axis=0)
    qpad = jnp.where(mine, jnp.concatenate([q4, q4], axis=1), 0.0).astype(BF16)

    n_c = kc_ref.shape[1]
    bias_c = tabc_ref[0].reshape(rows, n_c)
    s_c = _nt(qpad, kc_ref[0].astype(BF16)) + bias_c
    e_c = jnp.where(bias_c > 0.5 * NEG, jnp.exp(s_c - jnp.max(s_c, axis=-1, keepdims=True)), 0.0)
    p_c = e_c / jnp.maximum(jnp.sum(e_c, axis=-1, keepdims=True), 1e-30)
    o_c = _mm(p_c.astype(BF16), vc_ref[0].astype(BF16))

    imp_c = jnp.sum(p_c.reshape(Q_PER_KV, TQ, n_c), axis=0)
    ratio = SEL_BLOCK // CMP_BLOCK
    pair = (lax.broadcasted_iota(jnp.int32, (n_c, LANES), 0) // ratio
            == lax.broadcasted_iota(jnp.int32, (n_c, LANES), 1)).astype(F32)
    imp = _mm(imp_c, pair, precision=HIGHEST)
    t = qi * TQ + lax.broadcasted_iota(jnp.int32, (TQ, 1), 0)
    selm = _select_blocks(imp, t, n_sel).astype(BF16)

    def bias_tile(tab_ref, kc):
        i0 = qi - 2 * kc + 1
        return jnp.concatenate(
            [jnp.concatenate([tab_ref[0, g, i0], tab_ref[0, g, i0 - 1]], axis=1) for g in range(Q_PER_KV)], axis=0)

    def sel_step(kc, carry):
        start = pl.multiple_of(kc * TKC, TKC)
        k = ks_ref[0, pl.ds(start, TKC), :].astype(BF16)
        v = vs_ref[0, pl.ds(start, TKC), :].astype(BF16)
        s = _nt(qpad, k) + bias_tile(tabs_ref, kc)
        blk = lax.broadcasted_iota(jnp.int32, (LANES, TKC), 0)
        key = lax.broadcasted_iota(jnp.int32, (LANES, TKC), 1)
        spread = (blk == kc * (TKC // SEL_BLOCK) + key // SEL_BLOCK).astype(BF16)
        keep = _mm(selm, spread)
        s = s.reshape(Q_PER_KV, TQ, TKC) + jnp.where(keep > 0.5, 0.0, NEG)[None]
        return _online_softmax_step(s.reshape(rows, TKC), carry, v)

    def win_step(kc, carry):
        start = pl.multiple_of(kc * TKC, TKC)
        k = kw_ref[0, pl.ds(start, TKC), :].astype(BF16)
        v = vw_ref[0, pl.ds(start, TKC), :].astype(BF16)
        return _online_softmax_step(_nt(qpad, k) + bias_tile(tabw_ref, kc), carry, v)

    init = (jnp.full((rows, 1), NEG, F32), jnp.zeros((rows, 1), F32), jnp.zeros((rows, LANES), F32))
    last = qi // 2 + 1
    _, l_s, a_s = lax.fori_loop(0, last, sel_step, init)
    _, l_w, a_w = lax.fori_loop(jnp.maximum(qi - WINDOW // TQ, 0) // 2, last, win_step, init)
    o_s = a_s / jnp.maximum(l_s, 1e-30)
    o_w = a_w / jnp.maximum(l_w, 1e-30)

    src = lax.broadcasted_iota(jnp.int32, (SMALL_W, LANES), 0)
    dst = lax.broadcasted_iota(jnp.int32, (SMALL_W, LANES), 1)
    pick = ((src == NSA_G_OFF + h * (3 * Q_PER_KV) + dst) & (dst < 3 * Q_PER_KV)).astype(F32)
    gates = jax.nn.sigmoid(_mm(sm_ref[0], pick, precision=HIGHEST))
    even = (h % 2) == 0
    outs = []
    for g in range(Q_PER_KV):
        rs = slice(g * TQ, (g + 1) * TQ)
        mix = (gates[:, 3 * g:3 * g + 1] * o_c[rs] + gates[:, 3 * g + 1:3 * g + 2] * o_s[rs]
               + gates[:, 3 * g + 2:3 * g + 3] * o_w[rs])
        outs.append(jnp.where(even, mix[:, :HEAD_DIM], mix[:, HEAD_DIM:]))
    o_ref[0] = jnp.concatenate(outs, axis=1).astype(BF16)


def _nsa_prompt(qn, paged, win, kc, vc, small, tab_s, tab_w, tab_c):
    b, L = qn.shape[:2]
    assert L % TKC == 0
    n_c = L // CMP_BLOCK
    seq = lambda col: pl.BlockSpec((1, L, LANES), lambda i, h, q: (i, 0, col(h)))
    cmp = pl.BlockSpec((1, n_c, LANES), lambda i, h, q: (i, 0, h // 2))
    return pl.pallas_call(
        functools.partial(_nsa_prompt_body, L // SEL_BLOCK),
        out_shape=jax.ShapeDtypeStruct((b, L, D_ATTN), BF16),
        grid=(b, KV_HEADS, L // TQ),
        in_specs=[pl.BlockSpec((1, TQ, KV_W), lambda i, h, q: (i, q, h)),
                  seq(lambda h: 4 + h // 2), seq(lambda h: 6 + h // 2),
                  seq(lambda h: h // 2), seq(lambda h: 2 + h // 2),
                  cmp, cmp,
                  pl.BlockSpec((1, TQ, SMALL_W), lambda i, h, q: (i, q, 0)),
                  pl.BlockSpec((1,) + tab_s.shape[1:], lambda i, h, q: (h, 0, 0, 0, 0)),
                  pl.BlockSpec((1,) + tab_w.shape[1:], lambda i, h, q: (h, 0, 0, 0, 0)),
                  pl.BlockSpec((1, Q_PER_KV, TQ, n_c), lambda i, h, q: (h, 0, q, 0))],
        out_specs=pl.BlockSpec((1, TQ, KV_W), lambda i, h, q: (i, q, h)),
        compiler_params=_cparams(("arbitrary", "arbitrary", "arbitrary")),
        name="nsa_prompt",
    )(qn, paged, paged, win, win, kc, vc, small, tab_s, tab_w, tab_c)


def _merge_body(x_ref, y_ref, o_ref, mg_ref, wa_ref, wb_ref, wo_ref, out_ref):
    ya = _mm(y_ref[...], wa_ref[...])
    yb = _mm(o_ref[...], wb_ref[...])
    gm = jax.nn.sigmoid(mg_ref[...])
    mix = (gm[:, :D_MODEL] * ya + gm[:, D_MODEL:] * yb).astype(BF16)
    out_ref[...] = x_ref[...] + _mm(mix, wo_ref[...])


def _merge(x2d, y2d, o2d, mg, wa, wb, wo, tm):
    n = x2d.shape[0]
    row = lambda w: pl.BlockSpec((tm, w), lambda i: (i, 0))
    return pl.pallas_call(
        _merge_body,
        out_shape=jax.ShapeDtypeStruct((n, D_MODEL), F32),
        grid=(n // tm,),
        in_specs=[row(D_MODEL), row(D_INNER), row(D_ATTN), row(2 * D_MODEL),
                  _const_spec((D_INNER, D_MODEL)), _const_spec((D_ATTN, D_MODEL)), _const_spec((D_MODEL, D_MODEL))],
        out_specs=row(D_MODEL),
        compiler_params=_cparams(("arbitrary",)),
        name="merge",
    )(x2d, y2d, o2d, mg, wa.astype(BF16), wb.astype(BF16), wo.astype(BF16))


def _layer_prompt(x, lw, f_bias, tm):
    (ffn1_norm, ffn1_w_in, ffn1_w_out, mix_norm, w_packed, conv_w, conv_b, dt_bias, a_log, d_skip, ssm_norm,
     qk_norm, cmp_prep, w_branch_ssm, w_branch_attn, w_out, ffn2_norm, ffn2_w_in, ffn2_w_out) = lw
    b, L = x.shape[:2]
    n = b * L
    x1 = _ffn(x.reshape(n, D_MODEL), ffn1_norm, ffn1_w_in, ffn1_w_out, tm)
    z, xbc, qn, paged, win, mg, small = _inproj(x1, mix_norm, w_packed, qk_norm, tm)
    r3 = lambda a: a.reshape(b, L, a.shape[-1])
    conv0 = jnp.zeros((b, D_CONV - 1, CONV_DIM), F32)
    ssm0 = jnp.zeros((b, M_HEADS, M_HEADDIM, D_STATE), F32)
    y, new_conv, new_ssm = _ssd(r3(xbc), r3(z), r3(small), conv0, ssm0, conv_w, conv_b, dt_bias, a_log, d_skip,
                                ssm_norm)
    pe, w1, w2 = cmp_prep
    kn2 = jnp.concatenate([qk_norm[1], qk_norm[1]]).reshape(1, LANES)
    kc, vc = _compress_prompt(r3(paged), pe, w1, w2, kn2)
    tab_s, tab_w, tab_c = _prompt_bias_tables(f_bias, L)
    o = _nsa_prompt(r3(qn), r3(paged), r3(win), kc, vc, r3(small), tab_s, tab_w, tab_c)
    x2 = _merge(x1, y.reshape(n, D_INNER), o.reshape(n, D_ATTN), mg, w_branch_ssm, w_branch_attn, w_out, tm)
    x3 = _ffn(x2, ffn2_norm, ffn2_w_in, ffn2_w_out, tm)
    n_keep = min(WINDOW, L)
    new_kv = paged.reshape(b, L, 4, KV_HEADS, HEAD_DIM)
    new_win = r3(win)[:, L - n_keep:].reshape(b, n_keep, 2, KV_HEADS, HEAD_DIM)
    return x3.reshape(b, L, D_MODEL), new_kv, new_win, new_ssm, new_conv


PAGES_PER_STEP = 32


def _sample_qpad(qb):
    ls = qb.shape[0]
    blocks = []
    for hq in range(Q_HEADS):
        h = hq // Q_PER_KV
        parts = []
        if h > 0:
            parts.append(jnp.zeros((ls, h * HEAD_DIM), F32))
        parts.append(qb[:, hq * HEAD_DIM:(hq + 1) * HEAD_DIM])
        if h < KV_HEADS - 1:
            parts.append(jnp.zeros((ls, (KV_HEADS - 1 - h) * HEAD_DIM), F32))
        blocks.append(jnp.concatenate(parts, axis=1))
    return jnp.concatenate(blocks, axis=0).astype(BF16)


def _sample_head_piece(o, hq, ls):
    h = hq // Q_PER_KV
    return o[hq * ls:(hq + 1) * ls, h * HEAD_DIM:(h + 1) * HEAD_DIM]


def _chunk_copies(pt_ref, cache_ref, buf, sem, cols, width, bb, ii, sl, p):
    page = pt_ref[bb, ii * PAGES_PER_STEP + p]
    row0 = pl.multiple_of(p * PAGE_SIZE, PAGE_SIZE)
    return [pltpu.make_async_copy(cache_ref.at[page, :, pl.ds(c0, width)],
                                  buf.at[sl, k, pl.ds(row0, PAGE_SIZE), :], sem.at[sl])
            for k, c0 in enumerate(cols)]


def _stream_chunks(pt_ref, cache_ref, buf, sem, cols, width):
    b = pl.program_id(0)
    i = pl.program_id(1)
    nch = pl.num_programs(1)
    step = b * nch + i
    slot = step % 2

    def run(bb, ii, sl, start):
        def body(p, carry):
            for cp in _chunk_copies(pt_ref, cache_ref, buf, sem, cols, width, bb, ii, sl, p):
                cp.start() if start else cp.wait()
            return carry
        lax.fori_loop(0, PAGES_PER_STEP, body, 0)

    @pl.when(step == 0)
    def _():
        run(b, i, slot, True)

    nxt = step + 1

    @pl.when(nxt < pl.num_programs(0) * nch)
    def _():
        run(nxt // nch, nxt % nch, nxt % 2, True)

    run(b, i, slot, False)
    return slot


def _cmp_sample_body(past_len, pt_ref, cache_ref, q_ref, biasc_ref, pe_ref, w1_ref, w2_ref, kn_ref,
                     oc_ref, selm_ref, buf, sem, kc_s, vc_s):
    i = pl.program_id(1)
    slot = _stream_chunks(pt_ref, cache_ref, buf, sem, [k * LANES for k in range(4)], LANES)
    nblk = PAGES_PER_STEP * PAGE_SIZE // CMP_BLOCK
    rows0 = pl.multiple_of(i * nblk, nblk)
    kc_s[pl.ds(rows0, nblk), :] = _head_rms(
        _compress_rows(buf.at[slot, 0], buf.at[slot, 1], nblk, pe_ref, w1_ref, w2_ref, 0), kn_ref[...])
    vc_s[pl.ds(rows0, nblk), :] = _compress_rows(buf.at[slot, 2], buf.at[slot, 3], nblk, pe_ref, w1_ref, w2_ref, 1)

    @pl.when(i == pl.num_programs(1) - 1)
    def _():
        ls = q_ref.shape[1]
        n_c = kc_s.shape[0]
        s = _nt(_sample_qpad(q_ref[0]), kc_s[...].astype(BF16)) + biasc_ref[...]
        e = jnp.exp(s - jnp.max(s, axis=-1, keepdims=True))
        p = e / jnp.maximum(jnp.sum(e, axis=-1, keepdims=True), 1e-30)
        oc = _mm(p.astype(BF16), vc_s[...].astype(BF16))
        oc_ref[0] = jnp.concatenate([_sample_head_piece(oc, hq, ls) for hq in range(Q_HEADS)], axis=1)
        imp_c = jnp.sum(p.reshape(KV_HEADS, Q_PER_KV, ls, n_c), axis=1).reshape(KV_HEADS * ls, n_c)
        w = selm_ref.shape[2]
        pair = (lax.broadcasted_iota(jnp.int32, (n_c, w), 0) // (SEL_BLOCK // CMP_BLOCK)
                == lax.broadcasted_iota(jnp.int32, (n_c, w), 1)).astype(F32)
        imp = _mm(imp_c, pair, precision=HIGHEST)
        t = past_len + lax.broadcasted_iota(jnp.int32, (KV_HEADS * ls, 1), 0) % ls
        selm_ref[0] = _select_blocks(imp, t, past_len // SEL_BLOCK + 1)


def _nsa_sample_body(pt_ref, cache_ref, q_ref, selm_ref, oc_ref, sm_ref, pnew_ref, cwin_ref, wnew_ref,
                     biass_ref, biasw_ref, biasn_ref, o_ref, buf, sem, m_s, l_s, a_s):
    i = pl.program_id(1)
    slot = _stream_chunks(pt_ref, cache_ref, buf, sem, [2 * KV_W, 3 * KV_W], KV_W)
    ls = q_ref.shape[1]
    rows = Q_HEADS * ls
    tk = PAGES_PER_STEP * PAGE_SIZE
    qpad = _sample_qpad(q_ref[0])

    @pl.when(i == 0)
    def _():
        m_s[...] = jnp.full(m_s.shape, NEG, F32)
        l_s[...] = jnp.zeros(l_s.shape, F32)
        a_s[...] = jnp.zeros(a_s.shape, F32)

    sel = selm_ref[0]
    w = sel.shape[1]
    sel = jnp.broadcast_to(sel.reshape(KV_HEADS, 1, ls, w), (KV_HEADS, Q_PER_KV, ls, w)).reshape(rows, w)
    blk = lax.broadcasted_iota(jnp.int32, (w, tk), 0)
    key = lax.broadcasted_iota(jnp.int32, (w, tk), 1)
    spread = (blk == i * (tk // SEL_BLOCK) + key // SEL_BLOCK).astype(BF16)
    keep = _mm(sel.astype(BF16), spread)
    s = _nt(qpad, buf[slot, 0].astype(BF16)) + biass_ref[...] + jnp.where(keep > 0.5, 0.0, NEG)
    m, l, a = _online_softmax_step(s, (m_s[...], l_s[...], a_s[...]), buf[slot, 1].astype(BF16))
    m_s[...] = m
    l_s[...] = l
    a_s[...] = a

    @pl.when(i == pl.num_programs(1) - 1)
    def _():
        def pad_rows(x):
            return jnp.concatenate([x, jnp.zeros((LANES - ls, x.shape[1]), F32)], axis=0).astype(BF16)

        pn = pnew_ref[0]
        wn = wnew_ref[0]
        s_n = _nt(qpad, pad_rows(pn[:, 2 * KV_W:3 * KV_W])) + biasn_ref[...]
        _, l2, a2 = _online_softmax_step(s_n, (m, l, a), pad_rows(pn[:, 3 * KV_W:4 * KV_W]))
        o_s = a2 / jnp.maximum(l2, 1e-30)
        cw = cwin_ref[0]
        init = (jnp.full((rows, 1), NEG, F32), jnp.zeros((rows, 1), F32), jnp.zeros((rows, KV_W), F32))
        c1 = _online_softmax_step(_nt(qpad, cw[:, 0:KV_W].astype(BF16)) + biasw_ref[...], init,
                                  cw[:, KV_W:2 * KV_W].astype(BF16))
        s_wn = _nt(qpad, pad_rows(wn[:, 0:KV_W])) + biasn_ref[...]
        _, l3, a3 = _online_softmax_step(s_wn, c1, pad_rows(wn[:, KV_W:2 * KV_W]))
        o_w = a3 / jnp.maximum(l3, 1e-30)
        gates = jax.nn.sigmoid(sm_ref[0])
        oc = oc_ref[0]
        outs = []
        for hq in range(Q_HEADS):
            g0 = NSA_G_OFF + 3 * hq
            outs.append(gates[:, g0:g0 + 1] * oc[:, hq * HEAD_DIM:(hq + 1) * HEAD_DIM]
                        + gates[:, g0 + 1:g0 + 2] * _sample_head_piece(o_s, hq, ls)
                        + gates[:, g0 + 2:g0 + 3] * _sample_head_piece(o_w, hq, ls))
        o_ref[0] = jnp.concatenate(outs, axis=1).astype(BF16)


def _sample_bias_tables(f, past_len, ls, n_buf):
    flip_rows = lambda g: jnp.stack(g, axis=1).reshape(Q_HEADS * ls, -1)
    bias_s = flip_rows([jnp.flip(f[:, i + 1:i + 1 + past_len], axis=1) for i in range(ls)])
    n_c = past_len // CMP_BLOCK
    bias_c = flip_rows([jnp.flip(f[:, i + 1:i + 1 + CMP_BLOCK * n_c:CMP_BLOCK], axis=1) for i in range(ls)])
    bias_w = flip_rows([jnp.flip(f[:, i + 1:i + 1 + n_buf], axis=1) for i in range(ls)])
    ti = np.tile(np.arange(ls), Q_HEADS)[:, None]
    dist_w = n_buf + ti - np.arange(n_buf)[None, :]
    bias_w = jnp.where(dist_w <= WINDOW, bias_w, NEG)
    dist_n = ti - np.arange(LANES)[None, :]
    hq = np.repeat(np.arange(Q_HEADS), ls)[:, None]
    bias_n = jnp.where((dist_n >= 0) & (np.arange(LANES)[None, :] < ls), f[hq, np.clip(dist_n, 0, None)], NEG)
    return bias_s, bias_c, bias_w, bias_n


def _nsa_sample(qn, paged, win, small, cache_pages, page_table, cache_win, f_bias, cmp_prep, kn2):
    b, ls = qn.shape[:2]
    n_pages = page_table.shape[1]
    past_len = n_pages * PAGE_SIZE
    n_buf = cache_win.shape[1]
    assert n_pages % PAGES_PER_STEP == 0 and ls <= SEL_BLOCK and ls % 8 == 0 and n_buf == min(WINDOW, past_len)
    nch = n_pages // PAGES_PER_STEP
    tk = PAGES_PER_STEP * PAGE_SIZE
    n_c = past_len // CMP_BLOCK
    n_sel = past_len // SEL_BLOCK + 1
    w_sel = -(-n_sel // LANES) * LANES
    rows = Q_HEADS * ls
    pe, w1, w2 = cmp_prep
    bias_s, bias_c, bias_w, bias_n = _sample_bias_tables(f_bias, past_len, ls, n_buf)
    tok = lambda wd: pl.BlockSpec((1, ls, wd), lambda i, c, pt: (i, 0, 0))
    const = lambda shape: pl.BlockSpec(shape, lambda i, c, pt: (0,) * len(shape), pipeline_mode=pl.Buffered(1))
    o_c, selm = pl.pallas_call(
        functools.partial(_cmp_sample_body, past_len),
        out_shape=[jax.ShapeDtypeStruct((b, ls, D_ATTN), F32),
                   jax.ShapeDtypeStruct((b, KV_HEADS * ls, w_sel), F32)],
        grid_spec=pltpu.PrefetchScalarGridSpec(
            num_scalar_prefetch=1, grid=(b, nch),
            in_specs=[pl.BlockSpec(memory_space=pl.ANY), tok(D_ATTN), const((rows, n_c)),
                      const(pe.shape), const(w1.shape), const(w2.shape), const((1, LANES))],
            out_specs=[tok(D_ATTN), pl.BlockSpec((1, KV_HEADS * ls, w_sel), lambda i, c, pt: (i, 0, 0))],
            scratch_shapes=[pltpu.VMEM((2, 4, tk, LANES), F32), pltpu.SemaphoreType.DMA((2,)),
                            pltpu.VMEM((n_c, KV_W), F32), pltpu.VMEM((n_c, KV_W), F32)]),
        compiler_params=_cparams(("arbitrary", "arbitrary")),
        name="nsa_sample_compress",
    )(page_table, cache_pages, qn, bias_c, pe, w1, w2, kn2)
    return pl.pallas_call(
        _nsa_sample_body,
        out_shape=jax.ShapeDtypeStruct((b, ls, D_ATTN), BF16),
        grid_spec=pltpu.PrefetchScalarGridSpec(
            num_scalar_prefetch=1, grid=(b, nch),
            in_specs=[pl.BlockSpec(memory_space=pl.ANY), tok(D_ATTN),
                      pl.BlockSpec((1, KV_HEADS * ls, w_sel), lambda i, c, pt: (i, 0, 0)),
                      tok(D_ATTN), tok(SMALL_W), tok(4 * KV_W),
                      pl.BlockSpec((1, n_buf, 2 * KV_W), lambda i, c, pt: (i, 0, 0)), tok(2 * KV_W),
                      pl.BlockSpec((rows, tk), lambda i, c, pt: (0, c)),
                      const((rows, n_buf)), const((rows, LANES))],
            out_specs=tok(D_ATTN),
            scratch_shapes=[pltpu.VMEM((2, 2, tk, KV_W), F32), pltpu.SemaphoreType.DMA((2,)),
                            pltpu.VMEM((rows, 1), F32), pltpu.VMEM((rows, 1), F32), pltpu.VMEM((rows, KV_W), F32)]),
        compiler_params=_cparams(("arbitrary", "arbitrary")),
        name="nsa_sample",
    )(page_table, cache_pages, qn, selm, o_c, small, paged, cache_win, win, bias_s, bias_w, bias_n)


def _layer_sample(x, ssm0, conv0, cache_kv_l, page_table, cache_win_l, lw, f_bias):
    (ffn1_norm, ffn1_w_in, ffn1_w_out, mix_norm, w_packed, conv_w, conv_b, dt_bias, a_log, d_skip, ssm_norm,
     qk_norm, cmp_prep, w_branch_ssm, w_branch_attn, w_out, ffn2_norm, ffn2_w_in, ffn2_w_out) = lw
    b, ls = x.shape[:2]
    n = b * ls
    tm = n
    x1 = _ffn(x.reshape(n, D_MODEL), ffn1_norm, ffn1_w_in, ffn1_w_out, tm)
    z, xbc, qn, paged, win, mg, small = _inproj(x1, mix_norm, w_packed, qk_norm, tm)
    r3 = lambda a: a.reshape(b, ls, a.shape[-1])
    y, new_conv, new_ssm = _ssd(r3(xbc), r3(z), r3(small), conv0, ssm0, conv_w, conv_b, dt_bias, a_log, d_skip,
                                ssm_norm)
    kn2 = jnp.concatenate([qk_norm[1], qk_norm[1]]).reshape(1, LANES)
    n_pool = cache_kv_l.shape[0]
    cache_pages = cache_kv_l.reshape(n_pool, PAGE_SIZE, 4 * KV_W)
    n_buf = cache_win_l.shape[1]
    cwin = cache_win_l.reshape(b, n_buf, 2 * KV_W)
    o = _nsa_sample(r3(qn), r3(paged), r3(win), r3(small), cache_pages, page_table, cwin, f_bias, cmp_prep, kn2)
    x2 = _merge(x1, y.reshape(n, D_INNER), o.reshape(n, D_ATTN), mg, w_branch_ssm, w_branch_attn, w_out, tm)
    x3 = _ffn(x2, ffn2_norm, ffn2_w_in, ffn2_w_out, tm)
    n_keep = min(WINDOW, n_buf + ls)
    new_kv = paged.reshape(b, ls, 4, KV_HEADS, HEAD_DIM)
    win_all = jnp.concatenate([cwin, r3(win)], axis=1)
    new_win = win_all[:, n_buf + ls - n_keep:].reshape(b, n_keep, 2, KV_HEADS, HEAD_DIM)
    return x3.reshape(b, ls, D_MODEL), new_kv, new_win, new_ssm, new_conv


def _layer_weights(l, ffn1_norm, ffn1_w_in, ffn1_w_out, mix_norm, w_in_proj, conv_w, conv_b, dt_bias, a_log,
                   d_skip, ssm_norm, qk_norm, cmp_pe, cmp_w1, cmp_w2, w_branch_ssm, w_branch_attn, w_out,
                   ffn2_norm, ffn2_w_in, ffn2_w_out):
    return (ffn1_norm[l], ffn1_w_in[l], ffn1_w_out[l], mix_norm[l], _pack_inproj_weight(w_in_proj[l]), conv_w[l],
            conv_b[l], dt_bias[l], a_log[l], d_skip[l], ssm_norm[l], qk_norm[l],
            _compress_weights(cmp_pe[l], cmp_w1[l], cmp_w2[l]), w_branch_ssm[l], w_branch_attn[l], w_out[l],
            ffn2_norm[l], ffn2_w_in[l], ffn2_w_out[l])


def kernel(x_prompt, x_sample, cache_kv, cache_win, state_ssm, state_conv, page_table, rel_bias, ffn1_norm, ffn1_w_in, ffn1_w_out, mix_norm, w_in_proj, conv_w, conv_b, dt_bias, a_log, d_skip, ssm_norm, qk_norm, cmp_pe, cmp_w1, cmp_w2, w_branch_ssm, w_branch_attn, w_out, ffn2_norm, ffn2_w_in, ffn2_w_out):
    depth = ffn1_norm.shape[0]
    past_len = page_table.shape[1] * PAGE_SIZE
    n_dist = -(-(past_len + x_sample.shape[1] + TQ) // LANES) * LANES
    f_bias = _relbias(rel_bias, max(n_dist, x_prompt.shape[1]))
    xp, xs = x_prompt, x_sample
    outs_p, outs_s = [], []
    for l in range(depth):
        lw = _layer_weights(l, ffn1_norm, ffn1_w_in, ffn1_w_out, mix_norm, w_in_proj, conv_w, conv_b, dt_bias,
                            a_log, d_skip, ssm_norm, qk_norm, cmp_pe, cmp_w1, cmp_w2, w_branch_ssm,
                            w_branch_attn, w_out, ffn2_norm, ffn2_w_in, ffn2_w_out)
        xp, *rp = _layer_prompt(xp, lw, f_bias, 256)
        outs_p.append(rp)
        xs, *rs = _layer_sample(xs, state_ssm[l], state_conv[l], cache_kv[l], page_table, cache_win[l], lw, f_bias)
        outs_s.append(rs)
    stack = lambda outs, k: jnp.stack([o[k] for o in outs])
    return ((xp, xs) + tuple(stack(outs_p, k) for k in range(4)) + tuple(stack(outs_s, k) for k in range(4)))
```

```python
import functools
import math

import jax
import jax.numpy as jnp
import numpy as np
from jax import lax
from jax.experimental import pallas as pl
from jax.experimental.pallas import tpu as pltpu

F32 = jnp.float32
BF16 = jnp.bfloat16
HIGHEST = lax.Precision.HIGHEST

D_MODEL = 1024
D_FF = 2816
D_INNER = 2048
M_HEADDIM = 64
M_HEADS = 32
M_GROUPS = 4
D_STATE = 128
D_CONV = 4
CONV_DIM = D_INNER + 2 * M_GROUPS * D_STATE
HEAD_DIM = 64
Q_HEADS = 16
KV_HEADS = 4
Q_PER_KV = 4
D_ATTN = Q_HEADS * HEAD_DIM
KV_W = KV_HEADS * HEAD_DIM
CMP_BLOCK = 32
CMP_HIDDEN = 128
SEL_BLOCK = 64
SEL_TOPK = 16
WINDOW = 512
NUM_BUCKETS = 32
MAX_DISTANCE = 2048
PAGE_SIZE = 128
EPS = 1e-6
NEG = -1e30

LANES = 128
SSD_CL = 128
TQ = 128
TKC = 256
VMEM_LIMIT = 56 * 1024 * 1024
SMALL_W = 128
NSA_G_OFF = M_HEADS


def _cparams(sem):
    return pltpu.CompilerParams(dimension_semantics=sem, vmem_limit_bytes=VMEM_LIMIT)


def _const_spec(shape):
    nd = len(shape)
    return pl.BlockSpec(shape, lambda *a: (0,) * nd, pipeline_mode=pl.Buffered(1))


def _nt(a, b, precision=None):
    return lax.dot_general(a, b, (((1,), (1,)), ((), ())), preferred_element_type=F32, precision=precision)


def _mm(a, b, precision=None):
    return jnp.dot(a, b, preferred_element_type=F32, precision=precision)


def _rms_rows(x, g):
    return x * lax.rsqrt(jnp.mean(x * x, axis=-1, keepdims=True) + EPS) * g


def _silu(x):
    return x * jax.nn.sigmoid(x)


def _head_rms(a, gain2):
    lane = lax.broadcasted_iota(jnp.int32, (1, LANES), 1)
    lo_mask = lane < HEAD_DIM
    outs = []
    for c in range(a.shape[1] // LANES):
        blk = a[:, c * LANES:(c + 1) * LANES]
        sq = blk * blk
        lo = jnp.sum(jnp.where(lo_mask, sq, 0.0), axis=-1, keepdims=True)
        hi = jnp.sum(jnp.where(lo_mask, 0.0, sq), axis=-1, keepdims=True)
        r = jnp.where(lo_mask, lax.rsqrt(lo / HEAD_DIM + EPS), lax.rsqrt(hi / HEAD_DIM + EPS))
        outs.append(blk * r * gain2)
    return outs[0] if len(outs) == 1 else jnp.concatenate(outs, axis=1)


def _ffn_body(x_ref, g_ref, wg_ref, wu_ref, wo_ref, o_ref):
    x = x_ref[...]
    hb = _rms_rows(x, g_ref[...]).astype(BF16)
    gate = _mm(hb, wg_ref[...])
    up = _mm(hb, wu_ref[...])
    act = (_silu(gate) * up).astype(BF16)
    o_ref[...] = x + 0.5 * _mm(act, wo_ref[...])


def _ffn(x2d, norm_g, w_in, w_out, tm):
    n = x2d.shape[0]
    wg = w_in[:, :D_FF].astype(BF16)
    wu = w_in[:, D_FF:].astype(BF16)
    wo = w_out.astype(BF16)
    return pl.pallas_call(
        _ffn_body,
        out_shape=jax.ShapeDtypeStruct((n, D_MODEL), F32),
        grid=(n // tm,),
        in_specs=[pl.BlockSpec((tm, D_MODEL), lambda i: (i, 0)),
                  _const_spec((1, D_MODEL)),
                  _const_spec((D_MODEL, D_FF)), _const_spec((D_MODEL, D_FF)), _const_spec((D_FF, D_MODEL))],
        out_specs=pl.BlockSpec((tm, D_MODEL), lambda i: (i, 0)),
        compiler_params=_cparams(("arbitrary",)),
        name="ffn",
    )(x2d, norm_g.reshape(1, D_MODEL), wg, wu, wo)


def _inproj_body(x_ref, g_ref, w_ref, qkn_ref, z_ref, xbc_ref, q_ref, pg_ref, win_ref, mg_ref, sm_ref):
    hb = _rms_rows(x_ref[...], g_ref[...]).astype(BF16)
    o = 0
    z_ref[...] = _mm(hb, w_ref[:, o:o + D_INNER]); o += D_INNER
    xbc_ref[...] = _mm(hb, w_ref[:, o:o + CONV_DIM]); o += CONV_DIM
    q = _mm(hb, w_ref[:, o:o + D_ATTN]); o += D_ATTN
    q_ref[...] = _head_rms(q, qkn_ref[0:1, :] * (HEAD_DIM ** -0.5))
    kv = _mm(hb, w_ref[:, o:o + 6 * KV_W]); o += 6 * KV_W
    pg_ref[:, 0:2 * KV_W] = kv[:, 0:2 * KV_W]
    pg_ref[:, 2 * KV_W:3 * KV_W] = _head_rms(kv[:, 2 * KV_W:3 * KV_W], qkn_ref[2:3, :])
    pg_ref[:, 3 * KV_W:4 * KV_W] = kv[:, 3 * KV_W:4 * KV_W]
    win_ref[:, 0:KV_W] = _head_rms(kv[:, 4 * KV_W:5 * KV_W], qkn_ref[3:4, :])
    win_ref[:, KV_W:2 * KV_W] = kv[:, 5 * KV_W:6 * KV_W]
    mg_ref[...] = _mm(hb, w_ref[:, o:o + 2 * D_MODEL]); o += 2 * D_MODEL
    sm_ref[...] = _mm(hb, w_ref[:, o:o + SMALL_W])


def _pack_inproj_weight(w):
    c = np.cumsum([0, D_INNER, CONV_DIM, M_HEADS, D_ATTN, 6 * KV_W, 3 * Q_HEADS, 2 * D_MODEL])
    z, xbc, dt, q, kv, ng, mg = [w[:, c[i]:c[i + 1]] for i in range(7)]
    pad = jnp.zeros((D_MODEL, SMALL_W - M_HEADS - 3 * Q_HEADS), w.dtype)
    return jnp.concatenate([z, xbc, q, kv, mg, dt, ng, pad], axis=1).astype(BF16)


def _inproj(x2d, norm_g, w_packed, qk_norm, tm):
    n = x2d.shape[0]
    wtot = w_packed.shape[1]
    qkn2 = jnp.concatenate([qk_norm, qk_norm], axis=1)
    widths = (D_INNER, CONV_DIM, D_ATTN, 4 * KV_W, 2 * KV_W, 2 * D_MODEL, SMALL_W)
    return pl.pallas_call(
        _inproj_body,
        out_shape=[jax.ShapeDtypeStruct((n, w), F32) for w in widths],
        grid=(n // tm,),
        in_specs=[pl.BlockSpec((tm, D_MODEL), lambda i: (i, 0)),
                  _const_spec((1, D_MODEL)), _const_spec((D_MODEL, wtot)), _const_spec((4, LANES))],
        out_specs=[pl.BlockSpec((tm, w), lambda i: (i, 0)) for w in widths],
        compiler_params=_cparams(("arbitrary",)),
        name="inproj",
    )(x2d, norm_g.reshape(1, D_MODEL), w_packed, qkn2)


def _split_bf16(v):
    hi = v.astype(BF16)
    lo = (v - hi.astype(F32)).astype(BF16)
    return hi, lo


def _ssd_body(lv, xbc_ref, z_ref, sm_ref, conv0_ref, ssm0_ref, cw_ref, cb_ref, dtb_ref, alog_ref, dsk_ref,
              nw_ref, y_ref, convo_ref, ssmo_ref, xp_ref, st_ref):
    cl = SSD_CL
    c = pl.program_id(1)
    nchunk = pl.num_programs(1)
    hp = M_HEADS // M_GROUPS * M_HEADDIM

    @pl.when(c == 0)
    def _():
        xp_ref[...] = jnp.zeros(xp_ref.shape, F32)
        xp_ref[8 - (D_CONV - 1):8, :] = conv0_ref[0]
        st_ref[...] = ssm0_ref[0].reshape(M_HEADS * M_HEADDIM, D_STATE).T

    xp_ref[8:8 + lv, :] = xbc_ref[0]
    conv = cb_ref[...]
    for k in range(D_CONV):
        conv = conv + xp_ref[pl.ds(8 - (D_CONV - 1) + k, cl), :] * cw_ref[k:k + 1, :]
    tail = xp_ref[8 + lv - (D_CONV - 1):8 + lv, :]
    convo_ref[0] = tail
    xp_ref[8 - (D_CONV - 1):8, :] = tail
    xc = _silu(conv)
    xs = xc[:, :D_INNER]

    sm = sm_ref[0]
    z = z_ref[0]
    if lv < cl:
        sm = jnp.concatenate([sm, jnp.zeros((cl - lv, SMALL_W), F32)], axis=0)
        z = jnp.concatenate([z, jnp.zeros((cl - lv, D_INNER), F32)], axis=0)
    xdt_in = sm + dtb_ref[...]
    dt = jnp.maximum(xdt_in, 0.0) + jnp.log1p(jnp.exp(-jnp.abs(xdt_in)))
    row = lax.broadcasted_iota(jnp.int32, (cl, 1), 0)
    dt = jnp.where(row < lv, dt, 0.0)
    a_neg = -jnp.exp(alog_ref[...])
    da = dt * a_neg
    ri = lax.broadcasted_iota(jnp.int32, (cl, cl), 0)
    ci = lax.broadcasted_iota(jnp.int32, (cl, cl), 1)
    tril = ri >= ci
    acs = _mm(tril.astype(F32), da, precision=HIGHEST)
    alast = acs[cl - 1:cl, :]
    eye = (lax.broadcasted_iota(jnp.int32, (LANES, LANES), 0)
           == lax.broadcasted_iota(jnp.int32, (LANES, LANES), 1)).astype(F32)
    acs_t = _nt(eye, acs, precision=HIGHEST)

    ej = lax.broadcasted_iota(jnp.int32, (LANES, D_INNER), 0)
    el = lax.broadcasted_iota(jnp.int32, (LANES, D_INNER), 1)
    expand = (ej == el // M_HEADDIM).astype(BF16)
    stack = jnp.concatenate([dt, jnp.exp(acs), dt * jnp.exp(alast - acs)], axis=0)
    s_hi, s_lo = _split_bf16(stack)
    ex = _mm(s_hi, expand) + _mm(s_lo, expand)
    dt_e = ex[0:cl]
    es_e = ex[cl:2 * cl]
    w_e = ex[2 * cl:3 * cl]
    xdt = (xs * dt_e).astype(BF16)
    xw = (xs * w_e).astype(BF16)
    chunk_decay = es_e[cl - 1:cl, :]

    lane = lax.broadcasted_iota(jnp.int32, (1, LANES), 1)
    lo_mask = lane < M_HEADDIM
    eye_b = eye.astype(BF16)
    y_parts = []
    for g in range(M_GROUPS):
        bg = xc[:, D_INNER + g * D_STATE:D_INNER + (g + 1) * D_STATE].astype(BF16)
        cg = xc[:, D_INNER + (M_GROUPS + g) * D_STATE:D_INNER + (M_GROUPS + g + 1) * D_STATE].astype(BF16)
        cb = _nt(cg, bg)
        st_g = st_ref[:, g * hp:(g + 1) * hp]
        y_off = _mm(cg, st_g.astype(BF16)) * es_e[:, g * hp:(g + 1) * hp]
        bg_t = _nt(eye_b, bg).astype(BF16)
        st_ref[:, g * hp:(g + 1) * hp] = (st_g * chunk_decay[:, g * hp:(g + 1) * hp]
                                          + _mm(bg_t, xw[:, g * hp:(g + 1) * hp]))
        for jj in range(hp // LANES):
            col0 = g * hp + jj * LANES
            ms = []
            for j in (2 * (col0 // LANES), 2 * (col0 // LANES) + 1):
                diff = acs[:, j:j + 1] - acs_t[j:j + 1, :]
                dec = jnp.exp(jnp.where(tril, diff, -jnp.inf))
                ms.append((cb * dec).astype(BF16))
            x2 = xdt[:, col0:col0 + LANES]
            zero = jnp.zeros_like(x2)
            rhs = jnp.concatenate([jnp.where(lo_mask, x2, zero), jnp.where(lo_mask, zero, x2)], axis=0)
            y_parts.append(_mm(jnp.concatenate(ms, axis=1), rhs) + y_off[:, jj * LANES:(jj + 1) * LANES])
    y = jnp.concatenate(y_parts, axis=1)
    y = (y + dsk_ref[...] * xs) * _silu(z)
    outs = []
    for g in range(M_GROUPS):
        yg = y[:, g * hp:(g + 1) * hp]
        outs.append(yg * lax.rsqrt(jnp.mean(yg * yg, axis=-1, keepdims=True) + EPS))
    y = jnp.concatenate(outs, axis=1) * nw_ref[...]
    y_ref[0] = y[:lv].astype(BF16)

    @pl.when(c == nchunk - 1)
    def _():
        ssmo_ref[0] = st_ref[...].T.reshape(M_HEADS, M_HEADDIM, D_STATE)


def _ssd(xbc, z, small, conv0, ssm0, conv_w, conv_b, dt_bias, a_log, d_skip, ssm_norm):
    b, L = xbc.shape[:2]
    lv = SSD_CL if L % SSD_CL == 0 else L
    assert lv == SSD_CL or (L < SSD_CL and L % 8 == 0)
    nchunk = L // lv
    pad = lambda v: jnp.pad(v.astype(F32), (0, SMALL_W - M_HEADS)).reshape(1, SMALL_W)
    rep = lambda v: jnp.repeat(v.astype(F32), M_HEADDIM).reshape(1, D_INNER)
    return pl.pallas_call(
        functools.partial(_ssd_body, lv),
        out_shape=[jax.ShapeDtypeStruct((b, L, D_INNER), BF16),
                   jax.ShapeDtypeStruct((b, D_CONV - 1, CONV_DIM), F32),
                   jax.ShapeDtypeStruct((b, M_HEADS, M_HEADDIM, D_STATE), F32)],
        grid=(b, nchunk),
        in_specs=[pl.BlockSpec((1, lv, CONV_DIM), lambda i, c: (i, c, 0)),
                  pl.BlockSpec((1, lv, D_INNER), lambda i, c: (i, c, 0)),
                  pl.BlockSpec((1, lv, SMALL_W), lambda i, c: (i, c, 0)),
                  pl.BlockSpec((1, D_CONV - 1, CONV_DIM), lambda i, c: (i, 0, 0)),
                  pl.BlockSpec((1, M_HEADS, M_HEADDIM, D_STATE), lambda i, c: (i, 0, 0, 0)),
                  _const_spec((D_CONV, CONV_DIM)), _const_spec((1, CONV_DIM)),
                  _const_spec((1, SMALL_W)), _const_spec((1, SMALL_W)),
                  _const_spec((1, D_INNER)), _const_spec((1, D_INNER))],
        out_specs=[pl.BlockSpec((1, lv, D_INNER), lambda i, c: (i, c, 0)),
                   pl.BlockSpec((1, D_CONV - 1, CONV_DIM), lambda i, c: (i, 0, 0)),
                   pl.BlockSpec((1, M_HEADS, M_HEADDIM, D_STATE), lambda i, c: (i, 0, 0, 0))],
        scratch_shapes=[pltpu.VMEM((8 + SSD_CL, CONV_DIM), F32), pltpu.VMEM((D_STATE, D_INNER), F32)],
        compiler_params=_cparams(("arbitrary", "arbitrary")),
        name="ssd",
    )(xbc, z, small, conv0, ssm0, conv_w, conv_b.reshape(1, CONV_DIM), pad(dt_bias), pad(a_log),
      rep(d_skip), ssm_norm.reshape(1, D_INNER))


NO_LIMIT = 2 ** 30


def _bias_table_body(base, rc, cc, lo, hi, ncol_valid, tab_ref, o_ref):
    _, rb, cb = o_ref.shape
    r = pl.program_id(0) * rb + lax.broadcasted_iota(jnp.int32, (rb, cb), 0)
    c = pl.program_id(1) * cb + lax.broadcasted_iota(jnp.int32, (rb, cb), 1)
    d = base + rc * r + cc * c
    valid = (d >= lo) & (d <= hi) & (c < ncol_valid)
    max_exact = NUM_BUCKETS // 2
    dd = jnp.maximum(d, 0)
    df = jnp.maximum(dd, 1).astype(F32)
    large = max_exact + (jnp.log(df / max_exact) / math.log(MAX_DISTANCE / max_exact)
                         * (NUM_BUCKETS - max_exact)).astype(jnp.int32)
    large = jnp.minimum(large, NUM_BUCKETS - 1)
    bucket = jnp.where(dd < max_exact, dd, large)
    for hq in range(Q_HEADS):
        acc = jnp.zeros((rb, cb), F32)
        for bkt in range(NUM_BUCKETS):
            acc = jnp.where(bucket == bkt, tab_ref[bkt, hq], acc)
        o_ref[hq] = jnp.where(valid, acc, NEG)


def _bias_table(rel_bias, rows, cols, base, rc, cc, lo=0, hi=NO_LIMIT, ncol_valid=NO_LIMIT):
    rb = LANES if rows > LANES and rows % LANES == 0 else rows
    cb = 2048 if cols > 2048 and cols % 2048 == 0 else cols
    return pl.pallas_call(
        functools.partial(_bias_table_body, base, rc, cc, lo, hi, ncol_valid),
        out_shape=jax.ShapeDtypeStruct((Q_HEADS, rows, cols), F32),
        grid=(rows // rb, cols // cb),
        in_specs=[pl.BlockSpec(memory_space=pltpu.SMEM)],
        out_specs=pl.BlockSpec((Q_HEADS, rb, cb), lambda i, j: (0, i, j)),
        compiler_params=_cparams(("arbitrary", "arbitrary")),
        name="bias_table",
    )(rel_bias.astype(F32))


def _prompt_bias_tables(rel_bias, L):
    n_off = L // TQ + 1
    n_off_w = WINDOW // TQ + 3
    tab_s = _bias_table(rel_bias, n_off * TQ, TQ, -TQ, 1, -1)
    tab_w = _bias_table(rel_bias, n_off_w * TQ, TQ, -TQ, 1, -1, hi=WINDOW)
    n_c = L // CMP_BLOCK
    tab_ct = _bias_table(rel_bias, n_c, L, -(CMP_BLOCK - 1), -CMP_BLOCK, 1)
    return (tab_s.reshape(KV_HEADS, Q_PER_KV, n_off, TQ, TQ), tab_w.reshape(KV_HEADS, Q_PER_KV, n_off_w, TQ, TQ),
            tab_ct.reshape(KV_HEADS, Q_PER_KV, n_c, L))


def _compress_weights(cmp_pe, cmp_w1, cmp_w2):
    eye = jnp.eye(KV_HEADS, dtype=F32)
    w1 = jnp.einsum("hg,cjdf->cjhdgf", eye, cmp_w1).reshape(2, CMP_BLOCK, KV_W, KV_HEADS * CMP_HIDDEN)
    w2 = jnp.einsum("hg,cfd->chfgd", eye, cmp_w2).reshape(2, KV_HEADS * CMP_HIDDEN, KV_W)
    pe = jnp.tile(cmp_pe, (1, 1, KV_HEADS))
    return pe.astype(F32), w1.astype(BF16), w2.astype(BF16)


def _compress_rows(lo_ref, hi_ref, nblk, pe_ref, w1_ref, w2_ref, c):
    acc = jnp.zeros((nblk, KV_HEADS * CMP_HIDDEN), F32)
    for j in range(CMP_BLOCK):
        take = pl.ds(j, nblk, stride=CMP_BLOCK)
        xj = jnp.concatenate([lo_ref[take, :], hi_ref[take, :]], axis=1) + pe_ref[c, j:j + 1, :]
        acc = acc + _mm(xj.astype(BF16), w1_ref[c, j])
    return _mm(_silu(acc).astype(BF16), w2_ref[c])


def _compress_prompt_body(k0_ref, k1_ref, v0_ref, v1_ref, pe_ref, w1_ref, w2_ref, kn_ref, kc_ref, vc_ref):
    nblk = kc_ref.shape[1]
    kc_ref[0] = _head_rms(_compress_rows(k0_ref.at[0], k1_ref.at[0], nblk, pe_ref, w1_ref, w2_ref, 0), kn_ref[...])
    vc_ref[0] = _compress_rows(v0_ref.at[0], v1_ref.at[0], nblk, pe_ref, w1_ref, w2_ref, 1)


def _compress_prompt(paged, pe, w1, w2, kn2):
    b, L = paged.shape[:2]
    n_c = L // CMP_BLOCK
    col = lambda c: pl.BlockSpec((1, L, LANES), lambda i: (i, 0, c))
    return pl.pallas_call(
        _compress_prompt_body,
        out_shape=[jax.ShapeDtypeStruct((b, n_c, KV_W), F32)] * 2,
        grid=(b,),
        in_specs=[col(0), col(1), col(2), col(3),
                  _const_spec(pe.shape), _const_spec(w1.shape), _const_spec(w2.shape), _const_spec((1, LANES))],
        out_specs=[pl.BlockSpec((1, n_c, KV_W), lambda i: (i, 0, 0))] * 2,
        compiler_params=_cparams(("arbitrary",)),
        name="compress_prompt",
    )(paged, paged, paged, paged, pe, w1, w2, kn2)


def _online_softmax_step(s, carry, v):
    m, l, acc = carry
    m_new = jnp.maximum(m, jnp.max(s, axis=-1, keepdims=True))
    alpha = jnp.exp(m - m_new)
    p = jnp.exp(s - m_new)
    return (m_new, alpha * l + jnp.sum(p, axis=-1, keepdims=True), alpha * acc + _mm(p.astype(BF16), v))


def _select_blocks(imp, t, n_blk):
    big = 1e30
    j = lax.broadcasted_iota(jnp.int32, (1, imp.shape[1]), 1)
    cur = t // SEL_BLOCK
    forced = (j == 0) | (j == cur) | (j == cur - 1)
    causal = j * SEL_BLOCK <= t
    score = jnp.where(forced, big, jnp.where(causal, imp, -big))
    rank = jnp.zeros(imp.shape, F32)
    for i in range(n_blk):
        ci = score[:, i:i + 1]
        ahead = (ci > score) | ((ci == score) & (i < j))
        rank = rank + jnp.where(ahead, 1.0, 0.0)
    return jnp.where(rank < SEL_TOPK, 1.0, 0.0)


def _nsa_prompt_body(n_sel, q_ref, ks_ref, vs_ref, kw_ref, vw_ref, kc_ref, vc_ref, sm_ref, tabs_ref, tabw_ref,
                     tabc_ref, o_ref, kaug_scr, vaug_scr, kwin_scr, vwin_scr, s_scr, p_scr):
    h = pl.program_id(1)
    qi = pl.program_id(2)
    rows = Q_PER_KV * TQ
    lane = lax.broadcasted_iota(jnp.int32, (1, LANES), 1)
    mine = (lane // HEAD_DIM) == (h % 2)

    @pl.when(qi == 0)
    def _():
        seq = kaug_scr.shape[0]
        key = lax.broadcasted_iota(jnp.int32, (seq, LANES), 0)
        blk = lax.broadcasted_iota(jnp.int32, (seq, LANES), 1)
        kaug_scr[:, 0:LANES] = ks_ref[0].astype(BF16)
        kaug_scr[:, LANES:2 * LANES] = jnp.where(blk == key // SEL_BLOCK, NEG, 0.0).astype(BF16)
        vaug_scr[...] = jnp.where(mine, vs_ref[0], 1.0).astype(BF16)
        kwin_scr[...] = kw_ref[0].astype(BF16)
        vwin_scr[...] = jnp.where(mine, vw_ref[0], 1.0).astype(BF16)

    qb = q_ref[0]
    q4 = jnp.concatenate([qb[:, g * HEAD_DIM:(g + 1) * HEAD_DIM] for g in range(Q_PER_KV)], axis=0)
    qpad = jnp.where(mine, jnp.concatenate([q4, q4], axis=1), 0.0).astype(BF16)

    n_c = kc_ref.shape[1]
    bias_ct = jnp.concatenate([tabc_ref[0, g] for g in range(Q_PER_KV)], axis=1)
    s_ct = _nt(kc_ref[0].astype(BF16), qpad) + bias_ct
    e_ct = jnp.where(bias_ct > 0.5 * NEG, jnp.exp(s_ct - jnp.max(s_ct, axis=0, keepdims=True)), 0.0)
    p_ct = e_ct / jnp.maximum(jnp.sum(e_ct, axis=0, keepdims=True), 1e-30)
    o_c = lax.dot_general(p_ct.astype(BF16), vc_ref[0].astype(BF16), (((0,), (0,)), ((), ())),
                          preferred_element_type=F32)

    imp_c = sum(p_ct[:, g * TQ:(g + 1) * TQ] for g in range(Q_PER_KV))
    nr = -(-n_sel // 8) * 8
    pair = (lax.broadcasted_iota(jnp.int32, (nr, n_c), 0)
            == lax.broadcasted_iota(jnp.int32, (nr, n_c), 1) // (SEL_BLOCK // CMP_BLOCK)).astype(BF16)
    c_hi = imp_c.astype(BF16)
    c_mid = (imp_c - c_hi.astype(F32)).astype(BF16)
    c_lo = (imp_c - c_hi.astype(F32) - c_mid.astype(F32)).astype(BF16)
    imp = _mm(pair, c_hi) + _mm(pair, c_mid) + _mm(pair, c_lo)
    t = qi * TQ + lax.broadcasted_iota(jnp.int32, (1, TQ), 1)
    j = lax.broadcasted_iota(jnp.int32, (nr, 1), 0)
    cur = t // SEL_BLOCK
    forced = (j == 0) | (j == cur) | (j == cur - 1)
    score = jnp.where(forced, 1e30, jnp.where(j * SEL_BLOCK <= t, imp, -1e30))
    rank = jnp.zeros((nr, TQ), F32)
    for i in range(n_sel):
        ri = score[i:i + 1, :]
        rank = rank + jnp.where((ri > score) | ((ri == score) & (i < j)), 1.0, 0.0)
    drop_t = jnp.where(rank < SEL_TOPK, 0.0, 1.0)
    drop_t = jnp.concatenate([drop_t, jnp.ones((LANES - nr, TQ), F32)], axis=0)
    drop = drop_t.T.astype(BF16)
    qaug = jnp.concatenate([qpad, jnp.concatenate([drop] * Q_PER_KV, axis=0)], axis=1)

    sub = 64
    even = (h % 2) == 0
    pick = lambda a: jnp.where(even, a[:, :HEAD_DIM], a[:, HEAD_DIM:])
    other = lambda a: jnp.where(even, a[:, HEAD_DIM:], a[:, :HEAD_DIM])

    def attend(q_lhs, k_slab, v_slab, tab_ref, kt0, n_kt):
        width = n_kt * TQ
        s_scr[:, 0:width] = _nt(q_lhs, k_slab)
        for rb in range(rows // sub):
            g, part = divmod(rb, TQ // sub)
            rr = slice(rb * sub, (rb + 1) * sub)
            tr = slice(part * sub, (part + 1) * sub)
            run = None
            for kt in range(n_kt):
                cc = slice(kt * TQ, (kt + 1) * TQ)
                s = s_scr[rr, cc] + tab_ref[0, g, jnp.maximum(qi - (kt0 + kt) + 1, 0), tr, :]
                s_scr[rr, cc] = s
                run = s if run is None else jnp.maximum(run, s)
            mb = jnp.broadcast_to(jnp.max(run, axis=-1, keepdims=True), (sub, TQ))
            for kt in range(n_kt):
                cc = slice(kt * TQ, (kt + 1) * TQ)
                p_scr[rr, cc] = jnp.exp(s_scr[rr, cc] - mb).astype(BF16)
        acc = _mm(p_scr[:, 0:width], v_slab)
        return pick(acc) / jnp.maximum(other(acc), 1e-30)

    n_tiles = kaug_scr.shape[0] // TQ
    ext_step = 4
    extents = [min((e + 1) * ext_step, n_tiles) for e in range(-(-n_tiles // ext_step))]
    o_s = lax.switch(qi // ext_step,
                     [functools.partial(lambda n: attend(qaug, kaug_scr[0:n * TQ, :], vaug_scr[0:n * TQ, :],
                                                         tabs_ref, 0, n), n) for n in extents])
    n_win = min(WINDOW // TQ + 1, n_tiles)
    w0 = jnp.maximum(qi - (n_win - 1), 0)
    w_rows = pl.ds(pl.multiple_of(w0 * TQ, TQ), n_win * TQ)
    o_w = attend(qpad, kwin_scr[w_rows, :], vwin_scr[w_rows, :], tabw_ref, w0, n_win)
    o_c = pick(o_c)

    gates = jax.nn.sigmoid(pltpu.roll(sm_ref[0], SMALL_W - NSA_G_OFF - h * (3 * Q_PER_KV), axis=1))
    outs = []
    for g in range(Q_PER_KV):
        rs = slice(g * TQ, (g + 1) * TQ)
        outs.append(gates[:, 3 * g:3 * g + 1] * o_c[rs] + gates[:, 3 * g + 1:3 * g + 2] * o_s[rs]
                    + gates[:, 3 * g + 2:3 * g + 3] * o_w[rs])
    o_ref[0] = jnp.concatenate(outs, axis=1).astype(BF16)


def _nsa_prompt(qn, paged, win, kc, vc, small, tab_s, tab_w, tab_c):
    b, L = qn.shape[:2]
    assert L % TKC == 0
    n_c = L // CMP_BLOCK
    rows = Q_PER_KV * TQ
    seq = lambda col: pl.BlockSpec((1, L, LANES), lambda i, h, q: (i, 0, col(h)))
    cmp = pl.BlockSpec((1, n_c, LANES), lambda i, h, q: (i, 0, h // 2))
    return pl.pallas_call(
        functools.partial(_nsa_prompt_body, L // SEL_BLOCK),
        out_shape=jax.ShapeDtypeStruct((b, L, D_ATTN), BF16),
        grid=(b, KV_HEADS, L // TQ),
        in_specs=[pl.BlockSpec((1, TQ, KV_W), lambda i, h, q: (i, q, h)),
                  seq(lambda h: 4 + h // 2), seq(lambda h: 6 + h // 2),
                  seq(lambda h: h // 2), seq(lambda h: 2 + h // 2),
                  cmp, cmp,
                  pl.BlockSpec((1, TQ, SMALL_W), lambda i, h, q: (i, q, 0)),
                  pl.BlockSpec((1,) + tab_s.shape[1:], lambda i, h, q: (h, 0, 0, 0, 0)),
                  pl.BlockSpec((1,) + tab_w.shape[1:], lambda i, h, q: (h, 0, 0, 0, 0)),
                  pl.BlockSpec((1, Q_PER_KV, n_c, TQ), lambda i, h, q: (h, 0, 0, q))],
        out_specs=pl.BlockSpec((1, TQ, KV_W), lambda i, h, q: (i, q, h)),
        scratch_shapes=[pltpu.VMEM((L, 2 * LANES), BF16), pltpu.VMEM((L, LANES), BF16),
                        pltpu.VMEM((L, LANES), BF16), pltpu.VMEM((L, LANES), BF16),
                        pltpu.VMEM((rows, L), F32), pltpu.VMEM((rows, L), BF16)],
        compiler_params=_cparams(("arbitrary", "arbitrary", "arbitrary")),
        name="nsa_prompt",
    )(qn, paged, paged, win, win, kc, vc, small, tab_s, tab_w, tab_c)


def _merge_body(x_ref, y_ref, o_ref, mg_ref, wa_ref, wb_ref, wo_ref, out_ref):
    ya = _mm(y_ref[...], wa_ref[...])
    yb = _mm(o_ref[...], wb_ref[...])
    gm = jax.nn.sigmoid(mg_ref[...])
    mix = (gm[:, :D_MODEL] * ya + gm[:, D_MODEL:] * yb).astype(BF16)
    out_ref[...] = x_ref[...] + _mm(mix, wo_ref[...])


def _merge(x2d, y2d, o2d, mg, wa, wb, wo, tm):
    n = x2d.shape[0]
    row = lambda w: pl.BlockSpec((tm, w), lambda i: (i, 0))
    return pl.pallas_call(
        _merge_body,
        out_shape=jax.ShapeDtypeStruct((n, D_MODEL), F32),
        grid=(n // tm,),
        in_specs=[row(D_MODEL), row(D_INNER), row(D_ATTN), row(2 * D_MODEL),
                  _const_spec((D_INNER, D_MODEL)), _const_spec((D_ATTN, D_MODEL)), _const_spec((D_MODEL, D_MODEL))],
        out_specs=row(D_MODEL),
        compiler_params=_cparams(("arbitrary",)),
        name="merge",
    )(x2d, y2d, o2d, mg, wa.astype(BF16), wb.astype(BF16), wo.astype(BF16))


def _layer_prompt(x, lw, f_bias, tm):
    (ffn1_norm, ffn1_w_in, ffn1_w_out, mix_norm, w_packed, conv_w, conv_b, dt_bias, a_log, d_skip, ssm_norm,
     qk_norm, cmp_prep, w_branch_ssm, w_branch_attn, w_out, ffn2_norm, ffn2_w_in, ffn2_w_out) = lw
    b, L = x.shape[:2]
    n = b * L
    x1 = _ffn(x.reshape(n, D_MODEL), ffn1_norm, ffn1_w_in, ffn1_w_out, tm)
    z, xbc, qn, paged, win, mg, small = _inproj(x1, mix_norm, w_packed, qk_norm, tm)
    r3 = lambda a: a.reshape(b, L, a.shape[-1])
    conv0 = jnp.zeros((b, D_CONV - 1, CONV_DIM), F32)
    ssm0 = jnp.zeros((b, M_HEADS, M_HEADDIM, D_STATE), F32)
    y, new_conv, new_ssm = _ssd(r3(xbc), r3(z), r3(small), conv0, ssm0, conv_w, conv_b, dt_bias, a_log, d_skip,
                                ssm_norm)
    pe, w1, w2 = cmp_prep
    kn2 = jnp.concatenate([qk_norm[1], qk_norm[1]]).reshape(1, LANES)
    kc, vc = _compress_prompt(r3(paged), pe, w1, w2, kn2)
    tab_s, tab_w, tab_c = _prompt_bias_tables(f_bias, L)
    o = _nsa_prompt(r3(qn), r3(paged), r3(win), kc, vc, r3(small), tab_s, tab_w, tab_c)
    x2 = _merge(x1, y.reshape(n, D_INNER), o.reshape(n, D_ATTN), mg, w_branch_ssm, w_branch_attn, w_out, tm)
    x3 = _ffn(x2, ffn2_norm, ffn2_w_in, ffn2_w_out, tm)
    n_keep = min(WINDOW, L)
    new_kv = paged.reshape(b, L, 4, KV_HEADS, HEAD_DIM)
    new_win = r3(win)[:, L - n_keep:].reshape(b, n_keep, 2, KV_HEADS, HEAD_DIM)
    return x3.reshape(b, L, D_MODEL), new_kv, new_win, new_ssm, new_conv


PAGES_PER_STEP = 32


def _sample_qpad(qb):
    ls = qb.shape[0]
    blocks = []
    for hq in range(Q_HEADS):
        h = hq // Q_PER_KV
        parts = []
        if h > 0:
            parts.append(jnp.zeros((ls, h * HEAD_DIM), F32))
        parts.append(qb[:, hq * HEAD_DIM:(hq + 1) * HEAD_DIM])
        if h < KV_HEADS - 1:
            parts.append(jnp.zeros((ls, (KV_HEADS - 1 - h) * HEAD_DIM), F32))
        blocks.append(jnp.concatenate(parts, axis=1))
    return jnp.concatenate(blocks, axis=0).astype(BF16)


def _sample_head_piece(o, hq, ls):
    h = hq // Q_PER_KV
    return o[hq * ls:(hq + 1) * ls, h * HEAD_DIM:(h + 1) * HEAD_DIM]


def _chunk_copies(pt_ref, cache_ref, buf, sem, cols, width, bb, ii, sl, p):
    page = pt_ref[bb, ii * PAGES_PER_STEP + p]
    row0 = pl.multiple_of(p * PAGE_SIZE, PAGE_SIZE)
    return [pltpu.make_async_copy(cache_ref.at[page, :, pl.ds(c0, width)],
                                  buf.at[sl, k, pl.ds(row0, PAGE_SIZE), :], sem.at[sl])
            for k, c0 in enumerate(cols)]


def _stream_chunks(pt_ref, cache_ref, buf, sem, cols, width):
    b = pl.program_id(0)
    i = pl.program_id(1)
    nch = pl.num_programs(1)
    step = b * nch + i
    slot = step % 2

    def run(bb, ii, sl, start):
        def body(p, carry):
            for cp in _chunk_copies(pt_ref, cache_ref, buf, sem, cols, width, bb, ii, sl, p):
                cp.start() if start else cp.wait()
            return carry
        lax.fori_loop(0, PAGES_PER_STEP, body, 0)

    @pl.when(step == 0)
    def _():
        run(b, i, slot, True)

    nxt = step + 1

    @pl.when(nxt < pl.num_programs(0) * nch)
    def _():
        run(nxt // nch, nxt % nch, nxt % 2, True)

    run(b, i, slot, False)
    return slot


def _cmp_sample_body(past_len, pt_ref, cache_ref, q_ref, biasc_ref, pe_ref, w1_ref, w2_ref, kn_ref,
                     oc_ref, selm_ref, buf, sem, kc_s, vc_s):
    i = pl.program_id(1)
    slot = _stream_chunks(pt_ref, cache_ref, buf, sem, [k * LANES for k in range(4)], LANES)
    nblk = PAGES_PER_STEP * PAGE_SIZE // CMP_BLOCK
    rows0 = pl.multiple_of(i * nblk, nblk)
    kc_s[pl.ds(rows0, nblk), :] = _head_rms(
        _compress_rows(buf.at[slot, 0], buf.at[slot, 1], nblk, pe_ref, w1_ref, w2_ref, 0), kn_ref[...])
    vc_s[pl.ds(rows0, nblk), :] = _compress_rows(buf.at[slot, 2], buf.at[slot, 3], nblk, pe_ref, w1_ref, w2_ref, 1)

    @pl.when(i == pl.num_programs(1) - 1)
    def _():
        ls = q_ref.shape[1]
        n_c = kc_s.shape[0]
        s = _nt(_sample_qpad(q_ref[0]), kc_s[...].astype(BF16)) + biasc_ref[...]
        e = jnp.exp(s - jnp.max(s, axis=-1, keepdims=True))
        p = e / jnp.maximum(jnp.sum(e, axis=-1, keepdims=True), 1e-30)
        oc = _mm(p.astype(BF16), vc_s[...].astype(BF16))
        oc_ref[0] = jnp.concatenate([_sample_head_piece(oc, hq, ls) for hq in range(Q_HEADS)], axis=1)
        imp_c = jnp.sum(p.reshape(KV_HEADS, Q_PER_KV, ls, n_c), axis=1).reshape(KV_HEADS * ls, n_c)
        w = selm_ref.shape[2]
        pair = (lax.broadcasted_iota(jnp.int32, (n_c, w), 0) // (SEL_BLOCK // CMP_BLOCK)
                == lax.broadcasted_iota(jnp.int32, (n_c, w), 1)).astype(F32)
        imp = _mm(imp_c, pair, precision=HIGHEST)
        t = past_len + lax.broadcasted_iota(jnp.int32, (KV_HEADS * ls, 1), 0) % ls
        selm_ref[0] = _select_blocks(imp, t, past_len // SEL_BLOCK + 1)


def _nsa_sample_body(pt_ref, cache_ref, q_ref, selm_ref, oc_ref, sm_ref, pnew_ref, cwin_ref, wnew_ref,
                     biass_ref, biasw_ref, biasn_ref, o_ref, buf, sem, m_s, l_s, a_s):
    i = pl.program_id(1)
    slot = _stream_chunks(pt_ref, cache_ref, buf, sem, [2 * KV_W, 3 * KV_W], KV_W)
    ls = q_ref.shape[1]
    rows = Q_HEADS * ls
    tk = PAGES_PER_STEP * PAGE_SIZE
    qpad = _sample_qpad(q_ref[0])

    @pl.when(i == 0)
    def _():
        m_s[...] = jnp.full(m_s.shape, NEG, F32)
        l_s[...] = jnp.zeros(l_s.shape, F32)
        a_s[...] = jnp.zeros(a_s.shape, F32)

    sel = selm_ref[0]
    w = sel.shape[1]
    sel = jnp.broadcast_to(sel.reshape(KV_HEADS, 1, ls, w), (KV_HEADS, Q_PER_KV, ls, w)).reshape(rows, w)
    blk = lax.broadcasted_iota(jnp.int32, (w, tk), 0)
    key = lax.broadcasted_iota(jnp.int32, (w, tk), 1)
    spread = (blk == i * (tk // SEL_BLOCK) + key // SEL_BLOCK).astype(BF16)
    keep = _mm(sel.astype(BF16), spread)
    s = _nt(qpad, buf[slot, 0].astype(BF16)) + biass_ref[...] + jnp.where(keep > 0.5, 0.0, NEG)
    m, l, a = _online_softmax_step(s, (m_s[...], l_s[...], a_s[...]), buf[slot, 1].astype(BF16))
    m_s[...] = m
    l_s[...] = l
    a_s[...] = a

    @pl.when(i == pl.num_programs(1) - 1)
    def _():
        def pad_rows(x):
            return jnp.concatenate([x, jnp.zeros((LANES - ls, x.shape[1]), F32)], axis=0).astype(BF16)

        pn = pnew_ref[0]
        wn = wnew_ref[0]
        s_n = _nt(qpad, pad_rows(pn[:, 2 * KV_W:3 * KV_W])) + biasn_ref[...]
        _, l2, a2 = _online_softmax_step(s_n, (m, l, a), pad_rows(pn[:, 3 * KV_W:4 * KV_W]))
        o_s = a2 / jnp.maximum(l2, 1e-30)
        cw = cwin_ref[0]
        init = (jnp.full((rows, 1), NEG, F32), jnp.zeros((rows, 1), F32), jnp.zeros((rows, KV_W), F32))
        c1 = _online_softmax_step(_nt(qpad, cw[:, 0:KV_W].astype(BF16)) + biasw_ref[...], init,
                                  cw[:, KV_W:2 * KV_W].astype(BF16))
        s_wn = _nt(qpad, pad_rows(wn[:, 0:KV_W])) + biasn_ref[...]
        _, l3, a3 = _online_softmax_step(s_wn, c1, pad_rows(wn[:, KV_W:2 * KV_W]))
        o_w = a3 / jnp.maximum(l3, 1e-30)
        gates = jax.nn.sigmoid(sm_ref[0])
        oc = oc_ref[0]
        outs = []
        for hq in range(Q_HEADS):
            g0 = NSA_G_OFF + 3 * hq
            outs.append(gates[:, g0:g0 + 1] * oc[:, hq * HEAD_DIM:(hq + 1) * HEAD_DIM]
                        + gates[:, g0 + 1:g0 + 2] * _sample_head_piece(o_s, hq, ls)
                        + gates[:, g0 + 2:g0 + 3] * _sample_head_piece(o_w, hq, ls))
        o_ref[0] = jnp.concatenate(outs, axis=1).astype(BF16)


def _sample_bias_tables(rel_bias, past_len, ls, n_buf):
    rows2d = lambda t: t.reshape(Q_HEADS * ls, t.shape[2])
    bias_s = _bias_table(rel_bias, ls, past_len, past_len, 1, -1)
    n_c = past_len // CMP_BLOCK
    bias_c = _bias_table(rel_bias, ls, n_c, past_len - (CMP_BLOCK - 1), 1, -CMP_BLOCK)
    bias_w = _bias_table(rel_bias, ls, n_buf, n_buf, 1, -1, hi=WINDOW)
    bias_n = _bias_table(rel_bias, ls, LANES, 0, 1, -1, ncol_valid=ls)
    return rows2d(bias_s), rows2d(bias_c), rows2d(bias_w), rows2d(bias_n)


def _nsa_sample(qn, paged, win, small, cache_pages, page_table, cache_win, rel_bias, cmp_prep, kn2):
    b, ls = qn.shape[:2]
    n_pages = page_table.shape[1]
    past_len = n_pages * PAGE_SIZE
    n_buf = cache_win.shape[1]
    assert n_pages % PAGES_PER_STEP == 0 and ls <= SEL_BLOCK and ls % 8 == 0 and n_buf == min(WINDOW, past_len)
    nch = n_pages // PAGES_PER_STEP
    tk = PAGES_PER_STEP * PAGE_SIZE
    n_c = past_len // CMP_BLOCK
    n_sel = past_len // SEL_BLOCK + 1
    w_sel = -(-n_sel // LANES) * LANES
    rows = Q_HEADS * ls
    pe, w1, w2 = cmp_prep
    bias_s, bias_c, bias_w, bias_n = _sample_bias_tables(rel_bias, past_len, ls, n_buf)
    tok = lambda wd: pl.BlockSpec((1, ls, wd), lambda i, c, pt: (i, 0, 0))
    const = lambda shape: pl.BlockSpec(shape, lambda i, c, pt: (0,) * len(shape), pipeline_mode=pl.Buffered(1))
    o_c, selm = pl.pallas_call(
        functools.partial(_cmp_sample_body, past_len),
        out_shape=[jax.ShapeDtypeStruct((b, ls, D_ATTN), F32),
                   jax.ShapeDtypeStruct((b, KV_HEADS * ls, w_sel), F32)],
        grid_spec=pltpu.PrefetchScalarGridSpec(
            num_scalar_prefetch=1, grid=(b, nch),
            in_specs=[pl.BlockSpec(memory_space=pl.ANY), tok(D_ATTN), const((rows, n_c)),
                      const(pe.shape), const(w1.shape), const(w2.shape), const((1, LANES))],
            out_specs=[tok(D_ATTN), pl.BlockSpec((1, KV_HEADS * ls, w_sel), lambda i, c, pt: (i, 0, 0))],
            scratch_shapes=[pltpu.VMEM((2, 4, tk, LANES), F32), pltpu.SemaphoreType.DMA((2,)),
                            pltpu.VMEM((n_c, KV_W), F32), pltpu.VMEM((n_c, KV_W), F32)]),
        compiler_params=_cparams(("arbitrary", "arbitrary")),
        name="nsa_sample_compress",
    )(page_table, cache_pages, qn, bias_c, pe, w1, w2, kn2)
    return pl.pallas_call(
        _nsa_sample_body,
        out_shape=jax.ShapeDtypeStruct((b, ls, D_ATTN), BF16),
        grid_spec=pltpu.PrefetchScalarGridSpec(
            num_scalar_prefetch=1, grid=(b, nch),
            in_specs=[pl.BlockSpec(memory_space=pl.ANY), tok(D_ATTN),
                      pl.BlockSpec((1, KV_HEADS * ls, w_sel), lambda i, c, pt: (i, 0, 0)),
                      tok(D_ATTN), tok(SMALL_W), tok(4 * KV_W),
                      pl.BlockSpec((1, n_buf, 2 * KV_W), lambda i, c, pt: (i, 0, 0)), tok(2 * KV_W),
                      pl.BlockSpec((rows, tk), lambda i, c, pt: (0, c)),
                      const((rows, n_buf)), const((rows, LANES))],
            out_specs=tok(D_ATTN),
            scratch_shapes=[pltpu.VMEM((2, 2, tk, KV_W), F32), pltpu.SemaphoreType.DMA((2,)),
                            pltpu.VMEM((rows, 1), F32), pltpu.VMEM((rows, 1), F32), pltpu.VMEM((rows, KV_W), F32)]),
        compiler_params=_cparams(("arbitrary", "arbitrary")),
        name="nsa_sample",
    )(page_table, cache_pages, qn, selm, o_c, small, paged, cache_win, win, bias_s, bias_w, bias_n)


def _layer_sample(x, ssm0, conv0, cache_kv_l, page_table, cache_win_l, lw, f_bias):
    (ffn1_norm, ffn1_w_in, ffn1_w_out, mix_norm, w_packed, conv_w, conv_b, dt_bias, a_log, d_skip, ssm_norm,
     qk_norm, cmp_prep, w_branch_ssm, w_branch_attn, w_out, ffn2_norm, ffn2_w_in, ffn2_w_out) = lw
    b, ls = x.shape[:2]
    n = b * ls
    tm = n
    x1 = _ffn(x.reshape(n, D_MODEL), ffn1_norm, ffn1_w_in, ffn1_w_out, tm)
    z, xbc, qn, paged, win, mg, small = _inproj(x1, mix_norm, w_packed, qk_norm, tm)
    r3 = lambda a: a.reshape(b, ls, a.shape[-1])
    y, new_conv, new_ssm = _ssd(r3(xbc), r3(z), r3(small), conv0, ssm0, conv_w, conv_b, dt_bias, a_log, d_skip,
                                ssm_norm)
    kn2 = jnp.concatenate([qk_norm[1], qk_norm[1]]).reshape(1, LANES)
    n_pool = cache_kv_l.shape[0]
    cache_pages = cache_kv_l.reshape(n_pool, PAGE_SIZE, 4 * KV_W)
    n_buf = cache_win_l.shape[1]
    cwin = cache_win_l.reshape(b, n_buf, 2 * KV_W)
    o = _nsa_sample(r3(qn), r3(paged), r3(win), r3(small), cache_pages, page_table, cwin, f_bias, cmp_prep, kn2)
    x2 = _merge(x1, y.reshape(n, D_INNER), o.reshape(n, D_ATTN), mg, w_branch_ssm, w_branch_attn, w_out, tm)
    x3 = _ffn(x2, ffn2_norm, ffn2_w_in, ffn2_w_out, tm)
    n_keep = min(WINDOW, n_buf + ls)
    new_kv = paged.reshape(b, ls, 4, KV_HEADS, HEAD_DIM)
    win_all = jnp.concatenate([cwin, r3(win)], axis=1)
    new_win = win_all[:, n_buf + ls - n_keep:].reshape(b, n_keep, 2, KV_HEADS, HEAD_DIM)
    return x3.reshape(b, ls, D_MODEL), new_kv, new_win, new_ssm, new_conv


def _layer_weights(l, ffn1_norm, ffn1_w_in, ffn1_w_out, mix_norm, w_in_proj, conv_w, conv_b, dt_bias, a_log,
                   d_skip, ssm_norm, qk_norm, cmp_pe, cmp_w1, cmp_w2, w_branch_ssm, w_branch_attn, w_out,
                   ffn2_norm, ffn2_w_in, ffn2_w_out):
    return (ffn1_norm[l], ffn1_w_in[l], ffn1_w_out[l], mix_norm[l], _pack_inproj_weight(w_in_proj[l]), conv_w[l],
            conv_b[l], dt_bias[l], a_log[l], d_skip[l], ssm_norm[l], qk_norm[l],
            _compress_weights(cmp_pe[l], cmp_w1[l], cmp_w2[l]), w_branch_ssm[l], w_branch_attn[l], w_out[l],
            ffn2_norm[l], ffn2_w_in[l], ffn2_w_out[l])


def kernel(x_prompt, x_sample, cache_kv, cache_win, state_ssm, state_conv, page_table, rel_bias, ffn1_norm, ffn1_w_in, ffn1_w_out, mix_norm, w_in_proj, conv_w, conv_b, dt_bias, a_log, d_skip, ssm_norm, qk_norm, cmp_pe, cmp_w1, cmp_w2, w_branch_ssm, w_branch_attn, w_out, ffn2_norm, ffn2_w_in, ffn2_w_out):
    depth = ffn1_norm.shape[0]
    f_bias = rel_bias
    xp, xs = x_prompt, x_sample
    outs_p, outs_s = [], []
    for l in range(depth):
        lw = _layer_weights(l, ffn1_norm, ffn1_w_in, ffn1_w_out, mix_norm, w_in_proj, conv_w, conv_b, dt_bias,
                            a_log, d_skip, ssm_norm, qk_norm, cmp_pe, cmp_w1, cmp_w2, w_branch_ssm,
                            w_branch_attn, w_out, ffn2_norm, ffn2_w_in, ffn2_w_out)
        xp, *rp = _layer_prompt(xp, lw, f_bias, 256)
        outs_p.append(rp)
        xs, *rs = _layer_sample(xs, state_ssm[l], state_conv[l], cache_kv[l], page_table, cache_win[l], lw, f_bias)
        outs_s.append(rs)
    stack = lambda outs, k: jnp.stack([o[k] for o in outs])
    return ((xp, xs) + tuple(stack(outs_p, k) for k in range(4)) + tuple(stack(outs_s, k) for k in range(4)))
```

```python
import functools
import math

import jax
import jax.numpy as jnp
import numpy as np
from jax import lax
from jax.experimental import pallas as pl
from jax.experimental.pallas import tpu as pltpu

F32 = jnp.float32
BF16 = jnp.bfloat16
HIGHEST = lax.Precision.HIGHEST

D_MODEL = 1024
D_FF = 2816
D_INNER = 2048
M_HEADDIM = 64
M_HEADS = 32
M_GROUPS = 4
D_STATE = 128
D_CONV = 4
CONV_DIM = D_INNER + 2 * M_GROUPS * D_STATE
HEAD_DIM = 64
Q_HEADS = 16
KV_HEADS = 4
Q_PER_KV = 4
D_ATTN = Q_HEADS * HEAD_DIM
KV_W = KV_HEADS * HEAD_DIM
CMP_BLOCK = 32
CMP_HIDDEN = 128
SEL_BLOCK = 64
SEL_TOPK = 16
WINDOW = 512
NUM_BUCKETS = 32
MAX_DISTANCE = 2048
PAGE_SIZE = 128
EPS = 1e-6
NEG = -1e30

LANES = 128
SSD_CL = 128
TQ = 128
TKC = 256
VMEM_LIMIT = 56 * 1024 * 1024
SMALL_W = 128
NSA_G_OFF = M_HEADS


def _cparams(sem):
    return pltpu.CompilerParams(dimension_semantics=sem, vmem_limit_bytes=VMEM_LIMIT)


def _const_spec(shape):
    nd = len(shape)
    return pl.BlockSpec(shape, lambda *a: (0,) * nd, pipeline_mode=pl.Buffered(1))


def _nt(a, b, precision=None):
    return lax.dot_general(a, b, (((1,), (1,)), ((), ())), preferred_element_type=F32, precision=precision)


def _mm(a, b, precision=None):
    return jnp.dot(a, b, preferred_element_type=F32, precision=precision)


def _rms_rows(x, g):
    return x * lax.rsqrt(jnp.mean(x * x, axis=-1, keepdims=True) + EPS) * g


def _silu(x):
    return x * jax.nn.sigmoid(x)


def _head_rms(a, gain2):
    lane = lax.broadcasted_iota(jnp.int32, (1, LANES), 1)
    lo_mask = lane < HEAD_DIM
    outs = []
    for c in range(a.shape[1] // LANES):
        blk = a[:, c * LANES:(c + 1) * LANES]
        sq = blk * blk
        lo = jnp.sum(jnp.where(lo_mask, sq, 0.0), axis=-1, keepdims=True)
        hi = jnp.sum(jnp.where(lo_mask, 0.0, sq), axis=-1, keepdims=True)
        r = jnp.where(lo_mask, lax.rsqrt(lo / HEAD_DIM + EPS), lax.rsqrt(hi / HEAD_DIM + EPS))
        outs.append(blk * r * gain2)
    return outs[0] if len(outs) == 1 else jnp.concatenate(outs, axis=1)


def _ffn_body(x_ref, g_ref, wg_ref, wu_ref, wo_ref, o_ref):
    x = x_ref[...]
    hb = _rms_rows(x, g_ref[...]).astype(BF16)
    gate = _mm(hb, wg_ref[...])
    up = _mm(hb, wu_ref[...])
    act = (_silu(gate) * up).astype(BF16)
    o_ref[...] = x + 0.5 * _mm(act, wo_ref[...])


def _ffn(x2d, norm_g, w_in, w_out, tm):
    n = x2d.shape[0]
    wg = w_in[:, :D_FF].astype(BF16)
    wu = w_in[:, D_FF:].astype(BF16)
    wo = w_out.astype(BF16)
    return pl.pallas_call(
        _ffn_body,
        out_shape=jax.ShapeDtypeStruct((n, D_MODEL), F32),
        grid=(n // tm,),
        in_specs=[pl.BlockSpec((tm, D_MODEL), lambda i: (i, 0)),
                  _const_spec((1, D_MODEL)),
                  _const_spec((D_MODEL, D_FF)), _const_spec((D_MODEL, D_FF)), _const_spec((D_FF, D_MODEL))],
        out_specs=pl.BlockSpec((tm, D_MODEL), lambda i: (i, 0)),
        compiler_params=_cparams(("arbitrary",)),
        name="ffn",
    )(x2d, norm_g.reshape(1, D_MODEL), wg, wu, wo)


def _inproj_body(x_ref, g_ref, w_ref, qkn_ref, z_ref, xbc_ref, q_ref, pg_ref, win_ref, mg_ref, sm_ref):
    hb = _rms_rows(x_ref[...], g_ref[...]).astype(BF16)
    o = 0
    z_ref[...] = _mm(hb, w_ref[:, o:o + D_INNER]); o += D_INNER
    xbc_ref[...] = _mm(hb, w_ref[:, o:o + CONV_DIM]); o += CONV_DIM
    q = _mm(hb, w_ref[:, o:o + D_ATTN]); o += D_ATTN
    q_ref[...] = _head_rms(q, qkn_ref[0:1, :] * (HEAD_DIM ** -0.5))
    kv = _mm(hb, w_ref[:, o:o + 6 * KV_W]); o += 6 * KV_W
    pg_ref[:, 0:2 * KV_W] = kv[:, 0:2 * KV_W]
    pg_ref[:, 2 * KV_W:3 * KV_W] = _head_rms(kv[:, 2 * KV_W:3 * KV_W], qkn_ref[2:3, :])
    pg_ref[:, 3 * KV_W:4 * KV_W] = kv[:, 3 * KV_W:4 * KV_W]
    win_ref[:, 0:KV_W] = _head_rms(kv[:, 4 * KV_W:5 * KV_W], qkn_ref[3:4, :])
    win_ref[:, KV_W:2 * KV_W] = kv[:, 5 * KV_W:6 * KV_W]
    mg_ref[...] = _mm(hb, w_ref[:, o:o + 2 * D_MODEL]); o += 2 * D_MODEL
    sm_ref[...] = _mm(hb, w_ref[:, o:o + SMALL_W])


def _pack_inproj_weight(w):
    c = np.cumsum([0, D_INNER, CONV_DIM, M_HEADS, D_ATTN, 6 * KV_W, 3 * Q_HEADS, 2 * D_MODEL])
    z, xbc, dt, q, kv, ng, mg = [w[:, c[i]:c[i + 1]] for i in range(7)]
    pad = jnp.zeros((D_MODEL, SMALL_W - M_HEADS - 3 * Q_HEADS), w.dtype)
    return jnp.concatenate([z, xbc, q, kv, mg, dt, ng, pad], axis=1).astype(BF16)


def _inproj(x2d, norm_g, w_packed, qk_norm, tm):
    n = x2d.shape[0]
    wtot = w_packed.shape[1]
    qkn2 = jnp.concatenate([qk_norm, qk_norm], axis=1)
    widths = (D_INNER, CONV_DIM, D_ATTN, 4 * KV_W, 2 * KV_W, 2 * D_MODEL, SMALL_W)
    return pl.pallas_call(
        _inproj_body,
        out_shape=[jax.ShapeDtypeStruct((n, w), F32) for w in widths],
        grid=(n // tm,),
        in_specs=[pl.BlockSpec((tm, D_MODEL), lambda i: (i, 0)),
                  _const_spec((1, D_MODEL)), _const_spec((D_MODEL, wtot)), _const_spec((4, LANES))],
        out_specs=[pl.BlockSpec((tm, w), lambda i: (i, 0)) for w in widths],
        compiler_params=_cparams(("arbitrary",)),
        name="inproj",
    )(x2d, norm_g.reshape(1, D_MODEL), w_packed, qkn2)


def _split_bf16(v):
    hi = v.astype(BF16)
    lo = (v - hi.astype(F32)).astype(BF16)
    return hi, lo


def _ssd_body(lv, xbc_ref, z_ref, sm_ref, conv0_ref, ssm0_ref, cw_ref, cb_ref, dtb_ref, alog_ref, dsk_ref,
              nw_ref, y_ref, convo_ref, ssmo_ref, xp_ref, st_ref):
    cl = SSD_CL
    c = pl.program_id(1)
    nchunk = pl.num_programs(1)
    hp = M_HEADS // M_GROUPS * M_HEADDIM

    @pl.when(c == 0)
    def _():
        xp_ref[...] = jnp.zeros(xp_ref.shape, F32)
        xp_ref[8 - (D_CONV - 1):8, :] = conv0_ref[0]
        st_ref[...] = ssm0_ref[0].reshape(M_HEADS * M_HEADDIM, D_STATE).T

    xp_ref[8:8 + lv, :] = xbc_ref[0]
    conv = cb_ref[...]
    for k in range(D_CONV):
        conv = conv + xp_ref[pl.ds(8 - (D_CONV - 1) + k, cl), :] * cw_ref[k:k + 1, :]
    tail = xp_ref[8 + lv - (D_CONV - 1):8 + lv, :]
    convo_ref[0] = tail
    xp_ref[8 - (D_CONV - 1):8, :] = tail
    xc = _silu(conv)
    xs = xc[:, :D_INNER]

    sm = sm_ref[0]
    z = z_ref[0]
    if lv < cl:
        sm = jnp.concatenate([sm, jnp.zeros((cl - lv, SMALL_W), F32)], axis=0)
        z = jnp.concatenate([z, jnp.zeros((cl - lv, D_INNER), F32)], axis=0)
    xdt_in = sm + dtb_ref[...]
    dt = jnp.maximum(xdt_in, 0.0) + jnp.log1p(jnp.exp(-jnp.abs(xdt_in)))
    row = lax.broadcasted_iota(jnp.int32, (cl, 1), 0)
    dt = jnp.where(row < lv, dt, 0.0)
    a_neg = -jnp.exp(alog_ref[...])
    da = dt * a_neg
    ri = lax.broadcasted_iota(jnp.int32, (cl, cl), 0)
    ci = lax.broadcasted_iota(jnp.int32, (cl, cl), 1)
    tril = ri >= ci
    acs = _mm(tril.astype(F32), da, precision=HIGHEST)
    alast = acs[cl - 1:cl, :]
    eye = (lax.broadcasted_iota(jnp.int32, (LANES, LANES), 0)
           == lax.broadcasted_iota(jnp.int32, (LANES, LANES), 1)).astype(F32)
    acs_t = _nt(eye, acs, precision=HIGHEST)

    ej = lax.broadcasted_iota(jnp.int32, (LANES, D_INNER), 0)
    el = lax.broadcasted_iota(jnp.int32, (LANES, D_INNER), 1)
    expand = (ej == el // M_HEADDIM).astype(BF16)
    stack = jnp.concatenate([dt, jnp.exp(acs), dt * jnp.exp(alast - acs)], axis=0)
    s_hi, s_lo = _split_bf16(stack)
    ex = _mm(s_hi, expand) + _mm(s_lo, expand)
    dt_e = ex[0:cl]
    es_e = ex[cl:2 * cl]
    w_e = ex[2 * cl:3 * cl]
    xdt = (xs * dt_e).astype(BF16)
    xw = (xs * w_e).astype(BF16)
    chunk_decay = es_e[cl - 1:cl, :]

    lane = lax.broadcasted_iota(jnp.int32, (1, LANES), 1)
    lo_mask = lane < M_HEADDIM
    eye_b = eye.astype(BF16)
    y_parts = []
    for g in range(M_GROUPS):
        bg = xc[:, D_INNER + g * D_STATE:D_INNER + (g + 1) * D_STATE].astype(BF16)
        cg = xc[:, D_INNER + (M_GROUPS + g) * D_STATE:D_INNER + (M_GROUPS + g + 1) * D_STATE].astype(BF16)
        cb = _nt(cg, bg)
        st_g = st_ref[:, g * hp:(g + 1) * hp]
        y_off = _mm(cg, st_g.astype(BF16)) * es_e[:, g * hp:(g + 1) * hp]
        bg_t = _nt(eye_b, bg).astype(BF16)
        st_ref[:, g * hp:(g + 1) * hp] = (st_g * chunk_decay[:, g * hp:(g + 1) * hp]
                                          + _mm(bg_t, xw[:, g * hp:(g + 1) * hp]))
        for jj in range(hp // LANES):
            col0 = g * hp + jj * LANES
            ms = []
            for j in (2 * (col0 // LANES), 2 * (col0 // LANES) + 1):
                diff = acs[:, j:j + 1] - acs_t[j:j + 1, :]
                dec = jnp.exp(jnp.where(tril, diff, -jnp.inf))
                ms.append((cb * dec).astype(BF16))
            x2 = xdt[:, col0:col0 + LANES]
            zero = jnp.zeros_like(x2)
            rhs = jnp.concatenate([jnp.where(lo_mask, x2, zero), jnp.where(lo_mask, zero, x2)], axis=0)
            y_parts.append(_mm(jnp.concatenate(ms, axis=1), rhs) + y_off[:, jj * LANES:(jj + 1) * LANES])
    y = jnp.concatenate(y_parts, axis=1)
    y = (y + dsk_ref[...] * xs) * _silu(z)
    outs = []
    for g in range(M_GROUPS):
        yg = y[:, g * hp:(g + 1) * hp]
        outs.append(yg * lax.rsqrt(jnp.mean(yg * yg, axis=-1, keepdims=True) + EPS))
    y = jnp.concatenate(outs, axis=1) * nw_ref[...]
    y_ref[0] = y[:lv].astype(BF16)

    @pl.when(c == nchunk - 1)
    def _():
        ssmo_ref[0] = st_ref[...].T.reshape(M_HEADS, M_HEADDIM, D_STATE)


def _ssd(xbc, z, small, conv0, ssm0, conv_w, conv_b, dt_bias, a_log, d_skip, ssm_norm):
    b, L = xbc.shape[:2]
    lv = SSD_CL if L % SSD_CL == 0 else L
    assert lv == SSD_CL or (L < SSD_CL and L % 8 == 0)
    nchunk = L // lv
    pad = lambda v: jnp.pad(v.astype(F32), (0, SMALL_W - M_HEADS)).reshape(1, SMALL_W)
    rep = lambda v: jnp.repeat(v.astype(F32), M_HEADDIM).reshape(1, D_INNER)
    return pl.pallas_call(
        functools.partial(_ssd_body, lv),
        out_shape=[jax.ShapeDtypeStruct((b, L, D_INNER), BF16),
                   jax.ShapeDtypeStruct((b, D_CONV - 1, CONV_DIM), F32),
                   jax.ShapeDtypeStruct((b, M_HEADS, M_HEADDIM, D_STATE), F32)],
        grid=(b, nchunk),
        in_specs=[pl.BlockSpec((1, lv, CONV_DIM), lambda i, c: (i, c, 0)),
                  pl.BlockSpec((1, lv, D_INNER), lambda i, c: (i, c, 0)),
                  pl.BlockSpec((1, lv, SMALL_W), lambda i, c: (i, c, 0)),
                  pl.BlockSpec((1, D_CONV - 1, CONV_DIM), lambda i, c: (i, 0, 0)),
                  pl.BlockSpec((1, M_HEADS, M_HEADDIM, D_STATE), lambda i, c: (i, 0, 0, 0)),
                  _const_spec((D_CONV, CONV_DIM)), _const_spec((1, CONV_DIM)),
                  _const_spec((1, SMALL_W)), _const_spec((1, SMALL_W)),
                  _const_spec((1, D_INNER)), _const_spec((1, D_INNER))],
        out_specs=[pl.BlockSpec((1, lv, D_INNER), lambda i, c: (i, c, 0)),
                   pl.BlockSpec((1, D_CONV - 1, CONV_DIM), lambda i, c: (i, 0, 0)),
                   pl.BlockSpec((1, M_HEADS, M_HEADDIM, D_STATE), lambda i, c: (i, 0, 0, 0))],
        scratch_shapes=[pltpu.VMEM((8 + SSD_CL, CONV_DIM), F32), pltpu.VMEM((D_STATE, D_INNER), F32)],
        compiler_params=_cparams(("arbitrary", "arbitrary")),
        name="ssd",
    )(xbc, z, small, conv0, ssm0, conv_w, conv_b.reshape(1, CONV_DIM), pad(dt_bias), pad(a_log),
      rep(d_skip), ssm_norm.reshape(1, D_INNER))


NO_LIMIT = 2 ** 30


def _bias_table_body(base, rc, cc, lo, hi, ncol_valid, tab_ref, o_ref):
    _, rb, cb = o_ref.shape
    r = pl.program_id(0) * rb + lax.broadcasted_iota(jnp.int32, (rb, cb), 0)
    c = pl.program_id(1) * cb + lax.broadcasted_iota(jnp.int32, (rb, cb), 1)
    d = base + rc * r + cc * c
    valid = (d >= lo) & (d <= hi) & (c < ncol_valid)
    max_exact = NUM_BUCKETS // 2
    dd = jnp.maximum(d, 0)
    df = jnp.maximum(dd, 1).astype(F32)
    large = max_exact + (jnp.log(df / max_exact) / math.log(MAX_DISTANCE / max_exact)
                         * (NUM_BUCKETS - max_exact)).astype(jnp.int32)
    large = jnp.minimum(large, NUM_BUCKETS - 1)
    bucket = jnp.where(dd < max_exact, dd, large)
    for hq in range(Q_HEADS):
        acc = jnp.zeros((rb, cb), F32)
        for bkt in range(NUM_BUCKETS):
            acc = jnp.where(bucket == bkt, tab_ref[bkt, hq], acc)
        o_ref[hq] = jnp.where(valid, acc, NEG)


def _bias_table(rel_bias, rows, cols, base, rc, cc, lo=0, hi=NO_LIMIT, ncol_valid=NO_LIMIT):
    rb = LANES if rows > LANES and rows % LANES == 0 else rows
    cb = 2048 if cols > 2048 and cols % 2048 == 0 else cols
    return pl.pallas_call(
        functools.partial(_bias_table_body, base, rc, cc, lo, hi, ncol_valid),
        out_shape=jax.ShapeDtypeStruct((Q_HEADS, rows, cols), F32),
        grid=(rows // rb, cols // cb),
        in_specs=[pl.BlockSpec(memory_space=pltpu.SMEM)],
        out_specs=pl.BlockSpec((Q_HEADS, rb, cb), lambda i, j: (0, i, j)),
        compiler_params=_cparams(("arbitrary", "arbitrary")),
        name="bias_table",
    )(rel_bias.astype(F32))


def _prompt_bias_tables(rel_bias, L):
    n_off = L // TQ + 1
    n_off_w = WINDOW // TQ + 3
    tab_s = _bias_table(rel_bias, n_off * TQ, TQ, -TQ, 1, -1)
    tab_w = _bias_table(rel_bias, n_off_w * TQ, TQ, -TQ, 1, -1, hi=WINDOW)
    n_c = L // CMP_BLOCK
    tab_ct = _bias_table(rel_bias, n_c, L, -(CMP_BLOCK - 1), -CMP_BLOCK, 1)
    return (tab_s.reshape(KV_HEADS, Q_PER_KV, n_off, TQ, TQ), tab_w.reshape(KV_HEADS, Q_PER_KV, n_off_w, TQ, TQ),
            tab_ct.reshape(KV_HEADS, Q_PER_KV, n_c, L))


def _compress_weights(cmp_pe, cmp_w1, cmp_w2):
    eye = jnp.eye(KV_HEADS, dtype=F32)
    w1 = jnp.einsum("hg,cjdf->cjhdgf", eye, cmp_w1).reshape(2, CMP_BLOCK, KV_W, KV_HEADS * CMP_HIDDEN)
    w2 = jnp.einsum("hg,cfd->chfgd", eye, cmp_w2).reshape(2, KV_HEADS * CMP_HIDDEN, KV_W)
    pe = jnp.tile(cmp_pe, (1, 1, KV_HEADS))
    return pe.astype(F32), w1.astype(BF16), w2.astype(BF16)


def _compress_rows(lo_ref, hi_ref, nblk, pe_ref, w1_ref, w2_ref, c):
    acc = jnp.zeros((nblk, KV_HEADS * CMP_HIDDEN), F32)
    for j in range(CMP_BLOCK):
        take = pl.ds(j, nblk, stride=CMP_BLOCK)
        xj = jnp.concatenate([lo_ref[take, :], hi_ref[take, :]], axis=1) + pe_ref[c, j:j + 1, :]
        acc = acc + _mm(xj.astype(BF16), w1_ref[c, j])
    return _mm(_silu(acc).astype(BF16), w2_ref[c])


def _compress_prompt_body(k0_ref, k1_ref, v0_ref, v1_ref, pe_ref, w1_ref, w2_ref, kn_ref, kc_ref, vc_ref):
    nblk = kc_ref.shape[1]
    kc_ref[0] = _head_rms(_compress_rows(k0_ref.at[0], k1_ref.at[0], nblk, pe_ref, w1_ref, w2_ref, 0), kn_ref[...])
    vc_ref[0] = _compress_rows(v0_ref.at[0], v1_ref.at[0], nblk, pe_ref, w1_ref, w2_ref, 1)


def _compress_prompt(paged, pe, w1, w2, kn2):
    b, L = paged.shape[:2]
    n_c = L // CMP_BLOCK
    col = lambda c: pl.BlockSpec((1, L, LANES), lambda i: (i, 0, c))
    return pl.pallas_call(
        _compress_prompt_body,
        out_shape=[jax.ShapeDtypeStruct((b, n_c, KV_W), F32)] * 2,
        grid=(b,),
        in_specs=[col(0), col(1), col(2), col(3),
                  _const_spec(pe.shape), _const_spec(w1.shape), _const_spec(w2.shape), _const_spec((1, LANES))],
        out_specs=[pl.BlockSpec((1, n_c, KV_W), lambda i: (i, 0, 0))] * 2,
        compiler_params=_cparams(("arbitrary",)),
        name="compress_prompt",
    )(paged, paged, paged, paged, pe, w1, w2, kn2)


def _online_softmax_step(s, carry, v):
    m, l, acc = carry
    m_new = jnp.maximum(m, jnp.max(s, axis=-1, keepdims=True))
    alpha = jnp.exp(m - m_new)
    p = jnp.exp(s - m_new)
    return (m_new, alpha * l + jnp.sum(p, axis=-1, keepdims=True), alpha * acc + _mm(p.astype(BF16), v))


def _select_blocks(imp, t, n_blk):
    big = 1e30
    j = lax.broadcasted_iota(jnp.int32, (1, imp.shape[1]), 1)
    cur = t // SEL_BLOCK
    forced = (j == 0) | (j == cur) | (j == cur - 1)
    causal = j * SEL_BLOCK <= t
    score = jnp.where(forced, big, jnp.where(causal, imp, -big))
    rank = jnp.zeros(imp.shape, F32)
    for i in range(n_blk):
        ci = score[:, i:i + 1]
        ahead = (ci > score) | ((ci == score) & (i < j))
        rank = rank + jnp.where(ahead, 1.0, 0.0)
    return jnp.where(rank < SEL_TOPK, 1.0, 0.0)


def _nsa_prompt_body(n_sel, q_ref, ks_ref, vs_ref, kw_ref, vw_ref, kc_ref, vc_ref, sm_ref, tabs_ref, tabw_ref,
                     tabc_ref, o_ref, kaug_scr, vaug_scr, kwin_scr, vwin_scr, s_scr, p_scr):
    h = pl.program_id(1)
    qi = pl.program_id(2)
    rows = Q_PER_KV * TQ
    lane = lax.broadcasted_iota(jnp.int32, (1, LANES), 1)
    mine = (lane // HEAD_DIM) == (h % 2)

    @pl.when(qi == 0)
    def _():
        seq = kaug_scr.shape[0]
        key = lax.broadcasted_iota(jnp.int32, (seq, LANES), 0)
        blk = lax.broadcasted_iota(jnp.int32, (seq, LANES), 1)
        kaug_scr[:, 0:LANES] = ks_ref[0].astype(BF16)
        kaug_scr[:, LANES:2 * LANES] = jnp.where(blk == key // SEL_BLOCK, NEG, 0.0).astype(BF16)
        vaug_scr[...] = jnp.where(mine, vs_ref[0], 1.0).astype(BF16)
        kwin_scr[...] = kw_ref[0].astype(BF16)
        vwin_scr[...] = jnp.where(mine, vw_ref[0], 1.0).astype(BF16)

    qb = q_ref[0]
    q4 = jnp.concatenate([qb[:, g * HEAD_DIM:(g + 1) * HEAD_DIM] for g in range(Q_PER_KV)], axis=0)
    qpad = jnp.where(mine, jnp.concatenate([q4, q4], axis=1), 0.0).astype(BF16)

    n_c = kc_ref.shape[1]
    bias_ct = jnp.concatenate([tabc_ref[0, g] for g in range(Q_PER_KV)], axis=1)
    s_ct = _nt(kc_ref[0].astype(BF16), qpad) + bias_ct
    e_ct = jnp.where(bias_ct > 0.5 * NEG, jnp.exp(s_ct - jnp.max(s_ct, axis=0, keepdims=True)), 0.0)
    p_ct = e_ct / jnp.maximum(jnp.sum(e_ct, axis=0, keepdims=True), 1e-30)
    o_c = lax.dot_general(p_ct.astype(BF16), vc_ref[0].astype(BF16), (((0,), (0,)), ((), ())),
                          preferred_element_type=F32)

    imp_c = sum(p_ct[:, g * TQ:(g + 1) * TQ] for g in range(Q_PER_KV))
    nr = -(-n_sel // 8) * 8
    pair = (lax.broadcasted_iota(jnp.int32, (nr, n_c), 0)
            == lax.broadcasted_iota(jnp.int32, (nr, n_c), 1) // (SEL_BLOCK // CMP_BLOCK)).astype(BF16)
    c_hi = imp_c.astype(BF16)
    c_mid = (imp_c - c_hi.astype(F32)).astype(BF16)
    c_lo = (imp_c - c_hi.astype(F32) - c_mid.astype(F32)).astype(BF16)
    imp = _mm(pair, c_hi) + _mm(pair, c_mid) + _mm(pair, c_lo)
    t = qi * TQ + lax.broadcasted_iota(jnp.int32, (1, TQ), 1)
    j = lax.broadcasted_iota(jnp.int32, (nr, 1), 0)
    cur = t // SEL_BLOCK
    forced = (j == 0) | (j == cur) | (j == cur - 1)
    score = jnp.where(forced, 1e30, jnp.where(j * SEL_BLOCK <= t, imp, -1e30))
    rank = jnp.zeros((nr, TQ), F32)
    for i in range(n_sel):
        ri = score[i:i + 1, :]
        rank = rank + jnp.where((ri > score) | ((ri == score) & (i < j)), 1.0, 0.0)
    drop_t = jnp.where(rank < SEL_TOPK, 0.0, 1.0)
    drop_t = jnp.concatenate([drop_t, jnp.ones((LANES - nr, TQ), F32)], axis=0)
    drop = drop_t.T.astype(BF16)
    qaug = jnp.concatenate([qpad, jnp.concatenate([drop] * Q_PER_KV, axis=0)], axis=1)

    sub = 64
    even = (h % 2) == 0
    pick = lambda a: jnp.where(even, a[:, :HEAD_DIM], a[:, HEAD_DIM:])
    other = lambda a: jnp.where(even, a[:, HEAD_DIM:], a[:, :HEAD_DIM])

    def attend(q_lhs, k_slab, v_slab, tab_ref, kt0, n_kt):
        width = n_kt * TQ
        n_half = 2
        hrows = rows // n_half
        for hf in range(n_half):
            s_scr[hf * hrows:(hf + 1) * hrows, 0:width] = _nt(q_lhs[hf * hrows:(hf + 1) * hrows], k_slab)
        accs = []
        for hf in range(n_half):
            for rb in range(hf * hrows // sub, (hf + 1) * hrows // sub):
                g, part = divmod(rb, TQ // sub)
                rr = slice(rb * sub, (rb + 1) * sub)
                tr = slice(part * sub, (part + 1) * sub)
                run = None
                for kt in range(n_kt):
                    cc = slice(kt * TQ, (kt + 1) * TQ)
                    s = s_scr[rr, cc] + tab_ref[0, g, jnp.maximum(qi - (kt0 + kt) + 1, 0), tr, :]
                    s_scr[rr, cc] = s
                    run = s if run is None else jnp.maximum(run, s)
                mb = jnp.broadcast_to(jnp.max(run, axis=-1, keepdims=True), (sub, TQ))
                for kt in range(n_kt):
                    cc = slice(kt * TQ, (kt + 1) * TQ)
                    p_scr[rr, cc] = jnp.exp(s_scr[rr, cc] - mb).astype(BF16)
            accs.append(_mm(p_scr[hf * hrows:(hf + 1) * hrows, 0:width], v_slab))
        acc = jnp.concatenate(accs, axis=0)
        return pick(acc) / jnp.maximum(other(acc), 1e-30)

    n_tiles = kaug_scr.shape[0] // TQ
    ext_step = 2
    extents = [min((e + 1) * ext_step, n_tiles) for e in range(-(-n_tiles // ext_step))]
    o_s = lax.switch(qi // ext_step,
                     [functools.partial(lambda n: attend(qaug, kaug_scr[0:n * TQ, :], vaug_scr[0:n * TQ, :],
                                                         tabs_ref, 0, n), n) for n in extents])
    n_win = min(WINDOW // TQ + 1, n_tiles)
    w0 = jnp.maximum(qi - (n_win - 1), 0)
    w_rows = pl.ds(pl.multiple_of(w0 * TQ, TQ), n_win * TQ)
    o_w = attend(qpad, kwin_scr[w_rows, :], vwin_scr[w_rows, :], tabw_ref, w0, n_win)
    o_c = pick(o_c)

    gates = jax.nn.sigmoid(pltpu.roll(sm_ref[0], SMALL_W - NSA_G_OFF - h * (3 * Q_PER_KV), axis=1))
    outs = []
    for g in range(Q_PER_KV):
        rs = slice(g * TQ, (g + 1) * TQ)
        outs.append(gates[:, 3 * g:3 * g + 1] * o_c[rs] + gates[:, 3 * g + 1:3 * g + 2] * o_s[rs]
                    + gates[:, 3 * g + 2:3 * g + 3] * o_w[rs])
    o_ref[0] = jnp.concatenate(outs, axis=1).astype(BF16)


def _nsa_prompt(qn, paged, win, kc, vc, small, tab_s, tab_w, tab_c):
    b, L = qn.shape[:2]
    assert L % TKC == 0
    n_c = L // CMP_BLOCK
    rows = Q_PER_KV * TQ
    seq = lambda col: pl.BlockSpec((1, L, LANES), lambda i, h, q: (i, 0, col(h)))
    cmp = pl.BlockSpec((1, n_c, LANES), lambda i, h, q: (i, 0, h // 2))
    return pl.pallas_call(
        functools.partial(_nsa_prompt_body, L // SEL_BLOCK),
        out_shape=jax.ShapeDtypeStruct((b, L, D_ATTN), BF16),
        grid=(b, KV_HEADS, L // TQ),
        in_specs=[pl.BlockSpec((1, TQ, KV_W), lambda i, h, q: (i, q, h)),
                  seq(lambda h: 4 + h // 2), seq(lambda h: 6 + h // 2),
                  seq(lambda h: h // 2), seq(lambda h: 2 + h // 2),
                  cmp, cmp,
                  pl.BlockSpec((1, TQ, SMALL_W), lambda i, h, q: (i, q, 0)),
                  pl.BlockSpec((1,) + tab_s.shape[1:], lambda i, h, q: (h, 0, 0, 0, 0)),
                  pl.BlockSpec((1,) + tab_w.shape[1:], lambda i, h, q: (h, 0, 0, 0, 0)),
                  pl.BlockSpec((1, Q_PER_KV, n_c, TQ), lambda i, h, q: (h, 0, 0, q))],
        out_specs=pl.BlockSpec((1, TQ, KV_W), lambda i, h, q: (i, q, h)),
        scratch_shapes=[pltpu.VMEM((L, 2 * LANES), BF16), pltpu.VMEM((L, LANES), BF16),
                        pltpu.VMEM((L, LANES), BF16), pltpu.VMEM((L, LANES), BF16),
                        pltpu.VMEM((rows, L), F32), pltpu.VMEM((rows, L), BF16)],
        compiler_params=_cparams(("arbitrary", "arbitrary", "arbitrary")),
        name="nsa_prompt",
    )(qn, paged, paged, win, win, kc, vc, small, tab_s, tab_w, tab_c)


def _merge_body(x_ref, y_ref, o_ref, mg_ref, wa_ref, wb_ref, wo_ref, out_ref):
    ya = _mm(y_ref[...], wa_ref[...])
    yb = _mm(o_ref[...], wb_ref[...])
    gm = jax.nn.sigmoid(mg_ref[...])
    mix = (gm[:, :D_MODEL] * ya + gm[:, D_MODEL:] * yb).astype(BF16)
    out_ref[...] = x_ref[...] + _mm(mix, wo_ref[...])


def _merge(x2d, y2d, o2d, mg, wa, wb, wo, tm):
    n = x2d.shape[0]
    row = lambda w: pl.BlockSpec((tm, w), lambda i: (i, 0))
    return pl.pallas_call(
        _merge_body,
        out_shape=jax.ShapeDtypeStruct((n, D_MODEL), F32),
        grid=(n // tm,),
        in_specs=[row(D_MODEL), row(D_INNER), row(D_ATTN), row(2 * D_MODEL),
                  _const_spec((D_INNER, D_MODEL)), _const_spec((D_ATTN, D_MODEL)), _const_spec((D_MODEL, D_MODEL))],
        out_specs=row(D_MODEL),
        compiler_params=_cparams(("arbitrary",)),
        name="merge",
    )(x2d, y2d, o2d, mg, wa.astype(BF16), wb.astype(BF16), wo.astype(BF16))


def _layer_prompt(x, lw, f_bias, tm):
    (ffn1_norm, ffn1_w_in, ffn1_w_out, mix_norm, w_packed, conv_w, conv_b, dt_bias, a_log, d_skip, ssm_norm,
     qk_norm, cmp_prep, w_branch_ssm, w_branch_attn, w_out, ffn2_norm, ffn2_w_in, ffn2_w_out) = lw
    b, L = x.shape[:2]
    n = b * L
    tm_wide = 2 * tm if n % (2 * tm) == 0 else tm
    x1 = _ffn(x.reshape(n, D_MODEL), ffn1_norm, ffn1_w_in, ffn1_w_out, tm_wide)
    z, xbc, qn, paged, win, mg, small = _inproj(x1, mix_norm, w_packed, qk_norm, tm)
    r3 = lambda a: a.reshape(b, L, a.shape[-1])
    conv0 = jnp.zeros((b, D_CONV - 1, CONV_DIM), F32)
    ssm0 = jnp.zeros((b, M_HEADS, M_HEADDIM, D_STATE), F32)
    y, new_conv, new_ssm = _ssd(r3(xbc), r3(z), r3(small), conv0, ssm0, conv_w, conv_b, dt_bias, a_log, d_skip,
                                ssm_norm)
    pe, w1, w2 = cmp_prep
    kn2 = jnp.concatenate([qk_norm[1], qk_norm[1]]).reshape(1, LANES)
    kc, vc = _compress_prompt(r3(paged), pe, w1, w2, kn2)
    tab_s, tab_w, tab_c = _prompt_bias_tables(f_bias, L)
    o = _nsa_prompt(r3(qn), r3(paged), r3(win), kc, vc, r3(small), tab_s, tab_w, tab_c)
    x2 = _merge(x1, y.reshape(n, D_INNER), o.reshape(n, D_ATTN), mg, w_branch_ssm, w_branch_attn, w_out, tm_wide)
    x3 = _ffn(x2, ffn2_norm, ffn2_w_in, ffn2_w_out, tm_wide)
    n_keep = min(WINDOW, L)
    new_kv = paged.reshape(b, L, 4, KV_HEADS, HEAD_DIM)
    new_win = r3(win)[:, L - n_keep:].reshape(b, n_keep, 2, KV_HEADS, HEAD_DIM)
    return x3.reshape(b, L, D_MODEL), new_kv, new_win, new_ssm, new_conv


PAGES_PER_STEP = 32


def _sample_qpad(qb):
    ls = qb.shape[0]
    blocks = []
    for hq in range(Q_HEADS):
        h = hq // Q_PER_KV
        parts = []
        if h > 0:
            parts.append(jnp.zeros((ls, h * HEAD_DIM), F32))
        parts.append(qb[:, hq * HEAD_DIM:(hq + 1) * HEAD_DIM])
        if h < KV_HEADS - 1:
            parts.append(jnp.zeros((ls, (KV_HEADS - 1 - h) * HEAD_DIM), F32))
        blocks.append(jnp.concatenate(parts, axis=1))
    return jnp.concatenate(blocks, axis=0).astype(BF16)


def _sample_head_piece(o, hq, ls):
    h = hq // Q_PER_KV
    return o[hq * ls:(hq + 1) * ls, h * HEAD_DIM:(h + 1) * HEAD_DIM]


def _chunk_copies(pt_ref, cache_ref, buf, sem, cols, width, bb, ii, sl, p):
    page = pt_ref[bb, ii * PAGES_PER_STEP + p]
    row0 = pl.multiple_of(p * PAGE_SIZE, PAGE_SIZE)
    return [pltpu.make_async_copy(cache_ref.at[page, :, pl.ds(c0, width)],
                                  buf.at[sl, k, pl.ds(row0, PAGE_SIZE), :], sem.at[sl])
            for k, c0 in enumerate(cols)]


def _stream_chunks(pt_ref, cache_ref, buf, sem, cols, width):
    b = pl.program_id(0)
    i = pl.program_id(1)
    nch = pl.num_programs(1)
    step = b * nch + i
    slot = step % 2

    def run(bb, ii, sl, start):
        def body(p, carry):
            for cp in _chunk_copies(pt_ref, cache_ref, buf, sem, cols, width, bb, ii, sl, p):
                cp.start() if start else cp.wait()
            return carry
        lax.fori_loop(0, PAGES_PER_STEP, body, 0)

    @pl.when(step == 0)
    def _():
        run(b, i, slot, True)

    nxt = step + 1

    @pl.when(nxt < pl.num_programs(0) * nch)
    def _():
        run(nxt // nch, nxt % nch, nxt % 2, True)

    run(b, i, slot, False)
    return slot


def _cmp_sample_body(past_len, pt_ref, cache_ref, q_ref, biasc_ref, pe_ref, w1_ref, w2_ref, kn_ref,
                     oc_ref, selm_ref, buf, sem, kc_s, vc_s):
    i = pl.program_id(1)
    slot = _stream_chunks(pt_ref, cache_ref, buf, sem, [k * LANES for k in range(4)], LANES)
    nblk = PAGES_PER_STEP * PAGE_SIZE // CMP_BLOCK
    rows0 = pl.multiple_of(i * nblk, nblk)
    kc_s[pl.ds(rows0, nblk), :] = _head_rms(
        _compress_rows(buf.at[slot, 0], buf.at[slot, 1], nblk, pe_ref, w1_ref, w2_ref, 0), kn_ref[...])
    vc_s[pl.ds(rows0, nblk), :] = _compress_rows(buf.at[slot, 2], buf.at[slot, 3], nblk, pe_ref, w1_ref, w2_ref, 1)

    @pl.when(i == pl.num_programs(1) - 1)
    def _():
        ls = q_ref.shape[1]
        n_c = kc_s.shape[0]
        s = _nt(_sample_qpad(q_ref[0]), kc_s[...].astype(BF16)) + biasc_ref[...]
        e = jnp.exp(s - jnp.max(s, axis=-1, keepdims=True))
        p = e / jnp.maximum(jnp.sum(e, axis=-1, keepdims=True), 1e-30)
        oc = _mm(p.astype(BF16), vc_s[...].astype(BF16))
        oc_ref[0] = jnp.concatenate([_sample_head_piece(oc, hq, ls) for hq in range(Q_HEADS)], axis=1)
        imp_c = jnp.sum(p.reshape(KV_HEADS, Q_PER_KV, ls, n_c), axis=1).reshape(KV_HEADS * ls, n_c)
        w = selm_ref.shape[2]
        pair = (lax.broadcasted_iota(jnp.int32, (n_c, w), 0) // (SEL_BLOCK // CMP_BLOCK)
                == lax.broadcasted_iota(jnp.int32, (n_c, w), 1)).astype(F32)
        imp = _mm(imp_c, pair, precision=HIGHEST)
        t = past_len + lax.broadcasted_iota(jnp.int32, (KV_HEADS * ls, 1), 0) % ls
        selm_ref[0] = _select_blocks(imp, t, past_len // SEL_BLOCK + 1)


def _nsa_sample_body(pt_ref, cache_ref, q_ref, selm_ref, oc_ref, sm_ref, pnew_ref, cwin_ref, wnew_ref,
                     biass_ref, biasw_ref, biasn_ref, o_ref, buf, sem, m_s, l_s, a_s):
    i = pl.program_id(1)
    slot = _stream_chunks(pt_ref, cache_ref, buf, sem, [2 * KV_W, 3 * KV_W], KV_W)
    ls = q_ref.shape[1]
    rows = Q_HEADS * ls
    tk = PAGES_PER_STEP * PAGE_SIZE
    qpad = _sample_qpad(q_ref[0])

    @pl.when(i == 0)
    def _():
        m_s[...] = jnp.full(m_s.shape, NEG, F32)
        l_s[...] = jnp.zeros(l_s.shape, F32)
        a_s[...] = jnp.zeros(a_s.shape, F32)

    sel = selm_ref[0]
    w = sel.shape[1]
    sel = jnp.broadcast_to(sel.reshape(KV_HEADS, 1, ls, w), (KV_HEADS, Q_PER_KV, ls, w)).reshape(rows, w)
    blk = lax.broadcasted_iota(jnp.int32, (w, tk), 0)
    key = lax.broadcasted_iota(jnp.int32, (w, tk), 1)
    spread = (blk == i * (tk // SEL_BLOCK) + key // SEL_BLOCK).astype(BF16)
    keep = _mm(sel.astype(BF16), spread)
    s = _nt(qpad, buf[slot, 0].astype(BF16)) + biass_ref[...] + jnp.where(keep > 0.5, 0.0, NEG)
    m, l, a = _online_softmax_step(s, (m_s[...], l_s[...], a_s[...]), buf[slot, 1].astype(BF16))
    m_s[...] = m
    l_s[...] = l
    a_s[...] = a

    @pl.when(i == pl.num_programs(1) - 1)
    def _():
        def pad_rows(x):
            return jnp.concatenate([x, jnp.zeros((LANES - ls, x.shape[1]), F32)], axis=0).astype(BF16)

        pn = pnew_ref[0]
        wn = wnew_ref[0]
        s_n = _nt(qpad, pad_rows(pn[:, 2 * KV_W:3 * KV_W])) + biasn_ref[...]
        _, l2, a2 = _online_softmax_step(s_n, (m, l, a), pad_rows(pn[:, 3 * KV_W:4 * KV_W]))
        o_s = a2 / jnp.maximum(l2, 1e-30)
        cw = cwin_ref[0]
        init = (jnp.full((rows, 1), NEG, F32), jnp.zeros((rows, 1), F32), jnp.zeros((rows, KV_W), F32))
        c1 = _online_softmax_step(_nt(qpad, cw[:, 0:KV_W].astype(BF16)) + biasw_ref[...], init,
                                  cw[:, KV_W:2 * KV_W].astype(BF16))
        s_wn = _nt(qpad, pad_rows(wn[:, 0:KV_W])) + biasn_ref[...]
        _, l3, a3 = _online_softmax_step(s_wn, c1, pad_rows(wn[:, KV_W:2 * KV_W]))
        o_w = a3 / jnp.maximum(l3, 1e-30)
        gates = jax.nn.sigmoid(sm_ref[0])
        oc = oc_ref[0]
        outs = []
        for hq in range(Q_HEADS):
            g0 = NSA_G_OFF + 3 * hq
            outs.append(gates[:, g0:g0 + 1] * oc[:, hq * HEAD_DIM:(hq + 1) * HEAD_DIM]
                        + gates[:, g0 + 1:g0 + 2] * _sample_head_piece(o_s, hq, ls)
                        + gates[:, g0 + 2:g0 + 3] * _sample_head_piece(o_w, hq, ls))
        o_ref[0] = jnp.concatenate(outs, axis=1).astype(BF16)


def _sample_bias_tables(rel_bias, past_len, ls, n_buf):
    rows2d = lambda t: t.reshape(Q_HEADS * ls, t.shape[2])
    bias_s = _bias_table(rel_bias, ls, past_len, past_len, 1, -1)
    n_c = past_len // CMP_BLOCK
    bias_c = _bias_table(rel_bias, ls, n_c, past_len - (CMP_BLOCK - 1), 1, -CMP_BLOCK)
    bias_w = _bias_table(rel_bias, ls, n_buf, n_buf, 1, -1, hi=WINDOW)
    bias_n = _bias_table(rel_bias, ls, LANES, 0, 1, -1, ncol_valid=ls)
    return rows2d(bias_s), rows2d(bias_c), rows2d(bias_w), rows2d(bias_n)


def _nsa_sample(qn, paged, win, small, cache_pages, page_table, cache_win, rel_bias, cmp_prep, kn2):
    b, ls = qn.shape[:2]
    n_pages = page_table.shape[1]
    past_len = n_pages * PAGE_SIZE
    n_buf = cache_win.shape[1]
    assert n_pages % PAGES_PER_STEP == 0 and ls <= SEL_BLOCK and ls % 8 == 0 and n_buf == min(WINDOW, past_len)
    nch = n_pages // PAGES_PER_STEP
    tk = PAGES_PER_STEP * PAGE_SIZE
    n_c = past_len // CMP_BLOCK
    n_sel = past_len // SEL_BLOCK + 1
    w_sel = -(-n_sel // LANES) * LANES
    rows = Q_HEADS * ls
    pe, w1, w2 = cmp_prep
    bias_s, bias_c, bias_w, bias_n = _sample_bias_tables(rel_bias, past_len, ls, n_buf)
    tok = lambda wd: pl.BlockSpec((1, ls, wd), lambda i, c, pt: (i, 0, 0))
    const = lambda shape: pl.BlockSpec(shape, lambda i, c, pt: (0,) * len(shape), pipeline_mode=pl.Buffered(1))
    o_c, selm = pl.pallas_call(
        functools.partial(_cmp_sample_body, past_len),
        out_shape=[jax.ShapeDtypeStruct((b, ls, D_ATTN), F32),
                   jax.ShapeDtypeStruct((b, KV_HEADS * ls, w_sel), F32)],
        grid_spec=pltpu.PrefetchScalarGridSpec(
            num_scalar_prefetch=1, grid=(b, nch),
            in_specs=[pl.BlockSpec(memory_space=pl.ANY), tok(D_ATTN), const((rows, n_c)),
                      const(pe.shape), const(w1.shape), const(w2.shape), const((1, LANES))],
            out_specs=[tok(D_ATTN), pl.BlockSpec((1, KV_HEADS * ls, w_sel), lambda i, c, pt: (i, 0, 0))],
            scratch_shapes=[pltpu.VMEM((2, 4, tk, LANES), F32), pltpu.SemaphoreType.DMA((2,)),
                            pltpu.VMEM((n_c, KV_W), F32), pltpu.VMEM((n_c, KV_W), F32)]),
        compiler_params=_cparams(("arbitrary", "arbitrary")),
        name="nsa_sample_compress",
    )(page_table, cache_pages, qn, bias_c, pe, w1, w2, kn2)
    return pl.pallas_call(
        _nsa_sample_body,
        out_shape=jax.ShapeDtypeStruct((b, ls, D_ATTN), BF16),
        grid_spec=pltpu.PrefetchScalarGridSpec(
            num_scalar_prefetch=1, grid=(b, nch),
            in_specs=[pl.BlockSpec(memory_space=pl.ANY), tok(D_ATTN),
                      pl.BlockSpec((1, KV_HEADS * ls, w_sel), lambda i, c, pt: (i, 0, 0)),
                      tok(D_ATTN), tok(SMALL_W), tok(4 * KV_W),
                      pl.BlockSpec((1, n_buf, 2 * KV_W), lambda i, c, pt: (i, 0, 0)), tok(2 * KV_W),
                      pl.BlockSpec((rows, tk), lambda i, c, pt: (0, c)),
                      const((rows, n_buf)), const((rows, LANES))],
            out_specs=tok(D_ATTN),
            scratch_shapes=[pltpu.VMEM((2, 2, tk, KV_W), F32), pltpu.SemaphoreType.DMA((2,)),
                            pltpu.VMEM((rows, 1), F32), pltpu.VMEM((rows, 1), F32), pltpu.VMEM((rows, KV_W), F32)]),
        compiler_params=_cparams(("arbitrary", "arbitrary")),
        name="nsa_sample",
    )(page_table, cache_pages, qn, selm, o_c, small, paged, cache_win, win, bias_s, bias_w, bias_n)


def _layer_sample(x, ssm0, conv0, cache_kv_l, page_table, cache_win_l, lw, f_bias):
    (ffn1_norm, ffn1_w_in, ffn1_w_out, mix_norm, w_packed, conv_w, conv_b, dt_bias, a_log, d_skip, ssm_norm,
     qk_norm, cmp_prep, w_branch_ssm, w_branch_attn, w_out, ffn2_norm, ffn2_w_in, ffn2_w_out) = lw
    b, ls = x.shape[:2]
    n = b * ls
    tm = n
    x1 = _ffn(x.reshape(n, D_MODEL), ffn1_norm, ffn1_w_in, ffn1_w_out, tm)
    z, xbc, qn, paged, win, mg, small = _inproj(x1, mix_norm, w_packed, qk_norm, tm)
    r3 = lambda a: a.reshape(b, ls, a.shape[-1])
    y, new_conv, new_ssm = _ssd(r3(xbc), r3(z), r3(small), conv0, ssm0, conv_w, conv_b, dt_bias, a_log, d_skip,
                                ssm_norm)
    kn2 = jnp.concatenate([qk_norm[1], qk_norm[1]]).reshape(1, LANES)
    n_pool = cache_kv_l.shape[0]
    cache_pages = cache_kv_l.reshape(n_pool, PAGE_SIZE, 4 * KV_W)
    n_buf = cache_win_l.shape[1]
    cwin = cache_win_l.reshape(b, n_buf, 2 * KV_W)
    o = _nsa_sample(r3(qn), r3(paged), r3(win), r3(small), cache_pages, page_table, cwin, f_bias, cmp_prep, kn2)
    x2 = _merge(x1, y.reshape(n, D_INNER), o.reshape(n, D_ATTN), mg, w_branch_ssm, w_branch_attn, w_out, tm)
    x3 = _ffn(x2, ffn2_norm, ffn2_w_in, ffn2_w_out, tm)
    n_keep = min(WINDOW, n_buf + ls)
    new_kv = paged.reshape(b, ls, 4, KV_HEADS, HEAD_DIM)
    win_all = jnp.concatenate([cwin, r3(win)], axis=1)
    new_win = win_all[:, n_buf + ls - n_keep:].reshape(b, n_keep, 2, KV_HEADS, HEAD_DIM)
    return x3.reshape(b, ls, D_MODEL), new_kv, new_win, new_ssm, new_conv


def _layer_weights(l, ffn1_norm, ffn1_w_in, ffn1_w_out, mix_norm, w_in_proj, conv_w, conv_b, dt_bias, a_log,
                   d_skip, ssm_norm, qk_norm, cmp_pe, cmp_w1, cmp_w2, w_branch_ssm, w_branch_attn, w_out,
                   ffn2_norm, ffn2_w_in, ffn2_w_out):
    return (ffn1_norm[l], ffn1_w_in[l], ffn1_w_out[l], mix_norm[l], _pack_inproj_weight(w_in_proj[l]), conv_w[l],
            conv_b[l], dt_bias[l], a_log[l], d_skip[l], ssm_norm[l], qk_norm[l],
            _compress_weights(cmp_pe[l], cmp_w1[l], cmp_w2[l]), w_branch_ssm[l], w_branch_attn[l], w_out[l],
            ffn2_norm[l], ffn2_w_in[l], ffn2_w_out[l])


def kernel(x_prompt, x_sample, cache_kv, cache_win, state_ssm, state_conv, page_table, rel_bias, ffn1_norm, ffn1_w_in, ffn1_w_out, mix_norm, w_in_proj, conv_w, conv_b, dt_bias, a_log, d_skip, ssm_norm, qk_norm, cmp_pe, cmp_w1, cmp_w2, w_branch_ssm, w_branch_attn, w_out, ffn2_norm, ffn2_w_in, ffn2_w_out):
    depth = ffn1_norm.shape[0]
    f_bias = rel_bias
    xp, xs = x_prompt, x_sample
    outs_p, outs_s = [], []
    for l in range(depth):
        lw = _layer_weights(l, ffn1_norm, ffn1_w_in, ffn1_w_out, mix_norm, w_in_proj, conv_w, conv_b, dt_bias,
                            a_log, d_skip, ssm_norm, qk_norm, cmp_pe, cmp_w1, cmp_w2, w_branch_ssm,
                            w_branch_attn, w_out, ffn2_norm, ffn2_w_in, ffn2_w_out)
        xp, *rp = _layer_prompt(xp, lw, f_bias, 256)
        outs_p.append(rp)
        xs, *rs = _layer_sample(xs, state_ssm[l], state_conv[l], cache_kv[l], page_table, cache_win[l], lw, f_bias)
        outs_s.append(rs)
    stack = lambda outs, k: jnp.stack([o[k] for o in outs])
    return ((xp, xs) + tuple(stack(outs_p, k) for k in range(4)) + tuple(stack(outs_s, k) for k in range(4)))
```

```python
import functools
import math

import jax
import jax.numpy as jnp
import numpy as np
from jax import lax
from jax.experimental import pallas as pl
from jax.experimental.pallas import tpu as pltpu

F32 = jnp.float32
BF16 = jnp.bfloat16
HIGHEST = lax.Precision.HIGHEST

D_MODEL = 1024
D_FF = 2816
D_INNER = 2048
M_HEADDIM = 64
M_HEADS = 32
M_GROUPS = 4
D_STATE = 128
D_CONV = 4
CONV_DIM = D_INNER + 2 * M_GROUPS * D_STATE
HEAD_DIM = 64
Q_HEADS = 16
KV_HEADS = 4
Q_PER_KV = 4
D_ATTN = Q_HEADS * HEAD_DIM
KV_W = KV_HEADS * HEAD_DIM
CMP_BLOCK = 32
CMP_HIDDEN = 128
SEL_BLOCK = 64
SEL_TOPK = 16
WINDOW = 512
NUM_BUCKETS = 32
MAX_DISTANCE = 2048
PAGE_SIZE = 128
EPS = 1e-6
NEG = -1e30

LANES = 128
SSD_CL = 128
TQ = 128
TKC = 256
VMEM_LIMIT = 56 * 1024 * 1024
SMALL_W = 128
NSA_G_OFF = M_HEADS


def _cparams(sem):
    return pltpu.CompilerParams(dimension_semantics=sem, vmem_limit_bytes=VMEM_LIMIT)


def _const_spec(shape):
    nd = len(shape)
    return pl.BlockSpec(shape, lambda *a: (0,) * nd, pipeline_mode=pl.Buffered(1))


def _nt(a, b, precision=None):
    return lax.dot_general(a, b, (((1,), (1,)), ((), ())), preferred_element_type=F32, precision=precision)


def _mm(a, b, precision=None):
    return jnp.dot(a, b, preferred_element_type=F32, precision=precision)


def _rms_rows(x, g):
    return x * lax.rsqrt(jnp.mean(x * x, axis=-1, keepdims=True) + EPS) * g


def _silu(x):
    return x * jax.nn.sigmoid(x)


def _head_rms(a, gain2):
    lane = lax.broadcasted_iota(jnp.int32, (1, LANES), 1)
    lo_mask = lane < HEAD_DIM
    outs = []
    for c in range(a.shape[1] // LANES):
        blk = a[:, c * LANES:(c + 1) * LANES]
        sq = blk * blk
        lo = jnp.sum(jnp.where(lo_mask, sq, 0.0), axis=-1, keepdims=True)
        hi = jnp.sum(jnp.where(lo_mask, 0.0, sq), axis=-1, keepdims=True)
        r = jnp.where(lo_mask, lax.rsqrt(lo / HEAD_DIM + EPS), lax.rsqrt(hi / HEAD_DIM + EPS))
        outs.append(blk * r * gain2)
    return outs[0] if len(outs) == 1 else jnp.concatenate(outs, axis=1)


def _ffn_body(x_ref, g_ref, wg_ref, wu_ref, wo_ref, o_ref):
    x = x_ref[...]
    hb = _rms_rows(x, g_ref[...]).astype(BF16)
    gate = _mm(hb, wg_ref[...])
    up = _mm(hb, wu_ref[...])
    act = (_silu(gate) * up).astype(BF16)
    o_ref[...] = x + 0.5 * _mm(act, wo_ref[...])


def _ffn(x2d, norm_g, w_in, w_out, tm):
    n = x2d.shape[0]
    wg = w_in[:, :D_FF].astype(BF16)
    wu = w_in[:, D_FF:].astype(BF16)
    wo = w_out.astype(BF16)
    return pl.pallas_call(
        _ffn_body,
        out_shape=jax.ShapeDtypeStruct((n, D_MODEL), F32),
        grid=(n // tm,),
        in_specs=[pl.BlockSpec((tm, D_MODEL), lambda i: (i, 0)),
                  _const_spec((1, D_MODEL)),
                  _const_spec((D_MODEL, D_FF)), _const_spec((D_MODEL, D_FF)), _const_spec((D_FF, D_MODEL))],
        out_specs=pl.BlockSpec((tm, D_MODEL), lambda i: (i, 0)),
        compiler_params=_cparams(("arbitrary",)),
        name="ffn",
    )(x2d, norm_g.reshape(1, D_MODEL), wg, wu, wo)


def _inproj_body(x_ref, g_ref, w_ref, qkn_ref, z_ref, xbc_ref, q_ref, pg_ref, win_ref, mg_ref, sm_ref):
    hb = _rms_rows(x_ref[...], g_ref[...]).astype(BF16)
    o = 0
    z_ref[...] = _mm(hb, w_ref[:, o:o + D_INNER]); o += D_INNER
    xbc_ref[...] = _mm(hb, w_ref[:, o:o + CONV_DIM]); o += CONV_DIM
    q = _mm(hb, w_ref[:, o:o + D_ATTN]); o += D_ATTN
    q_ref[...] = _head_rms(q, qkn_ref[0:1, :] * (HEAD_DIM ** -0.5))
    kv = _mm(hb, w_ref[:, o:o + 6 * KV_W]); o += 6 * KV_W
    pg_ref[:, 0:2 * KV_W] = kv[:, 0:2 * KV_W]
    pg_ref[:, 2 * KV_W:3 * KV_W] = _head_rms(kv[:, 2 * KV_W:3 * KV_W], qkn_ref[2:3, :])
    pg_ref[:, 3 * KV_W:4 * KV_W] = kv[:, 3 * KV_W:4 * KV_W]
    win_ref[:, 0:KV_W] = _head_rms(kv[:, 4 * KV_W:5 * KV_W], qkn_ref[3:4, :])
    win_ref[:, KV_W:2 * KV_W] = kv[:, 5 * KV_W:6 * KV_W]
    mg_ref[...] = _mm(hb, w_ref[:, o:o + 2 * D_MODEL]); o += 2 * D_MODEL
    sm_ref[...] = _mm(hb, w_ref[:, o:o + SMALL_W])


def _pack_inproj_weight(w):
    c = np.cumsum([0, D_INNER, CONV_DIM, M_HEADS, D_ATTN, 6 * KV_W, 3 * Q_HEADS, 2 * D_MODEL])
    z, xbc, dt, q, kv, ng, mg = [w[:, c[i]:c[i + 1]] for i in range(7)]
    pad = jnp.zeros((D_MODEL, SMALL_W - M_HEADS - 3 * Q_HEADS), w.dtype)
    return jnp.concatenate([z, xbc, q, kv, mg, dt, ng, pad], axis=1).astype(BF16)


def _inproj(x2d, norm_g, w_packed, qk_norm, tm):
    n = x2d.shape[0]
    wtot = w_packed.shape[1]
    qkn2 = jnp.concatenate([qk_norm, qk_norm], axis=1)
    widths = (D_INNER, CONV_DIM, D_ATTN, 4 * KV_W, 2 * KV_W, 2 * D_MODEL, SMALL_W)
    return pl.pallas_call(
        _inproj_body,
        out_shape=[jax.ShapeDtypeStruct((n, w), F32) for w in widths],
        grid=(n // tm,),
        in_specs=[pl.BlockSpec((tm, D_MODEL), lambda i: (i, 0)),
                  _const_spec((1, D_MODEL)), _const_spec((D_MODEL, wtot)), _const_spec((4, LANES))],
        out_specs=[pl.BlockSpec((tm, w), lambda i: (i, 0)) for w in widths],
        compiler_params=_cparams(("arbitrary",)),
        name="inproj",
    )(x2d, norm_g.reshape(1, D_MODEL), w_packed, qkn2)


def _split_bf16(v):
    hi = v.astype(BF16)
    lo = (v - hi.astype(F32)).astype(BF16)
    return hi, lo


def _ssd_body(lv, xbc_ref, z_ref, sm_ref, conv0_ref, ssm0_ref, cw_ref, cb_ref, dtb_ref, alog_ref, dsk_ref,
              nw_ref, y_ref, convo_ref, ssmo_ref, xp_ref, st_ref):
    cl = SSD_CL
    c = pl.program_id(1)
    nchunk = pl.num_programs(1)
    hp = M_HEADS // M_GROUPS * M_HEADDIM

    @pl.when(c == 0)
    def _():
        xp_ref[...] = jnp.zeros(xp_ref.shape, F32)
        xp_ref[8 - (D_CONV - 1):8, :] = conv0_ref[0]
        st_ref[...] = ssm0_ref[0].reshape(M_HEADS * M_HEADDIM, D_STATE).T

    xp_ref[8:8 + lv, :] = xbc_ref[0]
    conv = cb_ref[...]
    for k in range(D_CONV):
        conv = conv + xp_ref[pl.ds(8 - (D_CONV - 1) + k, cl), :] * cw_ref[k:k + 1, :]
    tail = xp_ref[8 + lv - (D_CONV - 1):8 + lv, :]
    convo_ref[0] = tail
    xp_ref[8 - (D_CONV - 1):8, :] = tail
    xc = _silu(conv)
    xs = xc[:, :D_INNER]

    sm = sm_ref[0]
    z = z_ref[0]
    if lv < cl:
        sm = jnp.concatenate([sm, jnp.zeros((cl - lv, SMALL_W), F32)], axis=0)
        z = jnp.concatenate([z, jnp.zeros((cl - lv, D_INNER), F32)], axis=0)
    xdt_in = sm + dtb_ref[...]
    dt = jnp.maximum(xdt_in, 0.0) + jnp.log1p(jnp.exp(-jnp.abs(xdt_in)))
    row = lax.broadcasted_iota(jnp.int32, (cl, 1), 0)
    dt = jnp.where(row < lv, dt, 0.0)
    a_neg = -jnp.exp(alog_ref[...])
    da = dt * a_neg
    ri = lax.broadcasted_iota(jnp.int32, (cl, cl), 0)
    ci = lax.broadcasted_iota(jnp.int32, (cl, cl), 1)
    tril = ri >= ci
    acs = _mm(tril.astype(F32), da, precision=HIGHEST)
    alast = acs[cl - 1:cl, :]
    eye = (lax.broadcasted_iota(jnp.int32, (LANES, LANES), 0)
           == lax.broadcasted_iota(jnp.int32, (LANES, LANES), 1)).astype(F32)
    acs_t = _nt(eye, acs, precision=HIGHEST)

    ej = lax.broadcasted_iota(jnp.int32, (LANES, D_INNER), 0)
    el = lax.broadcasted_iota(jnp.int32, (LANES, D_INNER), 1)
    expand = (ej == el // M_HEADDIM).astype(BF16)
    stack = jnp.concatenate([dt, jnp.exp(acs), dt * jnp.exp(alast - acs)], axis=0)
    s_hi, s_lo = _split_bf16(stack)
    ex = _mm(s_hi, expand) + _mm(s_lo, expand)
    dt_e = ex[0:cl]
    es_e = ex[cl:2 * cl]
    w_e = ex[2 * cl:3 * cl]
    xdt = (xs * dt_e).astype(BF16)
    xw = (xs * w_e).astype(BF16)
    chunk_decay = es_e[cl - 1:cl, :]

    lane = lax.broadcasted_iota(jnp.int32, (1, LANES), 1)
    lo_mask = lane < M_HEADDIM
    eye_b = eye.astype(BF16)
    y_parts = []
    for g in range(M_GROUPS):
        bg = xc[:, D_INNER + g * D_STATE:D_INNER + (g + 1) * D_STATE].astype(BF16)
        cg = xc[:, D_INNER + (M_GROUPS + g) * D_STATE:D_INNER + (M_GROUPS + g + 1) * D_STATE].astype(BF16)
        cb = _nt(cg, bg)
        st_g = st_ref[:, g * hp:(g + 1) * hp]
        y_off = _mm(cg, st_g.astype(BF16)) * es_e[:, g * hp:(g + 1) * hp]
        bg_t = _nt(eye_b, bg).astype(BF16)
        st_ref[:, g * hp:(g + 1) * hp] = (st_g * chunk_decay[:, g * hp:(g + 1) * hp]
                                          + _mm(bg_t, xw[:, g * hp:(g + 1) * hp]))
        for jj in range(hp // LANES):
            col0 = g * hp + jj * LANES
            ms = []
            for j in (2 * (col0 // LANES), 2 * (col0 // LANES) + 1):
                diff = acs[:, j:j + 1] - acs_t[j:j + 1, :]
                dec = jnp.exp(jnp.where(tril, diff, -jnp.inf))
                ms.append((cb * dec).astype(BF16))
            x2 = xdt[:, col0:col0 + LANES]
            zero = jnp.zeros_like(x2)
            rhs = jnp.concatenate([jnp.where(lo_mask, x2, zero), jnp.where(lo_mask, zero, x2)], axis=0)
            y_parts.append(_mm(jnp.concatenate(ms, axis=1), rhs) + y_off[:, jj * LANES:(jj + 1) * LANES])
    y = jnp.concatenate(y_parts, axis=1)
    y = (y + dsk_ref[...] * xs) * _silu(z)
    outs = []
    for g in range(M_GROUPS):
        yg = y[:, g * hp:(g + 1) * hp]
        outs.append(yg * lax.rsqrt(jnp.mean(yg * yg, axis=-1, keepdims=True) + EPS))
    y = jnp.concatenate(outs, axis=1) * nw_ref[...]
    y_ref[0] = y[:lv].astype(BF16)

    @pl.when(c == nchunk - 1)
    def _():
        ssmo_ref[0] = st_ref[...].T.reshape(M_HEADS, M_HEADDIM, D_STATE)


def _ssd(xbc, z, small, conv0, ssm0, conv_w, conv_b, dt_bias, a_log, d_skip, ssm_norm):
    b, L = xbc.shape[:2]
    lv = SSD_CL if L % SSD_CL == 0 else L
    assert lv == SSD_CL or (L < SSD_CL and L % 8 == 0)
    nchunk = L // lv
    pad = lambda v: jnp.pad(v.astype(F32), (0, SMALL_W - M_HEADS)).reshape(1, SMALL_W)
    rep = lambda v: jnp.repeat(v.astype(F32), M_HEADDIM).reshape(1, D_INNER)
    return pl.pallas_call(
        functools.partial(_ssd_body, lv),
        out_shape=[jax.ShapeDtypeStruct((b, L, D_INNER), BF16),
                   jax.ShapeDtypeStruct((b, D_CONV - 1, CONV_DIM), F32),
                   jax.ShapeDtypeStruct((b, M_HEADS, M_HEADDIM, D_STATE), F32)],
        grid=(b, nchunk),
        in_specs=[pl.BlockSpec((1, lv, CONV_DIM), lambda i, c: (i, c, 0)),
                  pl.BlockSpec((1, lv, D_INNER), lambda i, c: (i, c, 0)),
                  pl.BlockSpec((1, lv, SMALL_W), lambda i, c: (i, c, 0)),
                  pl.BlockSpec((1, D_CONV - 1, CONV_DIM), lambda i, c: (i, 0, 0)),
                  pl.BlockSpec((1, M_HEADS, M_HEADDIM, D_STATE), lambda i, c: (i, 0, 0, 0)),
                  _const_spec((D_CONV, CONV_DIM)), _const_spec((1, CONV_DIM)),
                  _const_spec((1, SMALL_W)), _const_spec((1, SMALL_W)),
                  _const_spec((1, D_INNER)), _const_spec((1, D_INNER))],
        out_specs=[pl.BlockSpec((1, lv, D_INNER), lambda i, c: (i, c, 0)),
                   pl.BlockSpec((1, D_CONV - 1, CONV_DIM), lambda i, c: (i, 0, 0)),
                   pl.BlockSpec((1, M_HEADS, M_HEADDIM, D_STATE), lambda i, c: (i, 0, 0, 0))],
        scratch_shapes=[pltpu.VMEM((8 + SSD_CL, CONV_DIM), F32), pltpu.VMEM((D_STATE, D_INNER), F32)],
        compiler_params=_cparams(("arbitrary", "arbitrary")),
        name="ssd",
    )(xbc, z, small, conv0, ssm0, conv_w, conv_b.reshape(1, CONV_DIM), pad(dt_bias), pad(a_log),
      rep(d_skip), ssm_norm.reshape(1, D_INNER))


NO_LIMIT = 2 ** 30


def _bias_table_body(base, rc, cc, lo, hi, ncol_valid, tab_ref, o_ref):
    _, rb, cb = o_ref.shape
    r = pl.program_id(0) * rb + lax.broadcasted_iota(jnp.int32, (rb, cb), 0)
    c = pl.program_id(1) * cb + lax.broadcasted_iota(jnp.int32, (rb, cb), 1)
    d = base + rc * r + cc * c
    valid = (d >= lo) & (d <= hi) & (c < ncol_valid)
    max_exact = NUM_BUCKETS // 2
    dd = jnp.maximum(d, 0)
    df = jnp.maximum(dd, 1).astype(F32)
    large = max_exact + (jnp.log(df / max_exact) / math.log(MAX_DISTANCE / max_exact)
                         * (NUM_BUCKETS - max_exact)).astype(jnp.int32)
    large = jnp.minimum(large, NUM_BUCKETS - 1)
    bucket = jnp.where(dd < max_exact, dd, large)
    for hq in range(Q_HEADS):
        acc = jnp.zeros((rb, cb), F32)
        for bkt in range(NUM_BUCKETS):
            acc = jnp.where(bucket == bkt, tab_ref[bkt, hq], acc)
        o_ref[hq] = jnp.where(valid, acc, NEG)


def _bias_table(rel_bias, rows, cols, base, rc, cc, lo=0, hi=NO_LIMIT, ncol_valid=NO_LIMIT):
    rb = LANES if rows > LANES and rows % LANES == 0 else rows
    cb = 2048 if cols > 2048 and cols % 2048 == 0 else cols
    return pl.pallas_call(
        functools.partial(_bias_table_body, base, rc, cc, lo, hi, ncol_valid),
        out_shape=jax.ShapeDtypeStruct((Q_HEADS, rows, cols), F32),
        grid=(rows // rb, cols // cb),
        in_specs=[pl.BlockSpec(memory_space=pltpu.SMEM)],
        out_specs=pl.BlockSpec((Q_HEADS, rb, cb), lambda i, j: (0, i, j)),
        compiler_params=_cparams(("arbitrary", "arbitrary")),
        name="bias_table",
    )(rel_bias.astype(F32))


def _prompt_bias_tables(rel_bias, L):
    n_off = L // TQ + 1
    n_off_w = WINDOW // TQ + 3
    tab_s = _bias_table(rel_bias, n_off * TQ, TQ, -TQ, 1, -1)
    tab_w = _bias_table(rel_bias, n_off_w * TQ, TQ, -TQ, 1, -1, hi=WINDOW)
    n_c = L // CMP_BLOCK
    tab_ct = _bias_table(rel_bias, n_c, L, -(CMP_BLOCK - 1), -CMP_BLOCK, 1)
    return (tab_s.reshape(KV_HEADS, Q_PER_KV, n_off, TQ, TQ), tab_w.reshape(KV_HEADS, Q_PER_KV, n_off_w, TQ, TQ),
            tab_ct.reshape(KV_HEADS, Q_PER_KV, n_c, L))


def _compress_weights(cmp_pe, cmp_w1, cmp_w2):
    eye = jnp.eye(2, dtype=F32)
    w1 = jnp.einsum("hg,cjdf->cjhdgf", eye, cmp_w1).reshape(2, CMP_BLOCK // 2, 2 * LANES, 2 * CMP_HIDDEN)
    w2 = jnp.einsum("hg,cfd->chfgd", eye, cmp_w2).reshape(2, 2 * CMP_HIDDEN, LANES)
    pe = jnp.tile(cmp_pe, (1, 1, 2))
    return pe.astype(F32), w1.astype(BF16), w2.astype(BF16)


def _compress_rows(lo_ref, hi_ref, nblk, pe_ref, w1_ref, w2_ref, c):
    acc = jnp.zeros((2 * nblk, 2 * CMP_HIDDEN), F32)
    for jj in range(CMP_BLOCK // 2):
        cols = []
        for j in (2 * jj, 2 * jj + 1):
            take = pl.ds(j, nblk, stride=CMP_BLOCK)
            pe_j = pe_ref[c, j:j + 1, :]
            cols.append(jnp.concatenate([lo_ref[take, :] + pe_j, hi_ref[take, :] + pe_j], axis=0))
        acc = acc + _mm(jnp.concatenate(cols, axis=1).astype(BF16), w1_ref[c, jj])
    out = _mm(_silu(acc).astype(BF16), w2_ref[c])
    return jnp.concatenate([out[:nblk], out[nblk:]], axis=1)


def _compress_prompt_body(k0_ref, k1_ref, v0_ref, v1_ref, pe_ref, w1_ref, w2_ref, kn_ref, kc_ref, vc_ref):
    nblk = kc_ref.shape[1]
    kc_ref[0] = _head_rms(_compress_rows(k0_ref.at[0], k1_ref.at[0], nblk, pe_ref, w1_ref, w2_ref, 0), kn_ref[...])
    vc_ref[0] = _compress_rows(v0_ref.at[0], v1_ref.at[0], nblk, pe_ref, w1_ref, w2_ref, 1)


def _compress_prompt(paged, pe, w1, w2, kn2):
    b, L = paged.shape[:2]
    n_c = L // CMP_BLOCK
    col = lambda c: pl.BlockSpec((1, L, LANES), lambda i: (i, 0, c))
    return pl.pallas_call(
        _compress_prompt_body,
        out_shape=[jax.ShapeDtypeStruct((b, n_c, KV_W), F32)] * 2,
        grid=(b,),
        in_specs=[col(0), col(1), col(2), col(3),
                  _const_spec(pe.shape), _const_spec(w1.shape), _const_spec(w2.shape), _const_spec((1, LANES))],
        out_specs=[pl.BlockSpec((1, n_c, KV_W), lambda i: (i, 0, 0))] * 2,
        compiler_params=_cparams(("arbitrary",)),
        name="compress_prompt",
    )(paged, paged, paged, paged, pe, w1, w2, kn2)


def _online_softmax_step(s, carry, v):
    m, l, acc = carry
    m_new = jnp.maximum(m, jnp.max(s, axis=-1, keepdims=True))
    alpha = jnp.exp(m - m_new)
    p = jnp.exp(s - m_new)
    return (m_new, alpha * l + jnp.sum(p, axis=-1, keepdims=True), alpha * acc + _mm(p.astype(BF16), v))


def _select_blocks(imp, t, n_blk):
    big = 1e30
    j = lax.broadcasted_iota(jnp.int32, (1, imp.shape[1]), 1)
    cur = t // SEL_BLOCK
    forced = (j == 0) | (j == cur) | (j == cur - 1)
    causal = j * SEL_BLOCK <= t
    score = jnp.where(forced, big, jnp.where(causal, imp, -big))
    rank = jnp.zeros(imp.shape, F32)
    for i in range(n_blk):
        ci = score[:, i:i + 1]
        ahead = (ci > score) | ((ci == score) & (i < j))
        rank = rank + jnp.where(ahead, 1.0, 0.0)
    return jnp.where(rank < SEL_TOPK, 1.0, 0.0)


def _nsa_prompt_body(n_sel, q_ref, ks_ref, vs_ref, kw_ref, vw_ref, kc_ref, vc_ref, sm_ref, tabs_ref, tabw_ref,
                     tabc_ref, o_ref, kaug_scr, vaug_scr, kwin_scr, vwin_scr, s_scr, p_scr):
    h = pl.program_id(1)
    qi = pl.program_id(2)
    rows = Q_PER_KV * TQ
    lane = lax.broadcasted_iota(jnp.int32, (1, LANES), 1)
    mine = (lane // HEAD_DIM) == (h % 2)

    @pl.when(qi == 0)
    def _():
        seq = kaug_scr.shape[0]
        key = lax.broadcasted_iota(jnp.int32, (seq, LANES), 0)
        blk = lax.broadcasted_iota(jnp.int32, (seq, LANES), 1)
        kaug_scr[:, 0:LANES] = ks_ref[0].astype(BF16)
        kaug_scr[:, LANES:2 * LANES] = jnp.where(blk == key // SEL_BLOCK, NEG, 0.0).astype(BF16)
        vaug_scr[...] = jnp.where(mine, vs_ref[0], 1.0).astype(BF16)
        kwin_scr[...] = kw_ref[0].astype(BF16)
        vwin_scr[...] = jnp.where(mine, vw_ref[0], 1.0).astype(BF16)

    qb = q_ref[0]
    q4 = jnp.concatenate([qb[:, g * HEAD_DIM:(g + 1) * HEAD_DIM] for g in range(Q_PER_KV)], axis=0)
    qpad = jnp.where(mine, jnp.concatenate([q4, q4], axis=1), 0.0).astype(BF16)

    sub = 64
    even = (h % 2) == 0
    pick = lambda a: jnp.where(even, a[:, :HEAD_DIM], a[:, HEAD_DIM:])
    other = lambda a: jnp.where(even, a[:, HEAD_DIM:], a[:, :HEAD_DIM])

    def attend(q_lhs, k_slab, v_slab, tab_ref, kt0, n_kt):
        width = n_kt * TQ
        n_half = 2
        hrows = rows // n_half
        for hf in range(n_half):
            s_scr[hf * hrows:(hf + 1) * hrows, 0:width] = _nt(q_lhs[hf * hrows:(hf + 1) * hrows], k_slab)
        accs = []
        for hf in range(n_half):
            for rb in range(hf * hrows // sub, (hf + 1) * hrows // sub):
                g, part = divmod(rb, TQ // sub)
                rr = slice(rb * sub, (rb + 1) * sub)
                tr = slice(part * sub, (part + 1) * sub)
                run = None
                for kt in range(n_kt):
                    cc = slice(kt * TQ, (kt + 1) * TQ)
                    s = s_scr[rr, cc] + tab_ref[0, g, jnp.maximum(qi - (kt0 + kt) + 1, 0), tr, :]
                    s_scr[rr, cc] = s
                    run = s if run is None else jnp.maximum(run, s)
                mb = jnp.broadcast_to(jnp.max(run, axis=-1, keepdims=True), (sub, TQ))
                for kt in range(n_kt):
                    cc = slice(kt * TQ, (kt + 1) * TQ)
                    p_scr[rr, cc] = jnp.exp(s_scr[rr, cc] - mb).astype(BF16)
            accs.append(_mm(p_scr[hf * hrows:(hf + 1) * hrows, 0:width], v_slab))
        acc = jnp.concatenate(accs, axis=0)
        return pick(acc) / jnp.maximum(other(acc), 1e-30)

    n_tiles = kaug_scr.shape[0] // TQ
    n_win = min(WINDOW // TQ + 1, n_tiles)
    w0 = jnp.maximum(qi - (n_win - 1), 0)
    w_rows = pl.ds(pl.multiple_of(w0 * TQ, TQ), n_win * TQ)
    o_w = attend(qpad, kwin_scr[w_rows, :], vwin_scr[w_rows, :], tabw_ref, w0, n_win)

    n_c = kc_ref.shape[1]
    bias_ct = jnp.concatenate([tabc_ref[0, g] for g in range(Q_PER_KV)], axis=1)
    s_ct = _nt(kc_ref[0].astype(BF16), qpad) + bias_ct
    e_ct = jnp.where(bias_ct > 0.5 * NEG, jnp.exp(s_ct - jnp.max(s_ct, axis=0, keepdims=True)), 0.0)
    p_ct = e_ct / jnp.maximum(jnp.sum(e_ct, axis=0, keepdims=True), 1e-30)
    o_c = lax.dot_general(p_ct.astype(BF16), vc_ref[0].astype(BF16), (((0,), (0,)), ((), ())),
                          preferred_element_type=F32)

    imp_c = sum(p_ct[:, g * TQ:(g + 1) * TQ] for g in range(Q_PER_KV))
    nr = -(-n_sel // 8) * 8
    pair = (lax.broadcasted_iota(jnp.int32, (nr, n_c), 0)
            == lax.broadcasted_iota(jnp.int32, (nr, n_c), 1) // (SEL_BLOCK // CMP_BLOCK)).astype(BF16)
    c_hi = imp_c.astype(BF16)
    c_mid = (imp_c - c_hi.astype(F32)).astype(BF16)
    c_lo = (imp_c - c_hi.astype(F32) - c_mid.astype(F32)).astype(BF16)
    imp = _mm(pair, c_hi) + _mm(pair, c_mid) + _mm(pair, c_lo)
    t = qi * TQ + lax.broadcasted_iota(jnp.int32, (1, TQ), 1)
    j = lax.broadcasted_iota(jnp.int32, (nr, 1), 0)
    cur = t // SEL_BLOCK
    forced = (j == 0) | (j == cur) | (j == cur - 1)
    score = jnp.where(forced, 1e30, jnp.where(j * SEL_BLOCK <= t, imp, -1e30))
    rank = jnp.zeros((nr, TQ), F32)
    for i in range(n_sel):
        ri = score[i:i + 1, :]
        rank = rank + jnp.where((ri > score) | ((ri == score) & (i < j)), 1.0, 0.0)
    drop_t = jnp.where(rank < SEL_TOPK, 0.0, 1.0)
    drop_t = jnp.concatenate([drop_t, jnp.ones((LANES - nr, TQ), F32)], axis=0)
    drop = drop_t.T.astype(BF16)
    qaug = jnp.concatenate([qpad, jnp.concatenate([drop] * Q_PER_KV, axis=0)], axis=1)

    ext_step = 2
    extents = [min((e + 1) * ext_step, n_tiles) for e in range(-(-n_tiles // ext_step))]
    o_s = lax.switch(qi // ext_step,
                     [functools.partial(lambda n: attend(qaug, kaug_scr[0:n * TQ, :], vaug_scr[0:n * TQ, :],
                                                         tabs_ref, 0, n), n) for n in extents])
    o_c = pick(o_c)

    gates = jax.nn.sigmoid(pltpu.roll(sm_ref[0], SMALL_W - NSA_G_OFF - h * (3 * Q_PER_KV), axis=1))
    outs = []
    for g in range(Q_PER_KV):
        rs = slice(g * TQ, (g + 1) * TQ)
        outs.append(gates[:, 3 * g:3 * g + 1] * o_c[rs] + gates[:, 3 * g + 1:3 * g + 2] * o_s[rs]
                    + gates[:, 3 * g + 2:3 * g + 3] * o_w[rs])
    o_ref[0] = jnp.concatenate(outs, axis=1).astype(BF16)


def _nsa_prompt(qn, paged, win, kc, vc, small, tab_s, tab_w, tab_c):
    b, L = qn.shape[:2]
    assert L % TKC == 0
    n_c = L // CMP_BLOCK
    rows = Q_PER_KV * TQ
    seq = lambda col: pl.BlockSpec((1, L, LANES), lambda i, h, q: (i, 0, col(h)))
    cmp = pl.BlockSpec((1, n_c, LANES), lambda i, h, q: (i, 0, h // 2))
    return pl.pallas_call(
        functools.partial(_nsa_prompt_body, L // SEL_BLOCK),
        out_shape=jax.ShapeDtypeStruct((b, L, D_ATTN), BF16),
        grid=(b, KV_HEADS, L // TQ),
        in_specs=[pl.BlockSpec((1, TQ, KV_W), lambda i, h, q: (i, q, h)),
                  seq(lambda h: 4 + h // 2), seq(lambda h: 6 + h // 2),
                  seq(lambda h: h // 2), seq(lambda h: 2 + h // 2),
                  cmp, cmp,
                  pl.BlockSpec((1, TQ, SMALL_W), lambda i, h, q: (i, q, 0)),
                  pl.BlockSpec((1,) + tab_s.shape[1:], lambda i, h, q: (h, 0, 0, 0, 0)),
                  pl.BlockSpec((1,) + tab_w.shape[1:], lambda i, h, q: (h, 0, 0, 0, 0)),
                  pl.BlockSpec((1, Q_PER_KV, n_c, TQ), lambda i, h, q: (h, 0, 0, q))],
        out_specs=pl.BlockSpec((1, TQ, KV_W), lambda i, h, q: (i, q, h)),
        scratch_shapes=[pltpu.VMEM((L, 2 * LANES), BF16), pltpu.VMEM((L, LANES), BF16),
                        pltpu.VMEM((L, LANES), BF16), pltpu.VMEM((L, LANES), BF16),
                        pltpu.VMEM((rows, L), F32), pltpu.VMEM((rows, L), BF16)],
        compiler_params=_cparams(("arbitrary", "arbitrary", "arbitrary")),
        name="nsa_prompt",
    )(qn, paged, paged, win, win, kc, vc, small, tab_s, tab_w, tab_c)


def _merge_body(x_ref, y_ref, o_ref, mg_ref, wa_ref, wb_ref, wo_ref, out_ref):
    ya = _mm(y_ref[...], wa_ref[...])
    yb = _mm(o_ref[...], wb_ref[...])
    gm = jax.nn.sigmoid(mg_ref[...])
    mix = (gm[:, :D_MODEL] * ya + gm[:, D_MODEL:] * yb).astype(BF16)
    out_ref[...] = x_ref[...] + _mm(mix, wo_ref[...])


def _merge(x2d, y2d, o2d, mg, wa, wb, wo, tm):
    n = x2d.shape[0]
    row = lambda w: pl.BlockSpec((tm, w), lambda i: (i, 0))
    return pl.pallas_call(
        _merge_body,
        out_shape=jax.ShapeDtypeStruct((n, D_MODEL), F32),
        grid=(n // tm,),
        in_specs=[row(D_MODEL), row(D_INNER), row(D_ATTN), row(2 * D_MODEL),
                  _const_spec((D_INNER, D_MODEL)), _const_spec((D_ATTN, D_MODEL)), _const_spec((D_MODEL, D_MODEL))],
        out_specs=row(D_MODEL),
        compiler_params=_cparams(("arbitrary",)),
        name="merge",
    )(x2d, y2d, o2d, mg, wa.astype(BF16), wb.astype(BF16), wo.astype(BF16))


def _layer_prompt(x, lw, f_bias, tm):
    (ffn1_norm, ffn1_w_in, ffn1_w_out, mix_norm, w_packed, conv_w, conv_b, dt_bias, a_log, d_skip, ssm_norm,
     qk_norm, cmp_prep, w_branch_ssm, w_branch_attn, w_out, ffn2_norm, ffn2_w_in, ffn2_w_out) = lw
    b, L = x.shape[:2]
    n = b * L
    tm_wide = 2 * tm if n % (2 * tm) == 0 else tm
    x1 = _ffn(x.reshape(n, D_MODEL), ffn1_norm, ffn1_w_in, ffn1_w_out, tm_wide)
    z, xbc, qn, paged, win, mg, small = _inproj(x1, mix_norm, w_packed, qk_norm, tm)
    r3 = lambda a: a.reshape(b, L, a.shape[-1])
    conv0 = jnp.zeros((b, D_CONV - 1, CONV_DIM), F32)
    ssm0 = jnp.zeros((b, M_HEADS, M_HEADDIM, D_STATE), F32)
    y, new_conv, new_ssm = _ssd(r3(xbc), r3(z), r3(small), conv0, ssm0, conv_w, conv_b, dt_bias, a_log, d_skip,
                                ssm_norm)
    pe, w1, w2 = cmp_prep
    kn2 = jnp.concatenate([qk_norm[1], qk_norm[1]]).reshape(1, LANES)
    kc, vc = _compress_prompt(r3(paged), pe, w1, w2, kn2)
    tab_s, tab_w, tab_c = _prompt_bias_tables(f_bias, L)
    o = _nsa_prompt(r3(qn), r3(paged), r3(win), kc, vc, r3(small), tab_s, tab_w, tab_c)
    x2 = _merge(x1, y.reshape(n, D_INNER), o.reshape(n, D_ATTN), mg, w_branch_ssm, w_branch_attn, w_out, tm_wide)
    x3 = _ffn(x2, ffn2_norm, ffn2_w_in, ffn2_w_out, tm_wide)
    n_keep = min(WINDOW, L)
    new_kv = paged.reshape(b, L, 4, KV_HEADS, HEAD_DIM)
    new_win = r3(win)[:, L - n_keep:].reshape(b, n_keep, 2, KV_HEADS, HEAD_DIM)
    return x3.reshape(b, L, D_MODEL), new_kv, new_win, new_ssm, new_conv


PAGES_PER_STEP = 32


def _sample_qpad(qb):
    ls = qb.shape[0]
    blocks = []
    for hq in range(Q_HEADS):
        h = hq // Q_PER_KV
        parts = []
        if h > 0:
            parts.append(jnp.zeros((ls, h * HEAD_DIM), F32))
        parts.append(qb[:, hq * HEAD_DIM:(hq + 1) * HEAD_DIM])
        if h < KV_HEADS - 1:
            parts.append(jnp.zeros((ls, (KV_HEADS - 1 - h) * HEAD_DIM), F32))
        blocks.append(jnp.concatenate(parts, axis=1))
    return jnp.concatenate(blocks, axis=0).astype(BF16)


def _sample_head_piece(o, hq, ls):
    h = hq // Q_PER_KV
    return o[hq * ls:(hq + 1) * ls, h * HEAD_DIM:(h + 1) * HEAD_DIM]


def _chunk_copies(pt_ref, cache_ref, buf, sem, cols, width, bb, ii, sl, p):
    page = pt_ref[bb, ii * PAGES_PER_STEP + p]
    row0 = pl.multiple_of(p * PAGE_SIZE, PAGE_SIZE)
    return [pltpu.make_async_copy(cache_ref.at[page, :, pl.ds(c0, width)],
                                  buf.at[sl, k, pl.ds(row0, PAGE_SIZE), :], sem.at[sl])
            for k, c0 in enumerate(cols)]


def _stream_chunks(pt_ref, cache_ref, buf, sem, cols, width):
    b = pl.program_id(0)
    i = pl.program_id(1)
    nch = pl.num_programs(1)
    step = b * nch + i
    slot = step % 2

    def run(bb, ii, sl, start):
        def body(p, carry):
            for cp in _chunk_copies(pt_ref, cache_ref, buf, sem, cols, width, bb, ii, sl, p):
                cp.start() if start else cp.wait()
            return carry
        lax.fori_loop(0, PAGES_PER_STEP, body, 0)

    @pl.when(step == 0)
    def _():
        run(b, i, slot, True)

    nxt = step + 1

    @pl.when(nxt < pl.num_programs(0) * nch)
    def _():
        run(nxt // nch, nxt % nch, nxt % 2, True)

    run(b, i, slot, False)
    return slot


def _cmp_sample_body(past_len, pt_ref, cache_ref, q_ref, biasc_ref, pe_ref, w1_ref, w2_ref, kn_ref,
                     oc_ref, selm_ref, buf, sem, kc_s, vc_s):
    i = pl.program_id(1)
    slot = _stream_chunks(pt_ref, cache_ref, buf, sem, [k * LANES for k in range(4)], LANES)
    nblk = PAGES_PER_STEP * PAGE_SIZE // CMP_BLOCK
    rows0 = pl.multiple_of(i * nblk, nblk)
    kc_s[pl.ds(rows0, nblk), :] = _head_rms(
        _compress_rows(buf.at[slot, 0], buf.at[slot, 1], nblk, pe_ref, w1_ref, w2_ref, 0), kn_ref[...])
    vc_s[pl.ds(rows0, nblk), :] = _compress_rows(buf.at[slot, 2], buf.at[slot, 3], nblk, pe_ref, w1_ref, w2_ref, 1)

    @pl.when(i == pl.num_programs(1) - 1)
    def _():
        ls = q_ref.shape[1]
        n_c = kc_s.shape[0]
        s = _nt(_sample_qpad(q_ref[0]), kc_s[...].astype(BF16)) + biasc_ref[...]
        e = jnp.exp(s - jnp.max(s, axis=-1, keepdims=True))
        p = e / jnp.maximum(jnp.sum(e, axis=-1, keepdims=True), 1e-30)
        oc = _mm(p.astype(BF16), vc_s[...].astype(BF16))
        oc_ref[0] = jnp.concatenate([_sample_head_piece(oc, hq, ls) for hq in range(Q_HEADS)], axis=1)
        imp_c = jnp.sum(p.reshape(KV_HEADS, Q_PER_KV, ls, n_c), axis=1).reshape(KV_HEADS * ls, n_c)
        w = selm_ref.shape[2]
        pair = (lax.broadcasted_iota(jnp.int32, (n_c, w), 0) // (SEL_BLOCK // CMP_BLOCK)
                == lax.broadcasted_iota(jnp.int32, (n_c, w), 1)).astype(F32)
        imp = _mm(imp_c, pair, precision=HIGHEST)
        t = past_len + lax.broadcasted_iota(jnp.int32, (KV_HEADS * ls, 1), 0) % ls
        selm_ref[0] = _select_blocks(imp, t, past_len // SEL_BLOCK + 1)


def _nsa_sample_body(pt_ref, cache_ref, q_ref, selm_ref, oc_ref, sm_ref, pnew_ref, cwin_ref, wnew_ref,
                     biass_ref, biasw_ref, biasn_ref, o_ref, buf, sem, m_s, l_s, a_s):
    i = pl.program_id(1)
    slot = _stream_chunks(pt_ref, cache_ref, buf, sem, [2 * KV_W, 3 * KV_W], KV_W)
    ls = q_ref.shape[1]
    rows = Q_HEADS * ls
    tk = PAGES_PER_STEP * PAGE_SIZE
    qpad = _sample_qpad(q_ref[0])

    @pl.when(i == 0)
    def _():
        m_s[...] = jnp.full(m_s.shape, NEG, F32)
        l_s[...] = jnp.zeros(l_s.shape, F32)
        a_s[...] = jnp.zeros(a_s.shape, F32)

    sel = selm_ref[0]
    w = sel.shape[1]
    sel = jnp.broadcast_to(sel.reshape(KV_HEADS, 1, ls, w), (KV_HEADS, Q_PER_KV, ls, w)).reshape(rows, w)
    blk = lax.broadcasted_iota(jnp.int32, (w, tk), 0)
    key = lax.broadcasted_iota(jnp.int32, (w, tk), 1)
    spread = (blk == i * (tk // SEL_BLOCK) + key // SEL_BLOCK).astype(BF16)
    keep = _mm(sel.astype(BF16), spread)
    s = _nt(qpad, buf[slot, 0].astype(BF16)) + biass_ref[...] + jnp.where(keep > 0.5, 0.0, NEG)
    m, l, a = _online_softmax_step(s, (m_s[...], l_s[...], a_s[...]), buf[slot, 1].astype(BF16))
    m_s[...] = m
    l_s[...] = l
    a_s[...] = a

    @pl.when(i == pl.num_programs(1) - 1)
    def _():
        def pad_rows(x):
            return jnp.concatenate([x, jnp.zeros((LANES - ls, x.shape[1]), F32)], axis=0).astype(BF16)

        pn = pnew_ref[0]
        wn = wnew_ref[0]
        s_n = _nt(qpad, pad_rows(pn[:, 2 * KV_W:3 * KV_W])) + biasn_ref[...]
        _, l2, a2 = _online_softmax_step(s_n, (m, l, a), pad_rows(pn[:, 3 * KV_W:4 * KV_W]))
        o_s = a2 / jnp.maximum(l2, 1e-30)
        cw = cwin_ref[0]
        init = (jnp.full((rows, 1), NEG, F32), jnp.zeros((rows, 1), F32), jnp.zeros((rows, KV_W), F32))
        c1 = _online_softmax_step(_nt(qpad, cw[:, 0:KV_W].astype(BF16)) + biasw_ref[...], init,
                                  cw[:, KV_W:2 * KV_W].astype(BF16))
        s_wn = _nt(qpad, pad_rows(wn[:, 0:KV_W])) + biasn_ref[...]
        _, l3, a3 = _online_softmax_step(s_wn, c1, pad_rows(wn[:, KV_W:2 * KV_W]))
        o_w = a3 / jnp.maximum(l3, 1e-30)
        gates = jax.nn.sigmoid(sm_ref[0])
        oc = oc_ref[0]
        outs = []
        for hq in range(Q_HEADS):
            g0 = NSA_G_OFF + 3 * hq
            outs.append(gates[:, g0:g0 + 1] * oc[:, hq * HEAD_DIM:(hq + 1) * HEAD_DIM]
                        + gates[:, g0 + 1:g0 + 2] * _sample_head_piece(o_s, hq, ls)
                        + gates[:, g0 + 2:g0 + 3] * _sample_head_piece(o_w, hq, ls))
        o_ref[0] = jnp.concatenate(outs, axis=1).astype(BF16)


def _sample_bias_tables(rel_bias, past_len, ls, n_buf):
    rows2d = lambda t: t.reshape(Q_HEADS * ls, t.shape[2])
    bias_s = _bias_table(rel_bias, ls, past_len, past_len, 1, -1)
    n_c = past_len // CMP_BLOCK
    bias_c = _bias_table(rel_bias, ls, n_c, past_len - (CMP_BLOCK - 1), 1, -CMP_BLOCK)
    bias_w = _bias_table(rel_bias, ls, n_buf, n_buf, 1, -1, hi=WINDOW)
    bias_n = _bias_table(rel_bias, ls, LANES, 0, 1, -1, ncol_valid=ls)
    return rows2d(bias_s), rows2d(bias_c), rows2d(bias_w), rows2d(bias_n)


def _nsa_sample(qn, paged, win, small, cache_pages, page_table, cache_win, rel_bias, cmp_prep, kn2):
    b, ls = qn.shape[:2]
    n_pages = page_table.shape[1]
    past_len = n_pages * PAGE_SIZE
    n_buf = cache_win.shape[1]
    assert n_pages % PAGES_PER_STEP == 0 and ls <= SEL_BLOCK and ls % 8 == 0 and n_buf == min(WINDOW, past_len)
    nch = n_pages // PAGES_PER_STEP
    tk = PAGES_PER_STEP * PAGE_SIZE
    n_c = past_len // CMP_BLOCK
    n_sel = past_len // SEL_BLOCK + 1
    w_sel = -(-n_sel // LANES) * LANES
    rows = Q_HEADS * ls
    pe, w1, w2 = cmp_prep
    bias_s, bias_c, bias_w, bias_n = _sample_bias_tables(rel_bias, past_len, ls, n_buf)
    tok = lambda wd: pl.BlockSpec((1, ls, wd), lambda i, c, pt: (i, 0, 0))
    const = lambda shape: pl.BlockSpec(shape, lambda i, c, pt: (0,) * len(shape), pipeline_mode=pl.Buffered(1))
    o_c, selm = pl.pallas_call(
        functools.partial(_cmp_sample_body, past_len),
        out_shape=[jax.ShapeDtypeStruct((b, ls, D_ATTN), F32),
                   jax.ShapeDtypeStruct((b, KV_HEADS * ls, w_sel), F32)],
        grid_spec=pltpu.PrefetchScalarGridSpec(
            num_scalar_prefetch=1, grid=(b, nch),
            in_specs=[pl.BlockSpec(memory_space=pl.ANY), tok(D_ATTN), const((rows, n_c)),
                      const(pe.shape), const(w1.shape), const(w2.shape), const((1, LANES))],
            out_specs=[tok(D_ATTN), pl.BlockSpec((1, KV_HEADS * ls, w_sel), lambda i, c, pt: (i, 0, 0))],
            scratch_shapes=[pltpu.VMEM((2, 4, tk, LANES), F32), pltpu.SemaphoreType.DMA((2,)),
                            pltpu.VMEM((n_c, KV_W), F32), pltpu.VMEM((n_c, KV_W), F32)]),
        compiler_params=_cparams(("arbitrary", "arbitrary")),
        name="nsa_sample_compress",
    )(page_table, cache_pages, qn, bias_c, pe, w1, w2, kn2)
    return pl.pallas_call(
        _nsa_sample_body,
        out_shape=jax.ShapeDtypeStruct((b, ls, D_ATTN), BF16),
        grid_spec=pltpu.PrefetchScalarGridSpec(
            num_scalar_prefetch=1, grid=(b, nch),
            in_specs=[pl.BlockSpec(memory_space=pl.ANY), tok(D_ATTN),
                      pl.BlockSpec((1, KV_HEADS * ls, w_sel), lambda i, c, pt: (i, 0, 0)),
                      tok(D_ATTN), tok(SMALL_W), tok(4 * KV_W),
                      pl.BlockSpec((1, n_buf, 2 * KV_W), lambda i, c, pt: (i, 0, 0)), tok(2 * KV_W),
                      pl.BlockSpec((rows, tk), lambda i, c, pt: (0, c)),
                      const((rows, n_buf)), const((rows, LANES))],
            out_specs=tok(D_ATTN),
            scratch_shapes=[pltpu.VMEM((2, 2, tk, KV_W), F32), pltpu.SemaphoreType.DMA((2,)),
                            pltpu.VMEM((rows, 1), F32), pltpu.VMEM((rows, 1), F32), pltpu.VMEM((rows, KV_W), F32)]),
        compiler_params=_cparams(("arbitrary", "arbitrary")),
        name="nsa_sample",
    )(page_table, cache_pages, qn, selm, o_c, small, paged, cache_win, win, bias_s, bias_w, bias_n)


def _layer_sample(x, ssm0, conv0, cache_kv_l, page_table, cache_win_l, lw, f_bias):
    (ffn1_norm, ffn1_w_in, ffn1_w_out, mix_norm, w_packed, conv_w, conv_b, dt_bias, a_log, d_skip, ssm_norm,
     qk_norm, cmp_prep, w_branch_ssm, w_branch_attn, w_out, ffn2_norm, ffn2_w_in, ffn2_w_out) = lw
    b, ls = x.shape[:2]
    n = b * ls
    tm = n
    x1 = _ffn(x.reshape(n, D_MODEL), ffn1_norm, ffn1_w_in, ffn1_w_out, tm)
    z, xbc, qn, paged, win, mg, small = _inproj(x1, mix_norm, w_packed, qk_norm, tm)
    r3 = lambda a: a.reshape(b, ls, a.shape[-1])
    y, new_conv, new_ssm = _ssd(r3(xbc), r3(z), r3(small), conv0, ssm0, conv_w, conv_b, dt_bias, a_log, d_skip,
                                ssm_norm)
    kn2 = jnp.concatenate([qk_norm[1], qk_norm[1]]).reshape(1, LANES)
    n_pool = cache_kv_l.shape[0]
    cache_pages = cache_kv_l.reshape(n_pool, PAGE_SIZE, 4 * KV_W)
    n_buf = cache_win_l.shape[1]
    cwin = cache_win_l.reshape(b, n_buf, 2 * KV_W)
    o = _nsa_sample(r3(qn), r3(paged), r3(win), r3(small), cache_pages, page_table, cwin, f_bias, cmp_prep, kn2)
    x2 = _merge(x1, y.reshape(n, D_INNER), o.reshape(n, D_ATTN), mg, w_branch_ssm, w_branch_attn, w_out, tm)
    x3 = _ffn(x2, ffn2_norm, ffn2_w_in, ffn2_w_out, tm)
    n_keep = min(WINDOW, n_buf + ls)
    new_kv = paged.reshape(b, ls, 4, KV_HEADS, HEAD_DIM)
    win_all = jnp.concatenate([cwin, r3(win)], axis=1)
    new_win = win_all[:, n_buf + ls - n_keep:].reshape(b, n_keep, 2, KV_HEADS, HEAD_DIM)
    return x3.reshape(b, ls, D_MODEL), new_kv, new_win, new_ssm, new_conv


def _layer_weights(l, ffn1_norm, ffn1_w_in, ffn1_w_out, mix_norm, w_in_proj, conv_w, conv_b, dt_bias, a_log,
                   d_skip, ssm_norm, qk_norm, cmp_pe, cmp_w1, cmp_w2, w_branch_ssm, w_branch_attn, w_out,
                   ffn2_norm, ffn2_w_in, ffn2_w_out):
    return (ffn1_norm[l], ffn1_w_in[l], ffn1_w_out[l], mix_norm[l], _pack_inproj_weight(w_in_proj[l]), conv_w[l],
            conv_b[l], dt_bias[l], a_log[l], d_skip[l], ssm_norm[l], qk_norm[l],
            _compress_weights(cmp_pe[l], cmp_w1[l], cmp_w2[l]), w_branch_ssm[l], w_branch_attn[l], w_out[l],
            ffn2_norm[l], ffn2_w_in[l], ffn2_w_out[l])


def kernel(x_prompt, x_sample, cache_kv, cache_win, state_ssm, state_conv, page_table, rel_bias, ffn1_norm, ffn1_w_in, ffn1_w_out, mix_norm, w_in_proj, conv_w, conv_b, dt_bias, a_log, d_skip, ssm_norm, qk_norm, cmp_pe, cmp_w1, cmp_w2, w_branch_ssm, w_branch_attn, w_out, ffn2_norm, ffn2_w_in, ffn2_w_out):
    depth = ffn1_norm.shape[0]
    f_bias = rel_bias
    xp, xs = x_prompt, x_sample
    outs_p, outs_s = [], []
    for l in range(depth):
        lw = _layer_weights(l, ffn1_norm, ffn1_w_in, ffn1_w_out, mix_norm, w_in_proj, conv_w, conv_b, dt_bias,
                            a_log, d_skip, ssm_norm, qk_norm, cmp_pe, cmp_w1, cmp_w2, w_branch_ssm,
                            w_branch_attn, w_out, ffn2_norm, ffn2_w_in, ffn2_w_out)
        xp, *rp = _layer_prompt(xp, lw, f_bias, 256)
        outs_p.append(rp)
        xs, *rs = _layer_sample(xs, state_ssm[l], state_conv[l], cache_kv[l], page_table, cache_win[l], lw, f_bias)
        outs_s.append(rs)
    stack = lambda outs, k: jnp.stack([o[k] for o in outs])
    return ((xp, xs) + tuple(stack(outs_p, k) for k in range(4)) + tuple(stack(outs_s, k) for k in range(4)))
```

```python
import functools
import math

import jax
import jax.numpy as jnp
import numpy as np
from jax import lax
from jax.experimental import pallas as pl
from jax.experimental.pallas import tpu as pltpu

F32 = jnp.float32
BF16 = jnp.bfloat16
HIGHEST = lax.Precision.HIGHEST

D_MODEL = 1024
D_FF = 2816
D_INNER = 2048
M_HEADDIM = 64
M_HEADS = 32
M_GROUPS = 4
D_STATE = 128
D_CONV = 4
CONV_DIM = D_INNER + 2 * M_GROUPS * D_STATE
HEAD_DIM = 64
Q_HEADS = 16
KV_HEADS = 4
Q_PER_KV = 4
D_ATTN = Q_HEADS * HEAD_DIM
KV_W = KV_HEADS * HEAD_DIM
CMP_BLOCK = 32
CMP_HIDDEN = 128
SEL_BLOCK = 64
SEL_TOPK = 16
WINDOW = 512
NUM_BUCKETS = 32
MAX_DISTANCE = 2048
PAGE_SIZE = 128
EPS = 1e-6
NEG = -1e30

LANES = 128
SSD_CL = 128
TQ = 128
TKC = 256
VMEM_LIMIT = 56 * 1024 * 1024
SMALL_W = 128
NSA_G_OFF = M_HEADS


def _cparams(sem):
    return pltpu.CompilerParams(dimension_semantics=sem, vmem_limit_bytes=VMEM_LIMIT)


def _const_spec(shape):
    nd = len(shape)
    return pl.BlockSpec(shape, lambda *a: (0,) * nd, pipeline_mode=pl.Buffered(1))


def _nt(a, b, precision=None):
    return lax.dot_general(a, b, (((1,), (1,)), ((), ())), preferred_element_type=F32, precision=precision)


def _mm(a, b, precision=None):
    return jnp.dot(a, b, preferred_element_type=F32, precision=precision)


def _rms_rows(x, g):
    return x * lax.rsqrt(jnp.mean(x * x, axis=-1, keepdims=True) + EPS) * g


def _silu(x):
    return x * jax.nn.sigmoid(x)


def _head_rms(a, gain2):
    lane = lax.broadcasted_iota(jnp.int32, (1, LANES), 1)
    lo_mask = lane < HEAD_DIM
    outs = []
    for c in range(a.shape[1] // LANES):
        blk = a[:, c * LANES:(c + 1) * LANES]
        sq = blk * blk
        lo = jnp.sum(jnp.where(lo_mask, sq, 0.0), axis=-1, keepdims=True)
        hi = jnp.sum(jnp.where(lo_mask, 0.0, sq), axis=-1, keepdims=True)
        r = jnp.where(lo_mask, lax.rsqrt(lo / HEAD_DIM + EPS), lax.rsqrt(hi / HEAD_DIM + EPS))
        outs.append(blk * r * gain2)
    return outs[0] if len(outs) == 1 else jnp.concatenate(outs, axis=1)


def _ffn_body(x_ref, g_ref, wg_ref, wu_ref, wo_ref, o_ref):
    x = x_ref[...]
    hb = _rms_rows(x, g_ref[...]).astype(BF16)
    gate = _mm(hb, wg_ref[...])
    up = _mm(hb, wu_ref[...])
    act = (_silu(gate) * up).astype(BF16)
    o_ref[...] = x + 0.5 * _mm(act, wo_ref[...])


def _ffn(x2d, norm_g, w_in, w_out, tm):
    n = x2d.shape[0]
    wg = w_in[:, :D_FF].astype(BF16)
    wu = w_in[:, D_FF:].astype(BF16)
    wo = w_out.astype(BF16)
    return pl.pallas_call(
        _ffn_body,
        out_shape=jax.ShapeDtypeStruct((n, D_MODEL), F32),
        grid=(n // tm,),
        in_specs=[pl.BlockSpec((tm, D_MODEL), lambda i: (i, 0)),
                  _const_spec((1, D_MODEL)),
                  _const_spec((D_MODEL, D_FF)), _const_spec((D_MODEL, D_FF)), _const_spec((D_FF, D_MODEL))],
        out_specs=pl.BlockSpec((tm, D_MODEL), lambda i: (i, 0)),
        compiler_params=_cparams(("arbitrary",)),
        name="ffn",
    )(x2d, norm_g.reshape(1, D_MODEL), wg, wu, wo)


def _inproj_body(x_ref, g_ref, w_ref, qkn_ref, z_ref, xbc_ref, q_ref, pg_ref, win_ref, mg_ref, sm_ref):
    hb = _rms_rows(x_ref[...], g_ref[...]).astype(BF16)
    o = 0
    z_ref[...] = _mm(hb, w_ref[:, o:o + D_INNER]); o += D_INNER
    xbc_ref[...] = _mm(hb, w_ref[:, o:o + CONV_DIM]); o += CONV_DIM
    q = _mm(hb, w_ref[:, o:o + D_ATTN]); o += D_ATTN
    q_ref[...] = _head_rms(q, qkn_ref[0:1, :] * (HEAD_DIM ** -0.5))
    kv = _mm(hb, w_ref[:, o:o + 6 * KV_W]); o += 6 * KV_W
    pg_ref[:, 0:2 * KV_W] = kv[:, 0:2 * KV_W]
    pg_ref[:, 2 * KV_W:3 * KV_W] = _head_rms(kv[:, 2 * KV_W:3 * KV_W], qkn_ref[2:3, :])
    pg_ref[:, 3 * KV_W:4 * KV_W] = kv[:, 3 * KV_W:4 * KV_W]
    win_ref[:, 0:KV_W] = _head_rms(kv[:, 4 * KV_W:5 * KV_W], qkn_ref[3:4, :])
    win_ref[:, KV_W:2 * KV_W] = kv[:, 5 * KV_W:6 * KV_W]
    mg_ref[...] = _mm(hb, w_ref[:, o:o + 2 * D_MODEL]); o += 2 * D_MODEL
    sm_ref[...] = _mm(hb, w_ref[:, o:o + SMALL_W])


def _pack_inproj_weight(w):
    c = np.cumsum([0, D_INNER, CONV_DIM, M_HEADS, D_ATTN, 6 * KV_W, 3 * Q_HEADS, 2 * D_MODEL])
    z, xbc, dt, q, kv, ng, mg = [w[:, c[i]:c[i + 1]] for i in range(7)]
    pad = jnp.zeros((D_MODEL, SMALL_W - M_HEADS - 3 * Q_HEADS), w.dtype)
    return jnp.concatenate([z, xbc, q, kv, mg, dt, ng, pad], axis=1).astype(BF16)


def _inproj(x2d, norm_g, w_packed, qk_norm, tm):
    n = x2d.shape[0]
    wtot = w_packed.shape[1]
    qkn2 = jnp.concatenate([qk_norm, qk_norm], axis=1)
    widths = (D_INNER, CONV_DIM, D_ATTN, 4 * KV_W, 2 * KV_W, 2 * D_MODEL, SMALL_W)
    return pl.pallas_call(
        _inproj_body,
        out_shape=[jax.ShapeDtypeStruct((n, w), F32) for w in widths],
        grid=(n // tm,),
        in_specs=[pl.BlockSpec((tm, D_MODEL), lambda i: (i, 0)),
                  _const_spec((1, D_MODEL)), _const_spec((D_MODEL, wtot)), _const_spec((4, LANES))],
        out_specs=[pl.BlockSpec((tm, w), lambda i: (i, 0)) for w in widths],
        compiler_params=_cparams(("arbitrary",)),
        name="inproj",
    )(x2d, norm_g.reshape(1, D_MODEL), w_packed, qkn2)


def _split_bf16(v):
    hi = v.astype(BF16)
    lo = (v - hi.astype(F32)).astype(BF16)
    return hi, lo


def _ssd_body(lv, xbc_ref, z_ref, sm_ref, conv0_ref, ssm0_ref, cw_ref, cb_ref, dtb_ref, alog_ref, dsk_ref,
              nw_ref, expand_ref, y_ref, convo_ref, ssmo_ref, xp_ref, st_ref):
    cl = SSD_CL
    c = pl.program_id(1)
    nchunk = pl.num_programs(1)
    hp = M_HEADS // M_GROUPS * M_HEADDIM

    @pl.when(c == 0)
    def _():
        xp_ref[...] = jnp.zeros(xp_ref.shape, F32)
        xp_ref[8 - (D_CONV - 1):8, :] = conv0_ref[0]
        st_ref[...] = ssm0_ref[0].reshape(M_HEADS * M_HEADDIM, D_STATE).T

    xp_ref[8:8 + lv, :] = xbc_ref[0]
    x_all = xp_ref[...]
    conv = cb_ref[...] + x_all[8:8 + cl] * cw_ref[D_CONV - 1:D_CONV, :]
    for k in range(D_CONV - 1):
        conv = conv + pltpu.roll(x_all, D_CONV - 1 - k, axis=0)[8:8 + cl] * cw_ref[k:k + 1, :]
    tail = xp_ref[8 + lv - (D_CONV - 1):8 + lv, :]
    convo_ref[0] = tail
    xp_ref[8 - (D_CONV - 1):8, :] = tail
    xc = _silu(conv)
    xs = xc[:, :D_INNER]

    sm = sm_ref[0]
    z = z_ref[0]
    if lv < cl:
        sm = jnp.concatenate([sm, jnp.zeros((cl - lv, SMALL_W), F32)], axis=0)
        z = jnp.concatenate([z, jnp.zeros((cl - lv, D_INNER), F32)], axis=0)
    xdt_in = sm + dtb_ref[...]
    dt = jnp.maximum(xdt_in, 0.0) + jnp.log1p(jnp.exp(-jnp.abs(xdt_in)))
    row = lax.broadcasted_iota(jnp.int32, (cl, 1), 0)
    dt = jnp.where(row < lv, dt, 0.0)
    a_neg = -jnp.exp(alog_ref[...])
    da = dt * a_neg
    ri = lax.broadcasted_iota(jnp.int32, (cl, cl), 0)
    ci = lax.broadcasted_iota(jnp.int32, (cl, cl), 1)
    tril = ri >= ci
    acs = _mm(tril.astype(F32), da, precision=HIGHEST)
    alast = acs[cl - 1:cl, :]
    eye = (lax.broadcasted_iota(jnp.int32, (LANES, LANES), 0)
           == lax.broadcasted_iota(jnp.int32, (LANES, LANES), 1)).astype(F32)
    acs_t = _nt(eye, acs, precision=HIGHEST)

    expand = expand_ref[...]
    stack =jnp.concatenate([dt, jnp.exp(acs), dt * jnp.exp(alast - acs)], axis=0)
    s_hi, s_lo = _split_bf16(stack)
    ex = _mm(s_hi, expand) + _mm(s_lo, expand)
    dt_e = ex[0:cl]
    es_e = ex[cl:2 * cl]
    w_e = ex[2 * cl:3 * cl]
    xdt = (xs * dt_e).astype(BF16)
    xw = (xs * w_e).astype(BF16)
    chunk_decay = es_e[cl - 1:cl, :]

    lane = lax.broadcasted_iota(jnp.int32, (1, LANES), 1)
    lo_mask = lane < M_HEADDIM
    eye_b = eye.astype(BF16)
    y_parts = []
    for g in range(M_GROUPS):
        bg = xc[:, D_INNER + g * D_STATE:D_INNER + (g + 1) * D_STATE].astype(BF16)
        cg = xc[:, D_INNER + (M_GROUPS + g) * D_STATE:D_INNER + (M_GROUPS + g + 1) * D_STATE].astype(BF16)
        cb = _nt(cg, bg)
        st_g = st_ref[:, g * hp:(g + 1) * hp]
        y_off = _mm(cg, st_g.astype(BF16)) * es_e[:, g * hp:(g + 1) * hp]
        bg_t = _nt(eye_b, bg).astype(BF16)
        st_ref[:, g * hp:(g + 1) * hp] = (st_g * chunk_decay[:, g * hp:(g + 1) * hp]
                                          + _mm(bg_t, xw[:, g * hp:(g + 1) * hp]))
        for jj in range(hp // LANES):
            col0 = g * hp + jj * LANES
            ms = []
            for j in (2 * (col0 // LANES), 2 * (col0 // LANES) + 1):
                diff = acs[:, j:j + 1] - acs_t[j:j + 1, :]
                dec = jnp.exp(jnp.where(tril, diff, -jnp.inf))
                ms.append((cb * dec).astype(BF16))
            x2 = xdt[:, col0:col0 + LANES]
            zero = jnp.zeros_like(x2)
            rhs = jnp.concatenate([jnp.where(lo_mask, x2, zero), jnp.where(lo_mask, zero, x2)], axis=0)
            y_parts.append(_mm(jnp.concatenate(ms, axis=1), rhs) + y_off[:, jj * LANES:(jj + 1) * LANES])
    y = jnp.concatenate(y_parts, axis=1)
    y = (y + dsk_ref[...] * xs) * _silu(z)
    outs = []
    for g in range(M_GROUPS):
        yg = y[:, g * hp:(g + 1) * hp]
        outs.append(yg * lax.rsqrt(jnp.mean(yg * yg, axis=-1, keepdims=True) + EPS))
    y = jnp.concatenate(outs, axis=1) * nw_ref[...]
    y_ref[0] = y[:lv].astype(BF16)

    @pl.when(c == nchunk - 1)
    def _():
        ssmo_ref[0] = st_ref[...].T.reshape(M_HEADS, M_HEADDIM, D_STATE)


def _ssd(xbc, z, small, conv0, ssm0, conv_w, conv_b, dt_bias, a_log, d_skip, ssm_norm):
    b, L = xbc.shape[:2]
    lv = SSD_CL if L % SSD_CL == 0 else L
    assert lv == SSD_CL or (L < SSD_CL and L % 8 == 0)
    nchunk = L // lv
    pad = lambda v: jnp.pad(v.astype(F32), (0, SMALL_W - M_HEADS)).reshape(1, SMALL_W)
    rep = lambda v: jnp.repeat(v.astype(F32), M_HEADDIM).reshape(1, D_INNER)
    return pl.pallas_call(
        functools.partial(_ssd_body, lv),
        out_shape=[jax.ShapeDtypeStruct((b, L, D_INNER), BF16),
                   jax.ShapeDtypeStruct((b, D_CONV - 1, CONV_DIM), F32),
                   jax.ShapeDtypeStruct((b, M_HEADS, M_HEADDIM, D_STATE), F32)],
        grid=(b, nchunk),
        in_specs=[pl.BlockSpec((1, lv, CONV_DIM), lambda i, c: (i, c, 0)),
                  pl.BlockSpec((1, lv, D_INNER), lambda i, c: (i, c, 0)),
                  pl.BlockSpec((1, lv, SMALL_W), lambda i, c: (i, c, 0)),
                  pl.BlockSpec((1, D_CONV - 1, CONV_DIM), lambda i, c: (i, 0, 0)),
                  pl.BlockSpec((1, M_HEADS, M_HEADDIM, D_STATE), lambda i, c: (i, 0, 0, 0)),
                  _const_spec((D_CONV, CONV_DIM)), _const_spec((1, CONV_DIM)),
                  _const_spec((1, SMALL_W)), _const_spec((1, SMALL_W)),
                  _const_spec((1, D_INNER)), _const_spec((1, D_INNER)), _const_spec((LANES, D_INNER))],
        out_specs=[pl.BlockSpec((1, lv, D_INNER), lambda i, c: (i, c, 0)),
                   pl.BlockSpec((1, D_CONV - 1, CONV_DIM), lambda i, c: (i, 0, 0)),
                   pl.BlockSpec((1, M_HEADS, M_HEADDIM, D_STATE), lambda i, c: (i, 0, 0, 0))],
        scratch_shapes=[pltpu.VMEM((8 + SSD_CL, CONV_DIM), F32), pltpu.VMEM((D_STATE, D_INNER), F32)],
        compiler_params=_cparams(("arbitrary", "arbitrary")),
        name="ssd",
    )(xbc, z, small, conv0, ssm0, conv_w, conv_b.reshape(1, CONV_DIM), pad(dt_bias), pad(a_log),
      rep(d_skip), ssm_norm.reshape(1, D_INNER),
      (np.arange(LANES)[:, None] == np.arange(D_INNER)[None, :] // M_HEADDIM).astype(BF16))


NO_LIMIT = 2 ** 30


def _bias_table_body(base, rc, cc, lo, hi, ncol_valid, tab_ref, o_ref):
    _, rb, cb = o_ref.shape
    r = pl.program_id(0) * rb + lax.broadcasted_iota(jnp.int32, (rb, cb), 0)
    c = pl.program_id(1) * cb + lax.broadcasted_iota(jnp.int32, (rb, cb), 1)
    d = base + rc * r + cc * c
    valid = (d >= lo) & (d <= hi) & (c < ncol_valid)
    max_exact = NUM_BUCKETS // 2
    dd = jnp.maximum(d, 0)
    df = jnp.maximum(dd, 1).astype(F32)
    large = max_exact + (jnp.log(df / max_exact) / math.log(MAX_DISTANCE / max_exact)
                         * (NUM_BUCKETS - max_exact)).astype(jnp.int32)
    large = jnp.minimum(large, NUM_BUCKETS - 1)
    bucket = jnp.where(dd < max_exact, dd, large)
    for hq in range(Q_HEADS):
        acc = jnp.zeros((rb, cb), F32)
        for bkt in range(NUM_BUCKETS):
            acc = jnp.where(bucket == bkt, tab_ref[bkt, hq], acc)
        o_ref[hq] = jnp.where(valid, acc, NEG)


def _bias_table(rel_bias, rows, cols, base, rc, cc, lo=0, hi=NO_LIMIT, ncol_valid=NO_LIMIT):
    rb = LANES if rows > LANES and rows % LANES == 0 else rows
    cb = 2048 if cols > 2048 and cols % 2048 == 0 else cols
    return pl.pallas_call(
        functools.partial(_bias_table_body, base, rc, cc, lo, hi, ncol_valid),
        out_shape=jax.ShapeDtypeStruct((Q_HEADS, rows, cols), F32),
        grid=(rows // rb, cols // cb),
        in_specs=[pl.BlockSpec(memory_space=pltpu.SMEM)],
        out_specs=pl.BlockSpec((Q_HEADS, rb, cb), lambda i, j: (0, i, j)),
        compiler_params=_cparams(("arbitrary", "arbitrary")),
        name="bias_table",
    )(rel_bias.astype(F32))


def _prompt_bias_tables(rel_bias, L):
    n_off = L // TQ + 1
    n_off_w = WINDOW // TQ + 3
    tab_s = _bias_table(rel_bias, n_off * TQ, TQ, -TQ, 1, -1)
    tab_w = _bias_table(rel_bias, n_off_w * TQ, TQ, -TQ, 1, -1, hi=WINDOW)
    n_c = L // CMP_BLOCK
    tab_ct = _bias_table(rel_bias, n_c, L, -(CMP_BLOCK - 1), -CMP_BLOCK, 1)
    return (tab_s.reshape(KV_HEADS, Q_PER_KV, n_off, TQ, TQ), tab_w.reshape(KV_HEADS, Q_PER_KV, n_off_w, TQ, TQ),
            tab_ct.reshape(KV_HEADS, Q_PER_KV, n_c, L))


def _compress_weights(cmp_pe, cmp_w1, cmp_w2):
    eye = jnp.eye(2, dtype=F32)
    w1 = jnp.einsum("hg,cjdf->cjhdgf", eye, cmp_w1).reshape(2, CMP_BLOCK // 2, 2 * LANES, 2 * CMP_HIDDEN)
    w2 = jnp.einsum("hg,cfd->chfgd", eye, cmp_w2).reshape(2, 2 * CMP_HIDDEN, LANES)
    pe = jnp.tile(cmp_pe, (1, 1, 2))
    return pe.astype(F32), w1.astype(BF16), w2.astype(BF16)


def _compress_rows(lo_ref, hi_ref, nblk, pe_ref, w1_ref, w2_ref, c):
    acc = jnp.zeros((2 * nblk, 2 * CMP_HIDDEN), F32)
    regroup = lambda ref: pltpu.einshape("(nj)l->(jn)l", ref[0:nblk * CMP_BLOCK, :], j=CMP_BLOCK)
    lo_all = regroup(lo_ref)
    hi_all = regroup(hi_ref)
    for jj in range(CMP_BLOCK // 2):
        cols = []
        for j in (2 * jj, 2 * jj + 1):
            take = slice(j * nblk, (j + 1) * nblk)
            pe_j = pe_ref[c, j:j + 1, :]
            cols.append(jnp.concatenate([lo_all[take] + pe_j, hi_all[take] + pe_j], axis=0))
        acc = acc + _mm(jnp.concatenate(cols, axis=1).astype(BF16), w1_ref[c, jj])
    out = _mm(_silu(acc).astype(BF16), w2_ref[c])
    return jnp.concatenate([out[:nblk], out[nblk:]], axis=1)


def _compress_prompt_body(k0_ref, k1_ref, v0_ref, v1_ref, pe_ref, w1_ref, w2_ref, kn_ref, kc_ref, vc_ref):
    nblk = kc_ref.shape[1]
    kc_ref[0] = _head_rms(_compress_rows(k0_ref.at[0], k1_ref.at[0], nblk, pe_ref, w1_ref, w2_ref, 0), kn_ref[...])
    vc_ref[0] = _compress_rows(v0_ref.at[0], v1_ref.at[0], nblk, pe_ref, w1_ref, w2_ref, 1)


def _compress_prompt(paged, pe, w1, w2, kn2):
    b, L = paged.shape[:2]
    n_c = L // CMP_BLOCK
    col = lambda c: pl.BlockSpec((1, L, LANES), lambda i: (i, 0, c))
    return pl.pallas_call(
        _compress_prompt_body,
        out_shape=[jax.ShapeDtypeStruct((b, n_c, KV_W), F32)] * 2,
        grid=(b,),
        in_specs=[col(0), col(1), col(2), col(3),
                  _const_spec(pe.shape), _const_spec(w1.shape), _const_spec(w2.shape), _const_spec((1, LANES))],
        out_specs=[pl.BlockSpec((1, n_c, KV_W), lambda i: (i, 0, 0))] * 2,
        compiler_params=_cparams(("arbitrary",)),
        name="compress_prompt",
    )(paged, paged, paged, paged, pe, w1, w2, kn2)


def _online_softmax_step(s, carry, v):
    m, l, acc = carry
    m_new = jnp.maximum(m, jnp.max(s, axis=-1, keepdims=True))
    alpha = jnp.exp(m - m_new)
    p = jnp.exp(s - m_new)
    return (m_new, alpha * l + jnp.sum(p, axis=-1, keepdims=True), alpha * acc + _mm(p.astype(BF16), v))


def _select_blocks(imp, t, n_blk):
    big = 1e30
    j = lax.broadcasted_iota(jnp.int32, (1, imp.shape[1]), 1)
    cur = t // SEL_BLOCK
    forced = (j == 0) | (j == cur) | (j == cur - 1)
    causal = j * SEL_BLOCK <= t
    score = jnp.where(forced, big, jnp.where(causal, imp, -big))
    rank = jnp.zeros(imp.shape, F32)
    for i in range(n_blk):
        ci = score[:, i:i + 1]
        ahead = (ci > score) | ((ci == score) & (i < j))
        rank = rank + jnp.where(ahead, 1.0, 0.0)
    return jnp.where(rank < SEL_TOPK, 1.0, 0.0)


def _nsa_prompt_body(n_sel, q_ref, ks_ref, vs_ref, kw_ref, vw_ref, kc_ref, vc_ref, sm_ref, tabs_ref, tabw_ref,
                     tabc_ref, o_ref, kaug_scr, vaug_scr, kwin_scr, vwin_scr, s_scr, p_scr):
    h = pl.program_id(1)
    qi = pl.program_id(2)
    rows = Q_PER_KV * TQ
    lane = lax.broadcasted_iota(jnp.int32, (1, LANES), 1)
    mine = (lane // HEAD_DIM) == (h % 2)

    @pl.when(qi == 0)
    def _():
        seq = kaug_scr.shape[0]
        key = lax.broadcasted_iota(jnp.int32, (seq, LANES), 0)
        blk = lax.broadcasted_iota(jnp.int32, (seq, LANES), 1)
        kaug_scr[:, 0:LANES] = ks_ref[0].astype(BF16)
        kaug_scr[:, LANES:2 * LANES] = jnp.where(blk == key // SEL_BLOCK, NEG, 0.0).astype(BF16)
        vaug_scr[...] = jnp.where(mine, vs_ref[0], 1.0).astype(BF16)
        kwin_scr[...] = kw_ref[0].astype(BF16)
        vwin_scr[...] = jnp.where(mine, vw_ref[0], 1.0).astype(BF16)

    qb = q_ref[0]
    q4 = jnp.concatenate([qb[:, g * HEAD_DIM:(g + 1) * HEAD_DIM] for g in range(Q_PER_KV)], axis=0)
    qpad = jnp.where(mine, jnp.concatenate([q4, q4], axis=1), 0.0).astype(BF16)

    sub = 64
    even = (h % 2) == 0
    pick = lambda a: jnp.where(even, a[:, :HEAD_DIM], a[:, HEAD_DIM:])
    other = lambda a: jnp.where(even, a[:, HEAD_DIM:], a[:, :HEAD_DIM])

    def attend(q_lhs, k_slab, v_slab, tab_ref, kt0, n_kt):
        width = n_kt * TQ
        n_half = 2
        hrows = rows // n_half
        for hf in range(n_half):
            s_scr[hf * hrows:(hf + 1) * hrows, 0:width] = _nt(q_lhs[hf * hrows:(hf + 1) * hrows], k_slab)
        accs = []
        for hf in range(n_half):
            for rb in range(hf * hrows // sub, (hf + 1) * hrows // sub):
                g, part = divmod(rb, TQ // sub)
                rr = slice(rb * sub, (rb + 1) * sub)
                tr = slice(part * sub, (part + 1) * sub)
                run = None
                for kt in range(n_kt):
                    cc = slice(kt * TQ, (kt + 1) * TQ)
                    s = s_scr[rr, cc] + tab_ref[0, g, jnp.maximum(qi - (kt0 + kt) + 1, 0), tr, :]
                    s_scr[rr, cc] = s
                    run = s if run is None else jnp.maximum(run, s)
                mb = jnp.broadcast_to(jnp.max(run, axis=-1, keepdims=True), (sub, TQ))
                for kt in range(n_kt):
                    cc = slice(kt * TQ, (kt + 1) * TQ)
                    p_scr[rr, cc] = jnp.exp(s_scr[rr, cc] - mb).astype(BF16)
            accs.append(_mm(p_scr[hf * hrows:(hf + 1) * hrows, 0:width], v_slab))
        acc = jnp.concatenate(accs, axis=0)
        return pick(acc) / jnp.maximum(other(acc), 1e-30)

    n_tiles = kaug_scr.shape[0] // TQ
    n_win = min(WINDOW // TQ + 1, n_tiles)
    w0 = jnp.maximum(qi - (n_win - 1), 0)
    w_rows = pl.ds(pl.multiple_of(w0 * TQ, TQ), n_win * TQ)
    o_w = attend(qpad, kwin_scr[w_rows, :], vwin_scr[w_rows, :], tabw_ref, w0, n_win)

    n_c = kc_ref.shape[1]
    bias_ct = jnp.concatenate([tabc_ref[0, g] for g in range(Q_PER_KV)], axis=1)
    s_ct = _nt(kc_ref[0].astype(BF16), qpad) + bias_ct
    e_ct = jnp.where(bias_ct > 0.5 * NEG, jnp.exp(s_ct - jnp.max(s_ct, axis=0, keepdims=True)), 0.0)
    p_ct = e_ct / jnp.maximum(jnp.sum(e_ct, axis=0, keepdims=True), 1e-30)
    o_c = lax.dot_general(p_ct.astype(BF16), vc_ref[0].astype(BF16), (((0,), (0,)), ((), ())),
                          preferred_element_type=F32)

    imp_c = sum(p_ct[:, g * TQ:(g + 1) * TQ] for g in range(Q_PER_KV))
    nr = -(-n_sel // 8) * 8
    pair = (lax.broadcasted_iota(jnp.int32, (nr, n_c), 0)
            == lax.broadcasted_iota(jnp.int32, (nr, n_c), 1) // (SEL_BLOCK // CMP_BLOCK)).astype(BF16)
    c_hi = imp_c.astype(BF16)
    c_mid = (imp_c - c_hi.astype(F32)).astype(BF16)
    c_lo = (imp_c - c_hi.astype(F32) - c_mid.astype(F32)).astype(BF16)
    imp = _mm(pair, c_hi) + _mm(pair, c_mid) + _mm(pair, c_lo)
    t = qi * TQ + lax.broadcasted_iota(jnp.int32, (1, TQ), 1)
    j = lax.broadcasted_iota(jnp.int32, (nr, 1), 0)
    cur = t // SEL_BLOCK
    forced = (j == 0) | (j == cur) | (j == cur - 1)
    score = jnp.where(forced, 1e30, jnp.where(j * SEL_BLOCK <= t, imp, -1e30))
    rank = jnp.zeros((nr, TQ), F32)
    for i in range(n_sel):
        ri = score[i:i + 1, :]
        rank = rank + jnp.where((ri > score) | ((ri == score) & (i < j)), 1.0, 0.0)
    drop_t = jnp.where(rank < SEL_TOPK, 0.0, 1.0)
    drop_t = jnp.concatenate([drop_t, jnp.ones((LANES - nr, TQ), F32)], axis=0)
    drop = drop_t.T.astype(BF16)
    qaug = jnp.concatenate([qpad, jnp.concatenate([drop] * Q_PER_KV, axis=0)], axis=1)

    ext_step = 2
    extents = [min((e + 1) * ext_step, n_tiles) for e in range(-(-n_tiles // ext_step))]
    o_s = lax.switch(qi // ext_step,
                     [functools.partial(lambda n: attend(qaug, kaug_scr[0:n * TQ, :], vaug_scr[0:n * TQ, :],
                                                         tabs_ref, 0, n), n) for n in extents])
    o_c = pick(o_c)

    gates = jax.nn.sigmoid(pltpu.roll(sm_ref[0], SMALL_W - NSA_G_OFF - h * (3 * Q_PER_KV), axis=1))
    outs = []
    for g in range(Q_PER_KV):
        rs = slice(g * TQ, (g + 1) * TQ)
        outs.append(gates[:, 3 * g:3 * g + 1] * o_c[rs] + gates[:, 3 * g + 1:3 * g + 2] * o_s[rs]
                    + gates[:, 3 * g + 2:3 * g + 3] * o_w[rs])
    o_ref[0] = jnp.concatenate(outs, axis=1).astype(BF16)


def _nsa_prompt(qn, paged, win, kc, vc, small, tab_s, tab_w, tab_c):
    b, L = qn.shape[:2]
    assert L % TKC == 0
    n_c = L // CMP_BLOCK
    rows = Q_PER_KV * TQ
    seq = lambda col: pl.BlockSpec((1, L, LANES), lambda i, h, q: (i, 0, col(h)))
    cmp = pl.BlockSpec((1, n_c, LANES), lambda i, h, q: (i, 0, h // 2))
    return pl.pallas_call(
        functools.partial(_nsa_prompt_body, L // SEL_BLOCK),
        out_shape=jax.ShapeDtypeStruct((b, L, D_ATTN), BF16),
        grid=(b, KV_HEADS, L // TQ),
        in_specs=[pl.BlockSpec((1, TQ, KV_W), lambda i, h, q: (i, q, h)),
                  seq(lambda h: 4 + h // 2), seq(lambda h: 6 + h // 2),
                  seq(lambda h: h // 2), seq(lambda h: 2 + h // 2),
                  cmp, cmp,
                  pl.BlockSpec((1, TQ, SMALL_W), lambda i, h, q: (i, q, 0)),
                  pl.BlockSpec((1,) + tab_s.shape[1:], lambda i, h, q: (h, 0, 0, 0, 0)),
                  pl.BlockSpec((1,) + tab_w.shape[1:], lambda i, h, q: (h, 0, 0, 0, 0)),
                  pl.BlockSpec((1, Q_PER_KV, n_c, TQ), lambda i, h, q: (h, 0, 0, q))],
        out_specs=pl.BlockSpec((1, TQ, KV_W), lambda i, h, q: (i, q, h)),
        scratch_shapes=[pltpu.VMEM((L, 2 * LANES), BF16), pltpu.VMEM((L, LANES), BF16),
                        pltpu.VMEM((L, LANES), BF16), pltpu.VMEM((L, LANES), BF16),
                        pltpu.VMEM((rows, L), F32), pltpu.VMEM((rows, L), BF16)],
        compiler_params=_cparams(("arbitrary", "arbitrary", "arbitrary")),
        name="nsa_prompt",
    )(qn, paged, paged, win, win, kc, vc, small, tab_s, tab_w, tab_c)


def _merge_body(x_ref, y_ref, o_ref, mg_ref, wa_ref, wb_ref, wo_ref, out_ref):
    ya = _mm(y_ref[...], wa_ref[...])
    yb = _mm(o_ref[...], wb_ref[...])
    gm = jax.nn.sigmoid(mg_ref[...])
    mix = (gm[:, :D_MODEL] * ya + gm[:, D_MODEL:] * yb).astype(BF16)
    out_ref[...] = x_ref[...] + _mm(mix, wo_ref[...])


def _merge(x2d, y2d, o2d, mg, wa, wb, wo, tm):
    n = x2d.shape[0]
    row = lambda w: pl.BlockSpec((tm, w), lambda i: (i, 0))
    return pl.pallas_call(
        _merge_body,
        out_shape=jax.ShapeDtypeStruct((n, D_MODEL), F32),
        grid=(n // tm,),
        in_specs=[row(D_MODEL), row(D_INNER), row(D_ATTN), row(2 * D_MODEL),
                  _const_spec((D_INNER, D_MODEL)), _const_spec((D_ATTN, D_MODEL)), _const_spec((D_MODEL, D_MODEL))],
        out_specs=row(D_MODEL),
        compiler_params=_cparams(("arbitrary",)),
        name="merge",
    )(x2d, y2d, o2d, mg, wa.astype(BF16), wb.astype(BF16), wo.astype(BF16))


def _layer_prompt(x, lw, f_bias, tm):
    (ffn1_norm, ffn1_w_in, ffn1_w_out, mix_norm, w_packed, conv_w, conv_b, dt_bias, a_log, d_skip, ssm_norm,
     qk_norm, cmp_prep, w_branch_ssm, w_branch_attn, w_out, ffn2_norm, ffn2_w_in, ffn2_w_out) = lw
    b, L = x.shape[:2]
    n = b * L
    tm_wide = 2 * tm if n % (2 * tm) == 0 else tm
    x1 = _ffn(x.reshape(n, D_MODEL), ffn1_norm, ffn1_w_in, ffn1_w_out, tm_wide)
    z, xbc, qn, paged, win, mg, small = _inproj(x1, mix_norm, w_packed, qk_norm, tm)
    r3 = lambda a: a.reshape(b, L, a.shape[-1])
    conv0 = jnp.zeros((b, D_CONV - 1, CONV_DIM), F32)
    ssm0 = jnp.zeros((b, M_HEADS, M_HEADDIM, D_STATE), F32)
    y, new_conv, new_ssm = _ssd(r3(xbc), r3(z), r3(small), conv0, ssm0, conv_w, conv_b, dt_bias, a_log, d_skip,
                                ssm_norm)
    pe, w1, w2 = cmp_prep
    kn2 = jnp.concatenate([qk_norm[1], qk_norm[1]]).reshape(1, LANES)
    kc, vc = _compress_prompt(r3(paged), pe, w1, w2, kn2)
    tab_s, tab_w, tab_c = _prompt_bias_tables(f_bias, L)
    o = _nsa_prompt(r3(qn), r3(paged), r3(win), kc, vc, r3(small), tab_s, tab_w, tab_c)
    x2 = _merge(x1, y.reshape(n, D_INNER), o.reshape(n, D_ATTN), mg, w_branch_ssm, w_branch_attn, w_out, tm_wide)
    x3 = _ffn(x2, ffn2_norm, ffn2_w_in, ffn2_w_out, tm_wide)
    n_keep = min(WINDOW, L)
    new_kv = paged.reshape(b, L, 4, KV_HEADS, HEAD_DIM)
    new_win = r3(win)[:, L - n_keep:].reshape(b, n_keep, 2, KV_HEADS, HEAD_DIM)
    return x3.reshape(b, L, D_MODEL), new_kv, new_win, new_ssm, new_conv


PAGES_PER_STEP = 32


def _sample_qpad(qb):
    ls = qb.shape[0]
    blocks = []
    for hq in range(Q_HEADS):
        h = hq // Q_PER_KV
        parts = []
        if h > 0:
            parts.append(jnp.zeros((ls, h * HEAD_DIM), F32))
        parts.append(qb[:, hq * HEAD_DIM:(hq + 1) * HEAD_DIM])
        if h < KV_HEADS - 1:
            parts.append(jnp.zeros((ls, (KV_HEADS - 1 - h) * HEAD_DIM), F32))
        blocks.append(jnp.concatenate(parts, axis=1))
    return jnp.concatenate(blocks, axis=0).astype(BF16)


def _sample_head_piece(o, hq, ls):
    h = hq // Q_PER_KV
    return o[hq * ls:(hq + 1) * ls, h * HEAD_DIM:(h + 1) * HEAD_DIM]


def _chunk_copies(pt_ref, cache_ref, buf, sem, cols, width, bb, ii, sl, p):
    page = pt_ref[bb, ii * PAGES_PER_STEP + p]
    row0 = pl.multiple_of(p * PAGE_SIZE, PAGE_SIZE)
    return [pltpu.make_async_copy(cache_ref.at[page, :, pl.ds(c0, width)],
                                  buf.at[sl, k, pl.ds(row0, PAGE_SIZE), :], sem.at[sl])
            for k, c0 in enumerate(cols)]


def _stream_chunks(pt_ref, cache_ref, buf, sem, cols, width):
    b = pl.program_id(0)
    i = pl.program_id(1)
    nch = pl.num_programs(1)
    step = b * nch + i
    slot = step % 2

    def run(bb, ii, sl, start):
        def body(p, carry):
            for cp in _chunk_copies(pt_ref, cache_ref, buf, sem, cols, width, bb, ii, sl, p):
                cp.start() if start else cp.wait()
            return carry
        lax.fori_loop(0, PAGES_PER_STEP, body, 0)

    @pl.when(step == 0)
    def _():
        run(b, i, slot, True)

    nxt = step + 1

    @pl.when(nxt < pl.num_programs(0) * nch)
    def _():
        run(nxt // nch, nxt % nch, nxt % 2, True)

    run(b, i, slot, False)
    return slot


def _cmp_sample_body(past_len, pt_ref, cache_ref, q_ref, biasc_ref, pe_ref, w1_ref, w2_ref, kn_ref,
                     oc_ref, selm_ref, buf, sem, kc_s, vc_s):
    i = pl.program_id(1)
    slot = _stream_chunks(pt_ref, cache_ref, buf, sem, [k * LANES for k in range(4)], LANES)
    nblk = PAGES_PER_STEP * PAGE_SIZE // CMP_BLOCK
    rows0 = pl.multiple_of(i * nblk, nblk)
    kc_s[pl.ds(rows0, nblk), :] = _head_rms(
        _compress_rows(buf.at[slot, 0], buf.at[slot, 1], nblk, pe_ref, w1_ref, w2_ref, 0), kn_ref[...])
    vc_s[pl.ds(rows0, nblk), :] = _compress_rows(buf.at[slot, 2], buf.at[slot, 3], nblk, pe_ref, w1_ref, w2_ref, 1)

    @pl.when(i == pl.num_programs(1) - 1)
    def _():
        ls = q_ref.shape[1]
        n_c = kc_s.shape[0]
        s = _nt(_sample_qpad(q_ref[0]), kc_s[...].astype(BF16)) + biasc_ref[...]
        e = jnp.exp(s - jnp.max(s, axis=-1, keepdims=True))
        p = e / jnp.maximum(jnp.sum(e, axis=-1, keepdims=True), 1e-30)
        oc = _mm(p.astype(BF16), vc_s[...].astype(BF16))
        oc_ref[0] = jnp.concatenate([_sample_head_piece(oc, hq, ls) for hq in range(Q_HEADS)], axis=1)
        imp_c = jnp.sum(p.reshape(KV_HEADS, Q_PER_KV, ls, n_c), axis=1).reshape(KV_HEADS * ls, n_c)
        w = selm_ref.shape[2]
        pair = (lax.broadcasted_iota(jnp.int32, (n_c, w), 0) // (SEL_BLOCK // CMP_BLOCK)
                == lax.broadcasted_iota(jnp.int32, (n_c, w), 1)).astype(F32)
        imp = _mm(imp_c, pair, precision=HIGHEST)
        t = past_len + lax.broadcasted_iota(jnp.int32, (KV_HEADS * ls, 1), 0) % ls
        selm_ref[0] = _select_blocks(imp, t, past_len // SEL_BLOCK + 1)


def _nsa_sample_body(pt_ref, cache_ref, q_ref, selm_ref, oc_ref, sm_ref, pnew_ref, cwin_ref, wnew_ref,
                     biass_ref, biasw_ref, biasn_ref, o_ref, buf, sem, m_s, l_s, a_s):
    i = pl.program_id(1)
    slot = _stream_chunks(pt_ref, cache_ref, buf, sem, [2 * KV_W, 3 * KV_W], KV_W)
    ls = q_ref.shape[1]
    rows = Q_HEADS * ls
    tk = PAGES_PER_STEP * PAGE_SIZE
    qpad = _sample_qpad(q_ref[0])

    @pl.when(i == 0)
    def _():
        m_s[...] = jnp.full(m_s.shape, NEG, F32)
        l_s[...] = jnp.zeros(l_s.shape, F32)
        a_s[...] = jnp.zeros(a_s.shape, F32)

    sel = selm_ref[0]
    w = sel.shape[1]
    sel = jnp.broadcast_to(sel.reshape(KV_HEADS, 1, ls, w), (KV_HEADS, Q_PER_KV, ls, w)).reshape(rows, w)
    blk = lax.broadcasted_iota(jnp.int32, (w, tk), 0)
    key = lax.broadcasted_iota(jnp.int32, (w, tk), 1)
    spread = (blk == i * (tk // SEL_BLOCK) + key // SEL_BLOCK).astype(BF16)
    keep = _mm(sel.astype(BF16), spread)
    s = _nt(qpad, buf[slot, 0].astype(BF16)) + biass_ref[...] + jnp.where(keep > 0.5, 0.0, NEG)
    m, l, a = _online_softmax_step(s, (m_s[...], l_s[...], a_s[...]), buf[slot, 1].astype(BF16))
    m_s[...] = m
    l_s[...] = l
    a_s[...] = a

    @pl.when(i == pl.num_programs(1) - 1)
    def _():
        def pad_rows(x):
            return jnp.concatenate([x, jnp.zeros((LANES - ls, x.shape[1]), F32)], axis=0).astype(BF16)

        pn = pnew_ref[0]
        wn = wnew_ref[0]
        s_n = _nt(qpad, pad_rows(pn[:, 2 * KV_W:3 * KV_W])) + biasn_ref[...]
        _, l2, a2 = _online_softmax_step(s_n, (m, l, a), pad_rows(pn[:, 3 * KV_W:4 * KV_W]))
        o_s = a2 / jnp.maximum(l2, 1e-30)
        cw = cwin_ref[0]
        init = (jnp.full((rows, 1), NEG, F32), jnp.zeros((rows, 1), F32), jnp.zeros((rows, KV_W), F32))
        c1 = _online_softmax_step(_nt(qpad, cw[:, 0:KV_W].astype(BF16)) + biasw_ref[...], init,
                                  cw[:, KV_W:2 * KV_W].astype(BF16))
        s_wn = _nt(qpad, pad_rows(wn[:, 0:KV_W])) + biasn_ref[...]
        _, l3, a3 = _online_softmax_step(s_wn, c1, pad_rows(wn[:, KV_W:2 * KV_W]))
        o_w = a3 / jnp.maximum(l3, 1e-30)
        gates = jax.nn.sigmoid(sm_ref[0])
        oc = oc_ref[0]
        outs = []
        for hq in range(Q_HEADS):
            g0 = NSA_G_OFF + 3 * hq
            outs.append(gates[:, g0:g0 + 1] * oc[:, hq * HEAD_DIM:(hq + 1) * HEAD_DIM]
                        + gates[:, g0 + 1:g0 + 2] * _sample_head_piece(o_s, hq, ls)
                        + gates[:, g0 + 2:g0 + 3] * _sample_head_piece(o_w, hq, ls))
        o_ref[0] = jnp.concatenate(outs, axis=1).astype(BF16)


def _sample_bias_tables(rel_bias, past_len, ls, n_buf):
    rows2d = lambda t: t.reshape(Q_HEADS * ls, t.shape[2])
    bias_s = _bias_table(rel_bias, ls, past_len, past_len, 1, -1)
    n_c = past_len // CMP_BLOCK
    bias_c = _bias_table(rel_bias, ls, n_c, past_len - (CMP_BLOCK - 1), 1, -CMP_BLOCK)
    bias_w = _bias_table(rel_bias, ls, n_buf, n_buf, 1, -1, hi=WINDOW)
    bias_n = _bias_table(rel_bias, ls, LANES, 0, 1, -1, ncol_valid=ls)
    return rows2d(bias_s), rows2d(bias_c), rows2d(bias_w), rows2d(bias_n)


def _nsa_sample(qn, paged, win, small, cache_pages, page_table, cache_win, rel_bias, cmp_prep, kn2):
    b, ls = qn.shape[:2]
    n_pages = page_table.shape[1]
    past_len = n_pages * PAGE_SIZE
    n_buf = cache_win.shape[1]
    assert n_pages % PAGES_PER_STEP == 0 and ls <= SEL_BLOCK and ls % 8 == 0 and n_buf == min(WINDOW, past_len)
    nch = n_pages // PAGES_PER_STEP
    tk = PAGES_PER_STEP * PAGE_SIZE
    n_c = past_len // CMP_BLOCK
    n_sel = past_len // SEL_BLOCK + 1
    w_sel = -(-n_sel // LANES) * LANES
    rows = Q_HEADS * ls
    pe, w1, w2 = cmp_prep
    bias_s, bias_c, bias_w, bias_n = _sample_bias_tables(rel_bias, past_len, ls, n_buf)
    tok = lambda wd: pl.BlockSpec((1, ls, wd), lambda i, c, pt: (i, 0, 0))
    const = lambda shape: pl.BlockSpec(shape, lambda i, c, pt: (0,) * len(shape), pipeline_mode=pl.Buffered(1))
    o_c, selm = pl.pallas_call(
        functools.partial(_cmp_sample_body, past_len),
        out_shape=[jax.ShapeDtypeStruct((b, ls, D_ATTN), F32),
                   jax.ShapeDtypeStruct((b, KV_HEADS * ls, w_sel), F32)],
        grid_spec=pltpu.PrefetchScalarGridSpec(
            num_scalar_prefetch=1, grid=(b, nch),
            in_specs=[pl.BlockSpec(memory_space=pl.ANY), tok(D_ATTN), const((rows, n_c)),
                      const(pe.shape), const(w1.shape), const(w2.shape), const((1, LANES))],
            out_specs=[tok(D_ATTN), pl.BlockSpec((1, KV_HEADS * ls, w_sel), lambda i, c, pt: (i, 0, 0))],
            scratch_shapes=[pltpu.VMEM((2, 4, tk, LANES), F32), pltpu.SemaphoreType.DMA((2,)),
                            pltpu.VMEM((n_c, KV_W), F32), pltpu.VMEM((n_c, KV_W), F32)]),
        compiler_params=_cparams(("arbitrary", "arbitrary")),
        name="nsa_sample_compress",
    )(page_table, cache_pages, qn, bias_c, pe, w1, w2, kn2)
    return pl.pallas_call(
        _nsa_sample_body,
        out_shape=jax.ShapeDtypeStruct((b, ls, D_ATTN), BF16),
        grid_spec=pltpu.PrefetchScalarGridSpec(
            num_scalar_prefetch=1, grid=(b, nch),
            in_specs=[pl.BlockSpec(memory_space=pl.ANY), tok(D_ATTN),
                      pl.BlockSpec((1, KV_HEADS * ls, w_sel), lambda i, c, pt: (i, 0, 0)),
                      tok(D_ATTN), tok(SMALL_W), tok(4 * KV_W),
                      pl.BlockSpec((1, n_buf, 2 * KV_W), lambda i, c, pt: (i, 0, 0)), tok(2 * KV_W),
                      pl.BlockSpec((rows, tk), lambda i, c, pt: (0, c)),
                      const((rows, n_buf)), const((rows, LANES))],
            out_specs=tok(D_ATTN),
            scratch_shapes=[pltpu.VMEM((2, 2, tk, KV_W), F32), pltpu.SemaphoreType.DMA((2,)),
                            pltpu.VMEM((rows, 1), F32), pltpu.VMEM((rows, 1), F32), pltpu.VMEM((rows, KV_W), F32)]),
        compiler_params=_cparams(("arbitrary", "arbitrary")),
        name="nsa_sample",
    )(page_table, cache_pages, qn, selm, o_c, small, paged, cache_win, win, bias_s, bias_w, bias_n)


def _layer_sample(x, ssm0, conv0, cache_kv_l, page_table, cache_win_l, lw, f_bias):
    (ffn1_norm, ffn1_w_in, ffn1_w_out, mix_norm, w_packed, conv_w, conv_b, dt_bias, a_log, d_skip, ssm_norm,
     qk_norm, cmp_prep, w_branch_ssm, w_branch_attn, w_out, ffn2_norm, ffn2_w_in, ffn2_w_out) = lw
    b, ls = x.shape[:2]
    n = b * ls
    tm = n
    x1 = _ffn(x.reshape(n, D_MODEL), ffn1_norm, ffn1_w_in, ffn1_w_out, tm)
    z, xbc, qn, paged, win, mg, small = _inproj(x1, mix_norm, w_packed, qk_norm, tm)
    r3 = lambda a: a.reshape(b, ls, a.shape[-1])
    y, new_conv, new_ssm = _ssd(r3(xbc), r3(z), r3(small), conv0, ssm0, conv_w, conv_b, dt_bias, a_log, d_skip,
                                ssm_norm)
    kn2 = jnp.concatenate([qk_norm[1], qk_norm[1]]).reshape(1, LANES)
    n_pool = cache_kv_l.shape[0]
    cache_pages = cache_kv_l.reshape(n_pool, PAGE_SIZE, 4 * KV_W)
    n_buf = cache_win_l.shape[1]
    cwin = cache_win_l.reshape(b, n_buf, 2 * KV_W)
    o = _nsa_sample(r3(qn), r3(paged), r3(win), r3(small), cache_pages, page_table, cwin, f_bias, cmp_prep, kn2)
    x2 = _merge(x1, y.reshape(n, D_INNER), o.reshape(n, D_ATTN), mg, w_branch_ssm, w_branch_attn, w_out, tm)
    x3 = _ffn(x2, ffn2_norm, ffn2_w_in, ffn2_w_out, tm)
    n_keep = min(WINDOW, n_buf + ls)
    new_kv = paged.reshape(b, ls, 4, KV_HEADS, HEAD_DIM)
    win_all = jnp.concatenate([cwin, r3(win)], axis=1)
    new_win = win_all[:, n_buf + ls - n_keep:].reshape(b, n_keep, 2, KV_HEADS, HEAD_DIM)
    return x3.reshape(b, ls, D_MODEL), new_kv, new_win, new_ssm, new_conv


def _layer_weights(l, ffn1_norm, ffn1_w_in, ffn1_w_out, mix_norm, w_in_proj, conv_w, conv_b, dt_bias, a_log,
                   d_skip, ssm_norm, qk_norm, cmp_pe, cmp_w1, cmp_w2, w_branch_ssm, w_branch_attn, w_out,
                   ffn2_norm, ffn2_w_in, ffn2_w_out):
    return (ffn1_norm[l], ffn1_w_in[l], ffn1_w_out[l], mix_norm[l], _pack_inproj_weight(w_in_proj[l]), conv_w[l],
            conv_b[l], dt_bias[l], a_log[l], d_skip[l], ssm_norm[l], qk_norm[l],
            _compress_weights(cmp_pe[l], cmp_w1[l], cmp_w2[l]), w_branch_ssm[l], w_branch_attn[l], w_out[l],
            ffn2_norm[l], ffn2_w_in[l], ffn2_w_out[l])


def kernel(x_prompt, x_sample, cache_kv, cache_win, state_ssm, state_conv, page_table, rel_bias, ffn1_norm, ffn1_w_in, ffn1_w_out, mix_norm, w_in_proj, conv_w, conv_b, dt_bias, a_log, d_skip, ssm_norm, qk_norm, cmp_pe, cmp_w1, cmp_w2, w_branch_ssm, w_branch_attn, w_out, ffn2_norm, ffn2_w_in, ffn2_w_out):
    depth = ffn1_norm.shape[0]
    f_bias = rel_bias
    xp, xs = x_prompt, x_sample
    outs_p, outs_s = [], []
    for l in range(depth):
        lw = _layer_weights(l, ffn1_norm, ffn1_w_in, ffn1_w_out, mix_norm, w_in_proj, conv_w, conv_b, dt_bias,
                            a_log, d_skip, ssm_norm, qk_norm, cmp_pe, cmp_w1, cmp_w2, w_branch_ssm,
                            w_branch_attn, w_out, ffn2_norm, ffn2_w_in, ffn2_w_out)
        xp, *rp = _layer_prompt(xp, lw, f_bias, 256)
        outs_p.append(rp)
        xs, *rs = _layer_sample(xs, state_ssm[l], state_conv[l], cache_kv[l], page_table, cache_win[l], lw, f_bias)
        outs_s.append(rs)
    stack = lambda outs, k: jnp.stack([o[k] for o in outs])
    return ((xp, xs) + tuple(stack(outs_p, k) for k in range(4)) + tuple(stack(outs_s, k) for k in range(4)))
```

```python
import functools
import math

import jax
import jax.numpy as jnp
import numpy as np
from jax import lax
from jax.experimental import pallas as pl
from jax.experimental.pallas import tpu as pltpu

F32 = jnp.float32
BF16 = jnp.bfloat16
HIGHEST = lax.Precision.HIGHEST

D_MODEL = 1024
D_FF = 2816
D_INNER = 2048
M_HEADDIM = 64
M_HEADS = 32
M_GROUPS = 4
D_STATE = 128
D_CONV = 4
CONV_DIM = D_INNER + 2 * M_GROUPS * D_STATE
HEAD_DIM = 64
Q_HEADS = 16
KV_HEADS = 4
Q_PER_KV = 4
D_ATTN = Q_HEADS * HEAD_DIM
KV_W = KV_HEADS * HEAD_DIM
CMP_BLOCK = 32
CMP_HIDDEN = 128
SEL_BLOCK = 64
SEL_TOPK = 16
WINDOW = 512
NUM_BUCKETS = 32
MAX_DISTANCE = 2048
PAGE_SIZE = 128
EPS = 1e-6
NEG = -1e30

LANES = 128
SSD_CL = 128
TQ = 128
TKC = 256
VMEM_LIMIT = 56 * 1024 * 1024
SMALL_W = 128
NSA_G_OFF = M_HEADS


def _cparams(sem):
    return pltpu.CompilerParams(dimension_semantics=sem, vmem_limit_bytes=VMEM_LIMIT)


def _const_spec(shape):
    nd = len(shape)
    return pl.BlockSpec(shape, lambda *a: (0,) * nd, pipeline_mode=pl.Buffered(1))


def _nt(a, b, precision=None):
    return lax.dot_general(a, b, (((1,), (1,)), ((), ())), preferred_element_type=F32, precision=precision)


def _mm(a, b, precision=None):
    return jnp.dot(a, b, preferred_element_type=F32, precision=precision)


def _rms_rows(x, g):
    return x * lax.rsqrt(jnp.mean(x * x, axis=-1, keepdims=True) + EPS) * g


def _silu(x):
    return x * jax.nn.sigmoid(x)


def _head_rms(a, gain2):
    lane = lax.broadcasted_iota(jnp.int32, (1, LANES), 1)
    lo_mask = lane < HEAD_DIM
    outs = []
    for c in range(a.shape[1] // LANES):
        blk = a[:, c * LANES:(c + 1) * LANES]
        sq = blk * blk
        lo = jnp.sum(jnp.where(lo_mask, sq, 0.0), axis=-1, keepdims=True)
        hi = jnp.sum(jnp.where(lo_mask, 0.0, sq), axis=-1, keepdims=True)
        r = jnp.where(lo_mask, lax.rsqrt(lo / HEAD_DIM + EPS), lax.rsqrt(hi / HEAD_DIM + EPS))
        outs.append(blk * r * gain2)
    return outs[0] if len(outs) == 1 else jnp.concatenate(outs, axis=1)


def _ffn_body(x_ref, g_ref, wg_ref, wu_ref, wo_ref, o_ref):
    x = x_ref[...]
    hb = _rms_rows(x, g_ref[...]).astype(BF16)
    gate = _mm(hb, wg_ref[...])
    up = _mm(hb, wu_ref[...])
    act = (_silu(gate) * up).astype(BF16)
    o_ref[...] = x + 0.5 * _mm(act, wo_ref[...])


def _ffn(x2d, norm_g, w_in, w_out, tm):
    n = x2d.shape[0]
    wg = w_in[:, :D_FF].astype(BF16)
    wu = w_in[:, D_FF:].astype(BF16)
    wo = w_out.astype(BF16)
    return pl.pallas_call(
        _ffn_body,
        out_shape=jax.ShapeDtypeStruct((n, D_MODEL), F32),
        grid=(n // tm,),
        in_specs=[pl.BlockSpec((tm, D_MODEL), lambda i: (i, 0)),
                  _const_spec((1, D_MODEL)),
                  _const_spec((D_MODEL, D_FF)), _const_spec((D_MODEL, D_FF)), _const_spec((D_FF, D_MODEL))],
        out_specs=pl.BlockSpec((tm, D_MODEL), lambda i: (i, 0)),
        compiler_params=_cparams(("arbitrary",)),
        name="ffn",
    )(x2d, norm_g.reshape(1, D_MODEL), wg, wu, wo)


def _inproj_body(x_ref, g_ref, w_ref, qkn_ref, z_ref, xbc_ref, q_ref, pg_ref, win_ref, mg_ref, sm_ref):
    hb = _rms_rows(x_ref[...], g_ref[...]).astype(BF16)
    o = 0
    z_ref[...] = _mm(hb, w_ref[:, o:o + D_INNER]); o += D_INNER
    xbc_ref[...] = _mm(hb, w_ref[:, o:o + CONV_DIM]); o += CONV_DIM
    q = _mm(hb, w_ref[:, o:o + D_ATTN]); o += D_ATTN
    q_ref[...] = _head_rms(q, qkn_ref[0:1, :] * (HEAD_DIM ** -0.5))
    kv = _mm(hb, w_ref[:, o:o + 6 * KV_W]); o += 6 * KV_W
    pg_ref[:, 0:2 * KV_W] = kv[:, 0:2 * KV_W]
    pg_ref[:, 2 * KV_W:3 * KV_W] = _head_rms(kv[:, 2 * KV_W:3 * KV_W], qkn_ref[2:3, :])
    pg_ref[:, 3 * KV_W:4 * KV_W] = kv[:, 3 * KV_W:4 * KV_W]
    win_ref[:, 0:KV_W] = _head_rms(kv[:, 4 * KV_W:5 * KV_W], qkn_ref[3:4, :])
    win_ref[:, KV_W:2 * KV_W] = kv[:, 5 * KV_W:6 * KV_W]
    mg_ref[...] = _mm(hb, w_ref[:, o:o + 2 * D_MODEL]); o += 2 * D_MODEL
    sm_ref[...] = _mm(hb, w_ref[:, o:o + SMALL_W])


def _pack_inproj_weight(w):
    c = np.cumsum([0, D_INNER, CONV_DIM, M_HEADS, D_ATTN, 6 * KV_W, 3 * Q_HEADS, 2 * D_MODEL])
    z, xbc, dt, q, kv, ng, mg = [w[:, c[i]:c[i + 1]] for i in range(7)]
    pad = jnp.zeros((D_MODEL, SMALL_W - M_HEADS - 3 * Q_HEADS), w.dtype)
    return jnp.concatenate([z, xbc, q, kv, mg, dt, ng, pad], axis=1).astype(BF16)


def _inproj(x2d, norm_g, w_packed, qk_norm, tm):
    n = x2d.shape[0]
    wtot = w_packed.shape[1]
    qkn2 = jnp.concatenate([qk_norm, qk_norm], axis=1)
    widths = (D_INNER, CONV_DIM, D_ATTN, 4 * KV_W, 2 * KV_W, 2 * D_MODEL, SMALL_W)
    return pl.pallas_call(
        _inproj_body,
        out_shape=[jax.ShapeDtypeStruct((n, w), F32) for w in widths],
        grid=(n // tm,),
        in_specs=[pl.BlockSpec((tm, D_MODEL), lambda i: (i, 0)),
                  _const_spec((1, D_MODEL)), _const_spec((D_MODEL, wtot)), _const_spec((4, LANES))],
        out_specs=[pl.BlockSpec((tm, w), lambda i: (i, 0)) for w in widths],
        compiler_params=_cparams(("arbitrary",)),
        name="inproj",
    )(x2d, norm_g.reshape(1, D_MODEL), w_packed, qkn2)


def _split_bf16(v):
    hi = v.astype(BF16)
    lo = (v - hi.astype(F32)).astype(BF16)
    return hi, lo


def _ssd_body(lv, xbc_ref, z_ref, sm_ref, conv0_ref, ssm0_ref, cw_ref, cb_ref, dtb_ref, alog_ref, dsk_ref,
              nw_ref, expand_ref, y_ref, convo_ref, ssmo_ref, xp_ref, st_ref):
    cl = SSD_CL
    c = pl.program_id(1)
    nchunk = pl.num_programs(1)
    hp = M_HEADS // M_GROUPS * M_HEADDIM

    @pl.when(c == 0)
    def _():
        xp_ref[...] = jnp.zeros(xp_ref.shape, F32)
        xp_ref[8 - (D_CONV - 1):8, :] = conv0_ref[0]
        st_ref[...] = ssm0_ref[0].reshape(M_HEADS * M_HEADDIM, D_STATE).T

    xp_ref[8:8 + lv, :] = xbc_ref[0]
    x_all = xp_ref[...]
    conv = cb_ref[...] + x_all[8:8 + cl] * cw_ref[D_CONV - 1:D_CONV, :]
    for k in range(D_CONV - 1):
        conv = conv + pltpu.roll(x_all, D_CONV - 1 - k, axis=0)[8:8 + cl] * cw_ref[k:k + 1, :]
    tail = xp_ref[8 + lv - (D_CONV - 1):8 + lv, :]
    convo_ref[0] = tail
    xp_ref[8 - (D_CONV - 1):8, :] = tail
    xc = _silu(conv)
    xs = xc[:, :D_INNER]

    sm = sm_ref[0]
    z = z_ref[0]
    if lv < cl:
        sm = jnp.concatenate([sm, jnp.zeros((cl - lv, SMALL_W), F32)], axis=0)
        z = jnp.concatenate([z, jnp.zeros((cl - lv, D_INNER), F32)], axis=0)
    xdt_in = sm + dtb_ref[...]
    dt = jnp.maximum(xdt_in, 0.0) + jnp.log1p(jnp.exp(-jnp.abs(xdt_in)))
    row = lax.broadcasted_iota(jnp.int32, (cl, 1), 0)
    dt = jnp.where(row < lv, dt, 0.0)
    a_neg = -jnp.exp(alog_ref[...])
    da = dt * a_neg
    ri = lax.broadcasted_iota(jnp.int32, (cl, cl), 0)
    ci = lax.broadcasted_iota(jnp.int32, (cl, cl), 1)
    tril = ri >= ci
    acs = _mm(tril.astype(F32), da, precision=HIGHEST)
    alast = acs[cl - 1:cl, :]
    eye = (lax.broadcasted_iota(jnp.int32, (LANES, LANES), 0)
           == lax.broadcasted_iota(jnp.int32, (LANES, LANES), 1)).astype(F32)
    acs_t = _nt(eye, acs, precision=HIGHEST)

    expand = expand_ref[...]
    stack =jnp.concatenate([dt, jnp.exp(acs), dt * jnp.exp(alast - acs)], axis=0)
    s_hi, s_lo = _split_bf16(stack)
    ex = _mm(s_hi, expand) + _mm(s_lo, expand)
    dt_e = ex[0:cl]
    es_e = ex[cl:2 * cl]
    w_e = ex[2 * cl:3 * cl]
    xdt = (xs * dt_e).astype(BF16)
    xw = (xs * w_e).astype(BF16)
    chunk_decay = es_e[cl - 1:cl, :]

    lane = lax.broadcasted_iota(jnp.int32, (1, LANES), 1)
    lo_mask = lane < M_HEADDIM
    eye_b = eye.astype(BF16)
    y_parts = []
    for g in range(M_GROUPS):
        bg = xc[:, D_INNER + g * D_STATE:D_INNER + (g + 1) * D_STATE].astype(BF16)
        cg = xc[:, D_INNER + (M_GROUPS + g) * D_STATE:D_INNER + (M_GROUPS + g + 1) * D_STATE].astype(BF16)
        cb = _nt(cg, bg)
        st_g = st_ref[:, g * hp:(g + 1) * hp]
        y_off = _mm(cg, st_g.astype(BF16)) * es_e[:, g * hp:(g + 1) * hp]
        bg_t = _nt(eye_b, bg).astype(BF16)
        st_ref[:, g * hp:(g + 1) * hp] = (st_g * chunk_decay[:, g * hp:(g + 1) * hp]
                                          + _mm(bg_t, xw[:, g * hp:(g + 1) * hp]))
        for jj in range(hp // LANES):
            col0 = g * hp + jj * LANES
            ms = []
            for j in (2 * (col0 // LANES), 2 * (col0 // LANES) + 1):
                diff = acs[:, j:j + 1] - acs_t[j:j + 1, :]
                dec = jnp.exp(jnp.where(tril, diff, -jnp.inf))
                ms.append((cb * dec).astype(BF16))
            x2 = xdt[:, col0:col0 + LANES]
            zero = jnp.zeros_like(x2)
            rhs = jnp.concatenate([jnp.where(lo_mask, x2, zero), jnp.where(lo_mask, zero, x2)], axis=0)
            y_parts.append(_mm(jnp.concatenate(ms, axis=1), rhs) + y_off[:, jj * LANES:(jj + 1) * LANES])
    y = jnp.concatenate(y_parts, axis=1)
    y = (y + dsk_ref[...] * xs) * _silu(z)
    outs = []
    for g in range(M_GROUPS):
        yg = y[:, g * hp:(g + 1) * hp]
        outs.append(yg * lax.rsqrt(jnp.mean(yg * yg, axis=-1, keepdims=True) + EPS))
    y = jnp.concatenate(outs, axis=1) * nw_ref[...]
    y_ref[0] = y[:lv].astype(BF16)

    @pl.when(c == nchunk - 1)
    def _():
        ssmo_ref[0] = st_ref[...].T.reshape(M_HEADS, M_HEADDIM, D_STATE)


def _ssd(xbc, z, small, conv0, ssm0, conv_w, conv_b, dt_bias, a_log, d_skip, ssm_norm):
    b, L = xbc.shape[:2]
    lv = SSD_CL if L % SSD_CL == 0 else L
    assert lv == SSD_CL or (L < SSD_CL and L % 8 == 0)
    nchunk = L // lv
    pad = lambda v: jnp.pad(v.astype(F32), (0, SMALL_W - M_HEADS)).reshape(1, SMALL_W)
    rep = lambda v: jnp.repeat(v.astype(F32), M_HEADDIM).reshape(1, D_INNER)
    return pl.pallas_call(
        functools.partial(_ssd_body, lv),
        out_shape=[jax.ShapeDtypeStruct((b, L, D_INNER), BF16),
                   jax.ShapeDtypeStruct((b, D_CONV - 1, CONV_DIM), F32),
                   jax.ShapeDtypeStruct((b, M_HEADS, M_HEADDIM, D_STATE), F32)],
        grid=(b, nchunk),
        in_specs=[pl.BlockSpec((1, lv, CONV_DIM), lambda i, c: (i, c, 0)),
                  pl.BlockSpec((1, lv, D_INNER), lambda i, c: (i, c, 0)),
                  pl.BlockSpec((1, lv, SMALL_W), lambda i, c: (i, c, 0)),
                  pl.BlockSpec((1, D_CONV - 1, CONV_DIM), lambda i, c: (i, 0, 0)),
                  pl.BlockSpec((1, M_HEADS, M_HEADDIM, D_STATE), lambda i, c: (i, 0, 0, 0)),
                  _const_spec((D_CONV, CONV_DIM)), _const_spec((1, CONV_DIM)),
                  _const_spec((1, SMALL_W)), _const_spec((1, SMALL_W)),
                  _const_spec((1, D_INNER)), _const_spec((1, D_INNER)), _const_spec((LANES, D_INNER))],
        out_specs=[pl.BlockSpec((1, lv, D_INNER), lambda i, c: (i, c, 0)),
                   pl.BlockSpec((1, D_CONV - 1, CONV_DIM), lambda i, c: (i, 0, 0)),
                   pl.BlockSpec((1, M_HEADS, M_HEADDIM, D_STATE), lambda i, c: (i, 0, 0, 0))],
        scratch_shapes=[pltpu.VMEM((8 + SSD_CL, CONV_DIM), F32), pltpu.VMEM((D_STATE, D_INNER), F32)],
        compiler_params=_cparams(("arbitrary", "arbitrary")),
        name="ssd",
    )(xbc, z, small, conv0, ssm0, conv_w, conv_b.reshape(1, CONV_DIM), pad(dt_bias), pad(a_log),
      rep(d_skip), ssm_norm.reshape(1, D_INNER),
      (np.arange(LANES)[:, None] == np.arange(D_INNER)[None, :] // M_HEADDIM).astype(BF16))


NO_LIMIT = 2 ** 30


def _bias_table_body(base, rc, cc, lo, hi, ncol_valid, tab_ref, o_ref):
    _, rb, cb = o_ref.shape
    r = pl.program_id(0) * rb + lax.broadcasted_iota(jnp.int32, (rb, cb), 0)
    c = pl.program_id(1) * cb + lax.broadcasted_iota(jnp.int32, (rb, cb), 1)
    d = base + rc * r + cc * c
    valid = (d >= lo) & (d <= hi) & (c < ncol_valid)
    max_exact = NUM_BUCKETS // 2
    dd = jnp.maximum(d, 0)
    df = jnp.maximum(dd, 1).astype(F32)
    large = max_exact + (jnp.log(df / max_exact) / math.log(MAX_DISTANCE / max_exact)
                         * (NUM_BUCKETS - max_exact)).astype(jnp.int32)
    large = jnp.minimum(large, NUM_BUCKETS - 1)
    bucket = jnp.where(dd < max_exact, dd, large)
    for hq in range(Q_HEADS):
        acc = jnp.zeros((rb, cb), F32)
        for bkt in range(NUM_BUCKETS):
            acc = jnp.where(bucket == bkt, tab_ref[bkt, hq], acc)
        o_ref[hq] = jnp.where(valid, acc, NEG)


def _bias_table(rel_bias, rows, cols, base, rc, cc, lo=0, hi=NO_LIMIT, ncol_valid=NO_LIMIT):
    rb = LANES if rows > LANES and rows % LANES == 0 else rows
    cb = 2048 if cols > 2048 and cols % 2048 == 0 else cols
    return pl.pallas_call(
        functools.partial(_bias_table_body, base, rc, cc, lo, hi, ncol_valid),
        out_shape=jax.ShapeDtypeStruct((Q_HEADS, rows, cols), F32),
        grid=(rows // rb, cols // cb),
        in_specs=[pl.BlockSpec(memory_space=pltpu.SMEM)],
        out_specs=pl.BlockSpec((Q_HEADS, rb, cb), lambda i, j: (0, i, j)),
        compiler_params=_cparams(("arbitrary", "arbitrary")),
        name="bias_table",
    )(rel_bias.astype(F32))


def _prompt_bias_tables(rel_bias, L):
    n_off = L // TQ + 1
    n_off_w = WINDOW // TQ + 3
    tab_s = _bias_table(rel_bias, n_off * TQ, TQ, -TQ, 1, -1)
    tab_w = _bias_table(rel_bias, n_off_w * TQ, TQ, -TQ, 1, -1, hi=WINDOW)
    n_c = L // CMP_BLOCK
    tab_ct = _bias_table(rel_bias, n_c, L, -(CMP_BLOCK - 1), -CMP_BLOCK, 1)
    return (tab_s.reshape(KV_HEADS, Q_PER_KV, n_off, TQ, TQ), tab_w.reshape(KV_HEADS, Q_PER_KV, n_off_w, TQ, TQ),
            tab_ct.reshape(KV_HEADS, Q_PER_KV, n_c, L))


def _compress_weights(cmp_pe, cmp_w1, cmp_w2):
    eye = jnp.eye(2, dtype=F32)
    w1 = jnp.einsum("hg,cjdf->cjhdgf", eye, cmp_w1).reshape(2, CMP_BLOCK // 2, 2 * LANES, 2 * CMP_HIDDEN)
    w2 = jnp.einsum("hg,cfd->chfgd", eye, cmp_w2).reshape(2, 2 * CMP_HIDDEN, LANES)
    pe = jnp.tile(cmp_pe, (1, 1, 2))
    return pe.astype(F32), w1.astype(BF16), w2.astype(BF16)


def _compress_rows(lo_ref, hi_ref, nblk, pe_ref, w1_ref, w2_ref, c):
    acc = jnp.zeros((2 * nblk, 2 * CMP_HIDDEN), F32)
    regroup = lambda ref: pltpu.einshape("(nj)l->(jn)l", ref[0:nblk * CMP_BLOCK, :], j=CMP_BLOCK)
    lo_all = regroup(lo_ref)
    hi_all = regroup(hi_ref)
    for jj in range(CMP_BLOCK // 2):
        cols = []
        for j in (2 * jj, 2 * jj + 1):
            take = slice(j * nblk, (j + 1) * nblk)
            pe_j = pe_ref[c, j:j + 1, :]
            cols.append(jnp.concatenate([lo_all[take] + pe_j, hi_all[take] + pe_j], axis=0))
        acc = acc + _mm(jnp.concatenate(cols, axis=1).astype(BF16), w1_ref[c, jj])
    out = _mm(_silu(acc).astype(BF16), w2_ref[c])
    return jnp.concatenate([out[:nblk], out[nblk:]], axis=1)


def _compress_prompt_body(k0_ref, k1_ref, v0_ref, v1_ref, pe_ref, w1_ref, w2_ref, kn_ref, kc_ref, vc_ref):
    nblk = kc_ref.shape[1]
    kc_ref[0] = _head_rms(_compress_rows(k0_ref.at[0], k1_ref.at[0], nblk, pe_ref, w1_ref, w2_ref, 0), kn_ref[...])
    vc_ref[0] = _compress_rows(v0_ref.at[0], v1_ref.at[0], nblk, pe_ref, w1_ref, w2_ref, 1)


def _compress_prompt(paged, pe, w1, w2, kn2):
    b, L = paged.shape[:2]
    n_c = L // CMP_BLOCK
    col = lambda c: pl.BlockSpec((1, L, LANES), lambda i: (i, 0, c))
    return pl.pallas_call(
        _compress_prompt_body,
        out_shape=[jax.ShapeDtypeStruct((b, n_c, KV_W), F32)] * 2,
        grid=(b,),
        in_specs=[col(0), col(1), col(2), col(3),
                  _const_spec(pe.shape), _const_spec(w1.shape), _const_spec(w2.shape), _const_spec((1, LANES))],
        out_specs=[pl.BlockSpec((1, n_c, KV_W), lambda i: (i, 0, 0))] * 2,
        compiler_params=_cparams(("arbitrary",)),
        name="compress_prompt",
    )(paged, paged, paged, paged, pe, w1, w2, kn2)


def _online_softmax_step(s, carry, v):
    m, l, acc = carry
    m_new = jnp.maximum(m, jnp.max(s, axis=-1, keepdims=True))
    alpha = jnp.exp(m - m_new)
    p = jnp.exp(s - m_new)
    return (m_new, alpha * l + jnp.sum(p, axis=-1, keepdims=True), alpha * acc + _mm(p.astype(BF16), v))


def _select_blocks(imp, t, n_blk):
    big = 1e30
    j = lax.broadcasted_iota(jnp.int32, (1, imp.shape[1]), 1)
    cur = t // SEL_BLOCK
    forced = (j == 0) | (j == cur) | (j == cur - 1)
    causal = j * SEL_BLOCK <= t
    score = jnp.where(forced, big, jnp.where(causal, imp, -big))
    rank = jnp.zeros(imp.shape, F32)
    for i in range(n_blk):
        ci = score[:, i:i + 1]
        ahead = (ci > score) | ((ci == score) & (i < j))
        rank = rank + jnp.where(ahead, 1.0, 0.0)
    return jnp.where(rank < SEL_TOPK, 1.0, 0.0)


def _nsa_prompt_body(n_sel, q_ref, ks_ref, vs_ref, kw_ref, vw_ref, kc_ref, vc_ref, sm_ref, tabs_ref, tabw_ref,
                     tabc_ref, o_ref, kaug_scr, vaug_scr, kwin_scr, vwin_scr, s_scr, p_scr):
    h = pl.program_id(0)
    qi = pl.program_id(2)
    rows = Q_PER_KV * TQ
    lane = lax.broadcasted_iota(jnp.int32, (1, LANES), 1)
    mine = (lane // HEAD_DIM) == (h % 2)

    @pl.when(qi == 0)
    def _():
        seq = kaug_scr.shape[0]
        key = lax.broadcasted_iota(jnp.int32, (seq, LANES), 0)
        blk = lax.broadcasted_iota(jnp.int32, (seq, LANES), 1)
        kaug_scr[:, 0:LANES] = ks_ref[0].astype(BF16)
        kaug_scr[:, LANES:2 * LANES] = jnp.where(blk == key // SEL_BLOCK, NEG, 0.0).astype(BF16)
        vaug_scr[...] = jnp.where(mine, vs_ref[0], 1.0).astype(BF16)
        kwin_scr[...] = kw_ref[0].astype(BF16)
        vwin_scr[...] = jnp.where(mine, vw_ref[0], 1.0).astype(BF16)

    qb = q_ref[0]
    q4 = jnp.concatenate([qb[:, g * HEAD_DIM:(g + 1) * HEAD_DIM] for g in range(Q_PER_KV)], axis=0)
    qpad = jnp.where(mine, jnp.concatenate([q4, q4], axis=1), 0.0).astype(BF16)

    sub = 64
    even = (h % 2) == 0
    pick = lambda a: jnp.where(even, a[:, :HEAD_DIM], a[:, HEAD_DIM:])
    other = lambda a: jnp.where(even, a[:, HEAD_DIM:], a[:, :HEAD_DIM])

    def attend(q_lhs, k_slab, v_slab, tab_ref, kt0, n_kt):
        width = n_kt * TQ
        n_half = 2
        hrows = rows // n_half
        for hf in range(n_half):
            s_scr[hf * hrows:(hf + 1) * hrows, 0:width] = _nt(q_lhs[hf * hrows:(hf + 1) * hrows], k_slab)
        accs = []
        for hf in range(n_half):
            for rb in range(hf * hrows // sub, (hf + 1) * hrows // sub):
                g, part = divmod(rb, TQ // sub)
                rr = slice(rb * sub, (rb + 1) * sub)
                tr = slice(part * sub, (part + 1) * sub)
                run = None
                for kt in range(n_kt):
                    cc = slice(kt * TQ, (kt + 1) * TQ)
                    s = s_scr[rr, cc] + tab_ref[0, g, jnp.maximum(qi - (kt0 + kt) + 1, 0), tr, :]
                    s_scr[rr, cc] = s
                    run = s if run is None else jnp.maximum(run, s)
                mb = jnp.broadcast_to(jnp.max(run, axis=-1, keepdims=True), (sub, TQ))
                for kt in range(n_kt):
                    cc = slice(kt * TQ, (kt + 1) * TQ)
                    p_scr[rr, cc] = jnp.exp(s_scr[rr, cc] - mb).astype(BF16)
            accs.append(_mm(p_scr[hf * hrows:(hf + 1) * hrows, 0:width], v_slab))
        acc = jnp.concatenate(accs, axis=0)
        return pick(acc) / jnp.maximum(other(acc), 1e-30)

    n_tiles = kaug_scr.shape[0] // TQ
    n_win = min(WINDOW // TQ + 1, n_tiles)
    w0 = jnp.maximum(qi - (n_win - 1), 0)
    w_rows = pl.ds(pl.multiple_of(w0 * TQ, TQ), n_win * TQ)
    o_w = attend(qpad, kwin_scr[w_rows, :], vwin_scr[w_rows, :], tabw_ref, w0, n_win)

    n_c = kc_ref.shape[1]
    bias_ct = jnp.concatenate([tabc_ref[0, g] for g in range(Q_PER_KV)], axis=1)
    s_ct = _nt(kc_ref[0].astype(BF16), qpad) + bias_ct
    e_ct = jnp.where(bias_ct > 0.5 * NEG, jnp.exp(s_ct - jnp.max(s_ct, axis=0, keepdims=True)), 0.0)
    p_ct = e_ct / jnp.maximum(jnp.sum(e_ct, axis=0, keepdims=True), 1e-30)
    o_c = lax.dot_general(p_ct.astype(BF16), vc_ref[0].astype(BF16), (((0,), (0,)), ((), ())),
                          preferred_element_type=F32)

    imp_c = sum(p_ct[:, g * TQ:(g + 1) * TQ] for g in range(Q_PER_KV))
    nr = -(-n_sel // 8) * 8
    pair = (lax.broadcasted_iota(jnp.int32, (nr, n_c), 0)
            == lax.broadcasted_iota(jnp.int32, (nr, n_c), 1) // (SEL_BLOCK // CMP_BLOCK)).astype(BF16)
    c_hi = imp_c.astype(BF16)
    c_mid = (imp_c - c_hi.astype(F32)).astype(BF16)
    c_lo = (imp_c - c_hi.astype(F32) - c_mid.astype(F32)).astype(BF16)
    imp = _mm(pair, c_hi) + _mm(pair, c_mid) + _mm(pair, c_lo)
    t = qi * TQ + lax.broadcasted_iota(jnp.int32, (1, TQ), 1)
    j = lax.broadcasted_iota(jnp.int32, (nr, 1), 0)
    cur = t // SEL_BLOCK
    forced = (j == 0) | (j == cur) | (j == cur - 1)
    score = jnp.where(forced, 1e30, jnp.where(j * SEL_BLOCK <= t, imp, -1e30))
    rank = jnp.zeros((nr, TQ), F32)
    for i in range(n_sel):
        ri = score[i:i + 1, :]
        rank = rank + jnp.where((ri > score) | ((ri == score) & (i < j)), 1.0, 0.0)
    drop_t = jnp.where(rank < SEL_TOPK, 0.0, 1.0)
    drop_t = jnp.concatenate([drop_t, jnp.ones((LANES - nr, TQ), F32)], axis=0)
    drop = drop_t.T.astype(BF16)
    qaug = jnp.concatenate([qpad, jnp.concatenate([drop] * Q_PER_KV, axis=0)], axis=1)

    ext_step = 2
    extents = [min((e + 1) * ext_step, n_tiles) for e in range(-(-n_tiles // ext_step))]
    o_s = lax.switch(qi // ext_step,
                     [functools.partial(lambda n: attend(qaug, kaug_scr[0:n * TQ, :], vaug_scr[0:n * TQ, :],
                                                         tabs_ref, 0, n), n) for n in extents])
    o_c = pick(o_c)

    gates = jax.nn.sigmoid(pltpu.roll(sm_ref[0], SMALL_W - NSA_G_OFF - h * (3 * Q_PER_KV), axis=1))
    outs = []
    for g in range(Q_PER_KV):
        rs = slice(g * TQ, (g + 1) * TQ)
        outs.append(gates[:, 3 * g:3 * g + 1] * o_c[rs] + gates[:, 3 * g + 1:3 * g + 2] * o_s[rs]
                    + gates[:, 3 * g + 2:3 * g + 3] * o_w[rs])
    o_ref[0] = jnp.concatenate(outs, axis=1).astype(BF16)


def _nsa_prompt(qn, paged, win, kc, vc, small, tab_s, tab_w, tab_c):
    b, L = qn.shape[:2]
    assert L % TKC == 0
    n_c = L // CMP_BLOCK
    rows = Q_PER_KV * TQ
    seq = lambda col: pl.BlockSpec((1, L, LANES), lambda h, i, q: (i, 0, col(h)))
    cmp = pl.BlockSpec((1, n_c, LANES), lambda h, i, q: (i, 0, h // 2))
    return pl.pallas_call(
        functools.partial(_nsa_prompt_body, L // SEL_BLOCK),
        out_shape=jax.ShapeDtypeStruct((b, L, D_ATTN), BF16),
        grid=(KV_HEADS, b, L // TQ),
        in_specs=[pl.BlockSpec((1, TQ, KV_W), lambda h, i, q: (i, q, h)),
                  seq(lambda h: 4 + h // 2), seq(lambda h: 6 + h // 2),
                  seq(lambda h: h // 2), seq(lambda h: 2 + h // 2),
                  cmp, cmp,
                  pl.BlockSpec((1, TQ, SMALL_W), lambda h, i, q: (i, q, 0)),
                  pl.BlockSpec((1,) + tab_s.shape[1:], lambda h, i, q: (h, 0, 0, 0, 0)),
                  pl.BlockSpec((1,) + tab_w.shape[1:], lambda h, i, q: (h, 0, 0, 0, 0)),
                  pl.BlockSpec((1, Q_PER_KV, n_c, TQ), lambda h, i, q: (h, 0, 0, q))],
        out_specs=pl.BlockSpec((1, TQ, KV_W), lambda h, i, q: (i, q, h)),
        scratch_shapes=[pltpu.VMEM((L, 2 * LANES), BF16), pltpu.VMEM((L, LANES), BF16),
                        pltpu.VMEM((L, LANES), BF16), pltpu.VMEM((L, LANES), BF16),
                        pltpu.VMEM((rows, L), F32), pltpu.VMEM((rows, L), BF16)],
        compiler_params=_cparams(("arbitrary", "arbitrary", "arbitrary")),
        name="nsa_prompt",
    )(qn, paged, paged, win, win, kc, vc, small, tab_s, tab_w, tab_c)


def _merge_body(x_ref, y_ref, o_ref, mg_ref, wa_ref, wb_ref, wo_ref, out_ref):
    ya = _mm(y_ref[...], wa_ref[...])
    yb = _mm(o_ref[...], wb_ref[...])
    gm = jax.nn.sigmoid(mg_ref[...])
    mix = (gm[:, :D_MODEL] * ya + gm[:, D_MODEL:] * yb).astype(BF16)
    out_ref[...] = x_ref[...] + _mm(mix, wo_ref[...])


def _merge(x2d, y2d, o2d, mg, wa, wb, wo, tm):
    n = x2d.shape[0]
    row = lambda w: pl.BlockSpec((tm, w), lambda i: (i, 0))
    return pl.pallas_call(
        _merge_body,
        out_shape=jax.ShapeDtypeStruct((n, D_MODEL), F32),
        grid=(n // tm,),
        in_specs=[row(D_MODEL), row(D_INNER), row(D_ATTN), row(2 * D_MODEL),
                  _const_spec((D_INNER, D_MODEL)), _const_spec((D_ATTN, D_MODEL)), _const_spec((D_MODEL, D_MODEL))],
        out_specs=row(D_MODEL),
        compiler_params=_cparams(("arbitrary",)),
        name="merge",
    )(x2d, y2d, o2d, mg, wa.astype(BF16), wb.astype(BF16), wo.astype(BF16))


def _layer_prompt(x, lw, f_bias, tm):
    (ffn1_norm, ffn1_w_in, ffn1_w_out, mix_norm, w_packed, conv_w, conv_b, dt_bias, a_log, d_skip, ssm_norm,
     qk_norm, cmp_prep, w_branch_ssm, w_branch_attn, w_out, ffn2_norm, ffn2_w_in, ffn2_w_out) = lw
    b, L = x.shape[:2]
    n = b * L
    tm_wide = 2 * tm if n % (2 * tm) == 0 else tm
    x1 = _ffn(x.reshape(n, D_MODEL), ffn1_norm, ffn1_w_in, ffn1_w_out, tm_wide)
    z, xbc, qn, paged, win, mg, small = _inproj(x1, mix_norm, w_packed, qk_norm, tm)
    r3 = lambda a: a.reshape(b, L, a.shape[-1])
    conv0 = jnp.zeros((b, D_CONV - 1, CONV_DIM), F32)
    ssm0 = jnp.zeros((b, M_HEADS, M_HEADDIM, D_STATE), F32)
    y, new_conv, new_ssm = _ssd(r3(xbc), r3(z), r3(small), conv0, ssm0, conv_w, conv_b, dt_bias, a_log, d_skip,
                                ssm_norm)
    pe, w1, w2 = cmp_prep
    kn2 = jnp.concatenate([qk_norm[1], qk_norm[1]]).reshape(1, LANES)
    kc, vc = _compress_prompt(r3(paged), pe, w1, w2, kn2)
    tab_s, tab_w, tab_c = _prompt_bias_tables(f_bias, L)
    o = _nsa_prompt(r3(qn), r3(paged), r3(win), kc, vc, r3(small), tab_s, tab_w, tab_c)
    x2 = _merge(x1, y.reshape(n, D_INNER), o.reshape(n, D_ATTN), mg, w_branch_ssm, w_branch_attn, w_out, tm_wide)
    x3 = _ffn(x2, ffn2_norm, ffn2_w_in, ffn2_w_out, tm_wide)
    n_keep = min(WINDOW, L)
    new_kv = paged.reshape(b, L, 4, KV_HEADS, HEAD_DIM)
    new_win = r3(win)[:, L - n_keep:].reshape(b, n_keep, 2, KV_HEADS, HEAD_DIM)
    return x3.reshape(b, L, D_MODEL), new_kv, new_win, new_ssm, new_conv


PAGES_PER_STEP = 32


def _sample_qpad(qb):
    ls = qb.shape[0]
    blocks = []
    for hq in range(Q_HEADS):
        h = hq // Q_PER_KV
        parts = []
        if h > 0:
            parts.append(jnp.zeros((ls, h * HEAD_DIM), F32))
        parts.append(qb[:, hq * HEAD_DIM:(hq + 1) * HEAD_DIM])
        if h < KV_HEADS - 1:
            parts.append(jnp.zeros((ls, (KV_HEADS - 1 - h) * HEAD_DIM), F32))
        blocks.append(jnp.concatenate(parts, axis=1))
    return jnp.concatenate(blocks, axis=0).astype(BF16)


def _sample_head_piece(o, hq, ls):
    h = hq // Q_PER_KV
    return o[hq * ls:(hq + 1) * ls, h * HEAD_DIM:(h + 1) * HEAD_DIM]


def _chunk_copies(pt_ref, cache_ref, buf, sem, cols, width, bb, ii, sl, p):
    page = pt_ref[bb, ii * PAGES_PER_STEP + p]
    row0 = pl.multiple_of(p * PAGE_SIZE, PAGE_SIZE)
    return [pltpu.make_async_copy(cache_ref.at[page, :, pl.ds(c0, width)],
                                  buf.at[sl, k, pl.ds(row0, PAGE_SIZE), :], sem.at[sl])
            for k, c0 in enumerate(cols)]


def _stream_chunks(pt_ref, cache_ref, buf, sem, cols, width):
    b = pl.program_id(0)
    i = pl.program_id(1)
    nch = pl.num_programs(1)
    step = b * nch + i
    slot = step % 2

    def run(bb, ii, sl, start):
        def body(p, carry):
            for cp in _chunk_copies(pt_ref, cache_ref, buf, sem, cols, width, bb, ii, sl, p):
                cp.start() if start else cp.wait()
            return carry
        lax.fori_loop(0, PAGES_PER_STEP, body, 0)

    @pl.when(step == 0)
    def _():
        run(b, i, slot, True)

    nxt = step + 1

    @pl.when(nxt < pl.num_programs(0) * nch)
    def _():
        run(nxt // nch, nxt % nch, nxt % 2, True)

    run(b, i, slot, False)
    return slot


def _cmp_sample_body(past_len, pt_ref, cache_ref, q_ref, biasc_ref, pe_ref, w1_ref, w2_ref, kn_ref,
                     oc_ref, selm_ref, buf, sem, kc_s, vc_s):
    i = pl.program_id(1)
    slot = _stream_chunks(pt_ref, cache_ref, buf, sem, [k * LANES for k in range(4)], LANES)
    nblk = PAGES_PER_STEP * PAGE_SIZE // CMP_BLOCK
    rows0 = pl.multiple_of(i * nblk, nblk)
    kc_s[pl.ds(rows0, nblk), :] = _head_rms(
        _compress_rows(buf.at[slot, 0], buf.at[slot, 1], nblk, pe_ref, w1_ref, w2_ref, 0), kn_ref[...])
    vc_s[pl.ds(rows0, nblk), :] = _compress_rows(buf.at[slot, 2], buf.at[slot, 3], nblk, pe_ref, w1_ref, w2_ref, 1)

    @pl.when(i == pl.num_programs(1) - 1)
    def _():
        ls = q_ref.shape[1]
        n_c = kc_s.shape[0]
        s = _nt(_sample_qpad(q_ref[0]), kc_s[...].astype(BF16)) + biasc_ref[...]
        e = jnp.exp(s - jnp.max(s, axis=-1, keepdims=True))
        p = e / jnp.maximum(jnp.sum(e, axis=-1, keepdims=True), 1e-30)
        oc = _mm(p.astype(BF16), vc_s[...].astype(BF16))
        oc_ref[0] = jnp.concatenate([_sample_head_piece(oc, hq, ls) for hq in range(Q_HEADS)], axis=1)
        imp_c = jnp.sum(p.reshape(KV_HEADS, Q_PER_KV, ls, n_c), axis=1).reshape(KV_HEADS * ls, n_c)
        w = selm_ref.shape[2]
        pair = (lax.broadcasted_iota(jnp.int32, (n_c, w), 0) // (SEL_BLOCK // CMP_BLOCK)
                == lax.broadcasted_iota(jnp.int32, (n_c, w), 1)).astype(F32)
        imp = _mm(imp_c, pair, precision=HIGHEST)
        t = past_len + lax.broadcasted_iota(jnp.int32, (KV_HEADS * ls, 1), 0) % ls
        selm_ref[0] = _select_blocks(imp, t, past_len // SEL_BLOCK + 1)


def _nsa_sample_body(n_chunks, pt_ref, cache_ref, q_ref, selm_ref, oc_ref, sm_ref, pnew_ref, cwin_ref, wnew_ref,
                     biass_ref, biasw_ref, biasn_ref, spread_ref, o_ref, buf, sem, m_s, l_s, a_s):
    i = pl.program_id(1)
    slot = _stream_chunks(pt_ref, cache_ref, buf, sem, [2 * KV_W, 3 * KV_W], KV_W)
    ls = q_ref.shape[1]
    rows = Q_HEADS * ls
    tk = PAGES_PER_STEP * PAGE_SIZE
    qpad = _sample_qpad(q_ref[0])

    @pl.when(i == 0)
    def _():
        m_s[...] = jnp.full(m_s.shape, NEG, F32)
        l_s[...] = jnp.zeros(l_s.shape, F32)
        a_s[...] = jnp.zeros(a_s.shape, F32)

    sel = selm_ref[0]
    w = sel.shape[1]
    sel = jnp.broadcast_to(sel.reshape(KV_HEADS, 1, ls, w), (KV_HEADS, Q_PER_KV, ls, w)).reshape(rows, w)
    bpc = tk // SEL_BLOCK
    sel_c = sel[:, 0:LANES]
    for k in range(1, n_chunks):
        sel_c = jnp.where(i == k, sel[:, k * bpc:k * bpc + LANES], sel_c)
    keep = _mm(sel_c.astype(BF16), spread_ref[...])
    s = _nt(qpad, buf[slot, 0].astype(BF16)) + biass_ref[...] + jnp.where(keep > 0.5, 0.0, NEG)
    m, l, a = _online_softmax_step(s, (m_s[...], l_s[...], a_s[...]), buf[slot, 1].astype(BF16))
    m_s[...] = m
    l_s[...] = l
    a_s[...] = a

    @pl.when(i == pl.num_programs(1) - 1)
    def _():
        def pad_rows(x):
            return jnp.concatenate([x, jnp.zeros((LANES - ls, x.shape[1]), F32)], axis=0).astype(BF16)

        pn = pnew_ref[0]
        wn = wnew_ref[0]
        s_n = _nt(qpad, pad_rows(pn[:, 2 * KV_W:3 * KV_W])) + biasn_ref[...]
        _, l2, a2 = _online_softmax_step(s_n, (m, l, a), pad_rows(pn[:, 3 * KV_W:4 * KV_W]))
        o_s = a2 / jnp.maximum(l2, 1e-30)
        cw = cwin_ref[0]
        init = (jnp.full((rows, 1), NEG, F32), jnp.zeros((rows, 1), F32), jnp.zeros((rows, KV_W), F32))
        c1 = _online_softmax_step(_nt(qpad, cw[:, 0:KV_W].astype(BF16)) + biasw_ref[...], init,
                                  cw[:, KV_W:2 * KV_W].astype(BF16))
        s_wn = _nt(qpad, pad_rows(wn[:, 0:KV_W])) + biasn_ref[...]
        _, l3, a3 = _online_softmax_step(s_wn, c1, pad_rows(wn[:, KV_W:2 * KV_W]))
        o_w = a3 / jnp.maximum(l3, 1e-30)
        gates = jax.nn.sigmoid(sm_ref[0])
        oc = oc_ref[0]
        outs = []
        for hq in range(Q_HEADS):
            g0 = NSA_G_OFF + 3 * hq
            outs.append(gates[:, g0:g0 + 1] * oc[:, hq * HEAD_DIM:(hq + 1) * HEAD_DIM]
                        + gates[:, g0 + 1:g0 + 2] * _sample_head_piece(o_s, hq, ls)
                        + gates[:, g0 + 2:g0 + 3] * _sample_head_piece(o_w, hq, ls))
        o_ref[0] = jnp.concatenate(outs, axis=1).astype(BF16)


def _sample_bias_tables(rel_bias, past_len, ls, n_buf):
    rows2d = lambda t: t.reshape(Q_HEADS * ls, t.shape[2])
    bias_s = _bias_table(rel_bias, ls, past_len, past_len, 1, -1)
    n_c = past_len // CMP_BLOCK
    bias_c = _bias_table(rel_bias, ls, n_c, past_len - (CMP_BLOCK - 1), 1, -CMP_BLOCK)
    bias_w = _bias_table(rel_bias, ls, n_buf, n_buf, 1, -1, hi=WINDOW)
    bias_n = _bias_table(rel_bias, ls, LANES, 0, 1, -1, ncol_valid=ls)
    return rows2d(bias_s), rows2d(bias_c), rows2d(bias_w), rows2d(bias_n)


def _nsa_sample(qn, paged, win, small, cache_pages, page_table, cache_win, rel_bias, cmp_prep, kn2):
    b, ls = qn.shape[:2]
    n_pages = page_table.shape[1]
    past_len = n_pages * PAGE_SIZE
    n_buf = cache_win.shape[1]
    assert n_pages % PAGES_PER_STEP == 0 and ls <= SEL_BLOCK and ls % 8 == 0 and n_buf == min(WINDOW, past_len)
    nch = n_pages // PAGES_PER_STEP
    tk = PAGES_PER_STEP * PAGE_SIZE
    n_c = past_len // CMP_BLOCK
    n_sel = past_len // SEL_BLOCK + 1
    w_sel = -(-n_sel // LANES) * LANES
    rows = Q_HEADS * ls
    pe, w1, w2 = cmp_prep
    bias_s, bias_c, bias_w, bias_n = _sample_bias_tables(rel_bias, past_len, ls, n_buf)
    tok = lambda wd: pl.BlockSpec((1, ls, wd), lambda i, c, pt: (i, 0, 0))
    const = lambda shape: pl.BlockSpec(shape, lambda i, c, pt: (0,) * len(shape), pipeline_mode=pl.Buffered(1))
    o_c, selm = pl.pallas_call(
        functools.partial(_cmp_sample_body, past_len),
        out_shape=[jax.ShapeDtypeStruct((b, ls, D_ATTN), F32),
                   jax.ShapeDtypeStruct((b, KV_HEADS * ls, w_sel), F32)],
        grid_spec=pltpu.PrefetchScalarGridSpec(
            num_scalar_prefetch=1, grid=(b, nch),
            in_specs=[pl.BlockSpec(memory_space=pl.ANY), tok(D_ATTN), const((rows, n_c)),
                      const(pe.shape), const(w1.shape), const(w2.shape), const((1, LANES))],
            out_specs=[tok(D_ATTN), pl.BlockSpec((1, KV_HEADS * ls, w_sel), lambda i, c, pt: (i, 0, 0))],
            scratch_shapes=[pltpu.VMEM((2, 4, tk, LANES), F32), pltpu.SemaphoreType.DMA((2,)),
                            pltpu.VMEM((n_c, KV_W), F32), pltpu.VMEM((n_c, KV_W), F32)]),
        compiler_params=_cparams(("arbitrary", "arbitrary")),
        name="nsa_sample_compress",
    )(page_table, cache_pages, qn, bias_c, pe, w1, w2, kn2)
    assert (nch - 1) * (tk // SEL_BLOCK) + LANES <= w_sel
    spread = (np.arange(LANES)[:, None] == np.arange(tk)[None, :] // SEL_BLOCK).astype(BF16)
    return pl.pallas_call(
        functools.partial(_nsa_sample_body, nch),
        out_shape=jax.ShapeDtypeStruct((b, ls, D_ATTN), BF16),
        grid_spec=pltpu.PrefetchScalarGridSpec(
            num_scalar_prefetch=1, grid=(b, nch),
            in_specs=[pl.BlockSpec(memory_space=pl.ANY), tok(D_ATTN),
                      pl.BlockSpec((1, KV_HEADS * ls, w_sel), lambda i, c, pt: (i, 0, 0)),
                      tok(D_ATTN), tok(SMALL_W), tok(4 * KV_W),
                      pl.BlockSpec((1, n_buf, 2 * KV_W), lambda i, c, pt: (i, 0, 0)), tok(2 * KV_W),
                      pl.BlockSpec((rows, tk), lambda i, c, pt: (0, c)),
                      const((rows, n_buf)), const((rows, LANES)), const((LANES, tk))],
            out_specs=tok(D_ATTN),
            scratch_shapes=[pltpu.VMEM((2, 2, tk, KV_W), F32), pltpu.SemaphoreType.DMA((2,)),
                            pltpu.VMEM((rows, 1), F32), pltpu.VMEM((rows, 1), F32), pltpu.VMEM((rows, KV_W), F32)]),
        compiler_params=_cparams(("arbitrary", "arbitrary")),
        name="nsa_sample",
    )(page_table, cache_pages, qn, selm, o_c, small, paged, cache_win, win, bias_s, bias_w, bias_n, spread)


def _layer_sample(x, ssm0, conv0, cache_kv_l, page_table, cache_win_l, lw, f_bias):
    (ffn1_norm, ffn1_w_in, ffn1_w_out, mix_norm, w_packed, conv_w, conv_b, dt_bias, a_log, d_skip, ssm_norm,
     qk_norm, cmp_prep, w_branch_ssm, w_branch_attn, w_out, ffn2_norm, ffn2_w_in, ffn2_w_out) = lw
    b, ls = x.shape[:2]
    n = b * ls
    tm = n
    x1 = _ffn(x.reshape(n, D_MODEL), ffn1_norm, ffn1_w_in, ffn1_w_out, tm)
    z, xbc, qn, paged, win, mg, small = _inproj(x1, mix_norm, w_packed, qk_norm, tm)
    r3 = lambda a: a.reshape(b, ls, a.shape[-1])
    y, new_conv, new_ssm = _ssd(r3(xbc), r3(z), r3(small), conv0, ssm0, conv_w, conv_b, dt_bias, a_log, d_skip,
                                ssm_norm)
    kn2 = jnp.concatenate([qk_norm[1], qk_norm[1]]).reshape(1, LANES)
    n_pool = cache_kv_l.shape[0]
    cache_pages = cache_kv_l.reshape(n_pool, PAGE_SIZE, 4 * KV_W)
    n_buf = cache_win_l.shape[1]
    cwin = cache_win_l.reshape(b, n_buf, 2 * KV_W)
    o = _nsa_sample(r3(qn), r3(paged), r3(win), r3(small), cache_pages, page_table, cwin, f_bias, cmp_prep, kn2)
    x2 = _merge(x1, y.reshape(n, D_INNER), o.reshape(n, D_ATTN), mg, w_branch_ssm, w_branch_attn, w_out, tm)
    x3 = _ffn(x2, ffn2_norm, ffn2_w_in, ffn2_w_out, tm)
    n_keep = min(WINDOW, n_buf + ls)
    new_kv = paged.reshape(b, ls, 4, KV_HEADS, HEAD_DIM)
    win_all = jnp.concatenate([cwin, r3(win)], axis=1)
    new_win = win_all[:, n_buf + ls - n_keep:].reshape(b, n_keep, 2, KV_HEADS, HEAD_DIM)
    return x3.reshape(b, ls, D_MODEL), new_kv, new_win, new_ssm, new_conv


def _layer_weights(l, ffn1_norm, ffn1_w_in, ffn1_w_out, mix_norm, w_in_proj, conv_w, conv_b, dt_bias, a_log,
                   d_skip, ssm_norm, qk_norm, cmp_pe, cmp_w1, cmp_w2, w_branch_ssm, w_branch_attn, w_out,
                   ffn2_norm, ffn2_w_in, ffn2_w_out):
    return (ffn1_norm[l], ffn1_w_in[l], ffn1_w_out[l], mix_norm[l], _pack_inproj_weight(w_in_proj[l]), conv_w[l],
            conv_b[l], dt_bias[l], a_log[l], d_skip[l], ssm_norm[l], qk_norm[l],
            _compress_weights(cmp_pe[l], cmp_w1[l], cmp_w2[l]), w_branch_ssm[l], w_branch_attn[l], w_out[l],
            ffn2_norm[l], ffn2_w_in[l], ffn2_w_out[l])


def kernel(x_prompt, x_sample, cache_kv, cache_win, state_ssm, state_conv, page_table, rel_bias, ffn1_norm, ffn1_w_in, ffn1_w_out, mix_norm, w_in_proj, conv_w, conv_b, dt_bias, a_log, d_skip, ssm_norm, qk_norm, cmp_pe, cmp_w1, cmp_w2, w_branch_ssm, w_branch_attn, w_out, ffn2_norm, ffn2_w_in, ffn2_w_out):
    depth = ffn1_norm.shape[0]
    f_bias = rel_bias
    xp, xs = x_prompt, x_sample
    outs_p, outs_s = [], []
    for l in range(depth):
        lw = _layer_weights(l, ffn1_norm, ffn1_w_in, ffn1_w_out, mix_norm, w_in_proj, conv_w, conv_b, dt_bias,
                            a_log, d_skip, ssm_norm, qk_norm, cmp_pe, cmp_w1, cmp_w2, w_branch_ssm,
                            w_branch_attn, w_out, ffn2_norm, ffn2_w_in, ffn2_w_out)
        xp, *rp = _layer_prompt(xp, lw, f_bias, 256)
        outs_p.append(rp)
        xs, *rs = _layer_sample(xs, state_ssm[l], state_conv[l], cache_kv[l], page_table, cache_win[l], lw, f_bias)
        outs_s.append(rs)
    stack = lambda outs, k: jnp.stack([o[k] for o in outs])
    return ((xp, xs) + tuple(stack(outs_p, k) for k in range(4)) + tuple(stack(outs_s, k) for k in range(4)))
```

```python
import functools
import math

import jax
import jax.numpy as jnp
import numpy as np
from jax import lax
from jax.experimental import pallas as pl
from jax.experimental.pallas import tpu as pltpu

F32 = jnp.float32
BF16 = jnp.bfloat16
HIGHEST = lax.Precision.HIGHEST

D_MODEL = 1024
D_FF = 2816
D_INNER = 2048
M_HEADDIM = 64
M_HEADS = 32
M_GROUPS = 4
D_STATE = 128
D_CONV = 4
CONV_DIM = D_INNER + 2 * M_GROUPS * D_STATE
HEAD_DIM = 64
Q_HEADS = 16
KV_HEADS = 4
Q_PER_KV = 4
D_ATTN = Q_HEADS * HEAD_DIM
KV_W = KV_HEADS * HEAD_DIM
CMP_BLOCK = 32
CMP_HIDDEN = 128
SEL_BLOCK = 64
SEL_TOPK = 16
WINDOW = 512
NUM_BUCKETS = 32
MAX_DISTANCE = 2048
PAGE_SIZE = 128
EPS = 1e-6
NEG = -1e30

LANES = 128
SSD_CL = 128
TQ = 128
TKC = 256
VMEM_LIMIT = 56 * 1024 * 1024
SMALL_W = 128
NSA_G_OFF = M_HEADS


def _cparams(sem):
    return pltpu.CompilerParams(dimension_semantics=sem, vmem_limit_bytes=VMEM_LIMIT)


def _const_spec(shape):
    nd = len(shape)
    return pl.BlockSpec(shape, lambda *a: (0,) * nd, pipeline_mode=pl.Buffered(1))


def _nt(a, b, precision=None):
    return lax.dot_general(a, b, (((1,), (1,)), ((), ())), preferred_element_type=F32, precision=precision)


def _mm(a, b, precision=None):
    return jnp.dot(a, b, preferred_element_type=F32, precision=precision)


def _rms_rows(x, g):
    return x * lax.rsqrt(jnp.mean(x * x, axis=-1, keepdims=True) + EPS) * g


def _silu(x):
    return x * jax.nn.sigmoid(x)


def _head_rms(a, gain2):
    lane = lax.broadcasted_iota(jnp.int32, (1, LANES), 1)
    lo_mask = lane < HEAD_DIM
    outs = []
    for c in range(a.shape[1] // LANES):
        blk = a[:, c * LANES:(c + 1) * LANES]
        sq = blk * blk
        lo = jnp.sum(jnp.where(lo_mask, sq, 0.0), axis=-1, keepdims=True)
        hi = jnp.sum(jnp.where(lo_mask, 0.0, sq), axis=-1, keepdims=True)
        r = jnp.where(lo_mask, lax.rsqrt(lo / HEAD_DIM + EPS), lax.rsqrt(hi / HEAD_DIM + EPS))
        outs.append(blk * r * gain2)
    return outs[0] if len(outs) == 1 else jnp.concatenate(outs, axis=1)


def _ffn_body(x_ref, g_ref, wg_ref, wu_ref, wo_ref, o_ref):
    x = x_ref[...]
    hb = _rms_rows(x, g_ref[...]).astype(BF16)
    gate = _mm(hb, wg_ref[...])
    up = _mm(hb, wu_ref[...])
    act = (_silu(gate) * up).astype(BF16)
    o_ref[...] = x + 0.5 * _mm(act, wo_ref[...])


def _ffn(x2d, norm_g, w_in, w_out, tm):
    n = x2d.shape[0]
    wg = w_in[:, :D_FF].astype(BF16)
    wu = w_in[:, D_FF:].astype(BF16)
    wo = w_out.astype(BF16)
    return pl.pallas_call(
        _ffn_body,
        out_shape=jax.ShapeDtypeStruct((n, D_MODEL), F32),
        grid=(n // tm,),
        in_specs=[pl.BlockSpec((tm, D_MODEL), lambda i: (i, 0)),
                  _const_spec((1, D_MODEL)),
                  _const_spec((D_MODEL, D_FF)), _const_spec((D_MODEL, D_FF)), _const_spec((D_FF, D_MODEL))],
        out_specs=pl.BlockSpec((tm, D_MODEL), lambda i: (i, 0)),
        compiler_params=_cparams(("arbitrary",)),
        name="ffn",
    )(x2d, norm_g.reshape(1, D_MODEL), wg, wu, wo)


def _inproj_body(x_ref, g_ref, w_ref, qkn_ref, z_ref, xbc_ref, q_ref, pg_ref, win_ref, mg_ref, sm_ref):
    hb = _rms_rows(x_ref[...], g_ref[...]).astype(BF16)
    o = 0
    z_ref[...] = _mm(hb, w_ref[:, o:o + D_INNER]); o += D_INNER
    xbc_ref[...] = _mm(hb, w_ref[:, o:o + CONV_DIM]); o += CONV_DIM
    q = _mm(hb, w_ref[:, o:o + D_ATTN]); o += D_ATTN
    q_ref[...] = _head_rms(q, qkn_ref[0:1, :] * (HEAD_DIM ** -0.5))
    kv = _mm(hb, w_ref[:, o:o + 6 * KV_W]); o += 6 * KV_W
    pg_ref[:, 0:2 * KV_W] = kv[:, 0:2 * KV_W]
    pg_ref[:, 2 * KV_W:3 * KV_W] = _head_rms(kv[:, 2 * KV_W:3 * KV_W], qkn_ref[2:3, :])
    pg_ref[:, 3 * KV_W:4 * KV_W] = kv[:, 3 * KV_W:4 * KV_W]
    win_ref[:, 0:KV_W] = _head_rms(kv[:, 4 * KV_W:5 * KV_W], qkn_ref[3:4, :])
    win_ref[:, KV_W:2 * KV_W] = kv[:, 5 * KV_W:6 * KV_W]
    mg_ref[...] = _mm(hb, w_ref[:, o:o + 2 * D_MODEL]); o += 2 * D_MODEL
    sm_ref[...] = _mm(hb, w_ref[:, o:o + SMALL_W])


def _pack_inproj_weight(w):
    c = np.cumsum([0, D_INNER, CONV_DIM, M_HEADS, D_ATTN, 6 * KV_W, 3 * Q_HEADS, 2 * D_MODEL])
    z, xbc, dt, q, kv, ng, mg = [w[:, c[i]:c[i + 1]] for i in range(7)]
    pad = jnp.zeros((D_MODEL, SMALL_W - M_HEADS - 3 * Q_HEADS), w.dtype)
    return jnp.concatenate([z, xbc, q, kv, mg, dt, ng, pad], axis=1).astype(BF16)


def _inproj(x2d, norm_g, w_packed, qk_norm, tm):
    n = x2d.shape[0]
    wtot = w_packed.shape[1]
    qkn2 = jnp.concatenate([qk_norm, qk_norm], axis=1)
    widths = (D_INNER, CONV_DIM, D_ATTN, 4 * KV_W, 2 * KV_W, 2 * D_MODEL, SMALL_W)
    return pl.pallas_call(
        _inproj_body,
        out_shape=[jax.ShapeDtypeStruct((n, w), F32) for w in widths],
        grid=(n // tm,),
        in_specs=[pl.BlockSpec((tm, D_MODEL), lambda i: (i, 0)),
                  _const_spec((1, D_MODEL)), _const_spec((D_MODEL, wtot)), _const_spec((4, LANES))],
        out_specs=[pl.BlockSpec((tm, w), lambda i: (i, 0)) for w in widths],
        compiler_params=_cparams(("arbitrary",)),
        name="inproj",
    )(x2d, norm_g.reshape(1, D_MODEL), w_packed, qkn2)


def _split_bf16(v):
    hi = v.astype(BF16)
    lo = (v - hi.astype(F32)).astype(BF16)
    return hi, lo


def _ssd_body(lv, xbc_ref, z_ref, sm_ref, conv0_ref, ssm0_ref, cw_ref, cb_ref, dtb_ref, alog_ref, dsk_ref,
              nw_ref, expand_ref, y_ref, convo_ref, ssmo_ref, xp_ref, st_ref):
    cl = SSD_CL
    c = pl.program_id(1)
    nchunk = pl.num_programs(1)
    hp = M_HEADS // M_GROUPS * M_HEADDIM

    @pl.when(c == 0)
    def _():
        xp_ref[...] = jnp.zeros(xp_ref.shape, F32)
        xp_ref[8 - (D_CONV - 1):8, :] = conv0_ref[0]
        st_ref[...] = ssm0_ref[0].reshape(M_HEADS * M_HEADDIM, D_STATE).T

    xp_ref[8:8 + lv, :] = xbc_ref[0]
    x_all = xp_ref[...]
    conv = cb_ref[...] + x_all[8:8 + cl] * cw_ref[D_CONV - 1:D_CONV, :]
    for k in range(D_CONV - 1):
        conv = conv + pltpu.roll(x_all, D_CONV - 1 - k, axis=0)[8:8 + cl] * cw_ref[k:k + 1, :]
    tail = xp_ref[8 + lv - (D_CONV - 1):8 + lv, :]
    convo_ref[0] = tail
    xp_ref[8 - (D_CONV - 1):8, :] = tail
    xc = _silu(conv)
    xs = xc[:, :D_INNER]

    sm = sm_ref[0]
    z = z_ref[0]
    if lv < cl:
        sm = jnp.concatenate([sm, jnp.zeros((cl - lv, SMALL_W), F32)], axis=0)
        z = jnp.concatenate([z, jnp.zeros((cl - lv, D_INNER), F32)], axis=0)
    xdt_in = sm + dtb_ref[...]
    dt = jnp.maximum(xdt_in, 0.0) + jnp.log1p(jnp.exp(-jnp.abs(xdt_in)))
    row = lax.broadcasted_iota(jnp.int32, (cl, 1), 0)
    dt = jnp.where(row < lv, dt, 0.0)
    a_neg = -jnp.exp(alog_ref[...])
    da = dt * a_neg
    ri = lax.broadcasted_iota(jnp.int32, (cl, cl), 0)
    ci = lax.broadcasted_iota(jnp.int32, (cl, cl), 1)
    tril = ri >= ci
    acs = _mm(tril.astype(F32), da, precision=HIGHEST)
    alast = acs[cl - 1:cl, :]
    eye = (lax.broadcasted_iota(jnp.int32, (LANES, LANES), 0)
           == lax.broadcasted_iota(jnp.int32, (LANES, LANES), 1)).astype(F32)
    acs_t = _nt(eye, acs, precision=HIGHEST)

    expand = expand_ref[...]
    stack =jnp.concatenate([dt, jnp.exp(acs), dt * jnp.exp(alast - acs)], axis=0)
    s_hi, s_lo = _split_bf16(stack)
    ex = _mm(s_hi, expand) + _mm(s_lo, expand)
    dt_e = ex[0:cl]
    es_e = ex[cl:2 * cl]
    w_e = ex[2 * cl:3 * cl]
    xdt = (xs * dt_e).astype(BF16)
    xw = (xs * w_e).astype(BF16)
    chunk_decay = es_e[cl - 1:cl, :]

    lane = lax.broadcasted_iota(jnp.int32, (1, LANES), 1)
    lo_mask = lane < M_HEADDIM
    eye_b = eye.astype(BF16)
    y_parts = []
    for g in range(M_GROUPS):
        bg = xc[:, D_INNER + g * D_STATE:D_INNER + (g + 1) * D_STATE].astype(BF16)
        cg = xc[:, D_INNER + (M_GROUPS + g) * D_STATE:D_INNER + (M_GROUPS + g + 1) * D_STATE].astype(BF16)
        cb = _nt(cg, bg)
        st_g = st_ref[:, g * hp:(g + 1) * hp]
        y_off = _mm(cg, st_g.astype(BF16)) * es_e[:, g * hp:(g + 1) * hp]
        bg_t = _nt(eye_b, bg).astype(BF16)
        st_ref[:, g * hp:(g + 1) * hp] = (st_g * chunk_decay[:, g * hp:(g + 1) * hp]
                                          + _mm(bg_t, xw[:, g * hp:(g + 1) * hp]))
        for jj in range(hp // LANES):
            col0 = g * hp + jj * LANES
            ms = []
            for j in (2 * (col0 // LANES), 2 * (col0 // LANES) + 1):
                diff = acs[:, j:j + 1] - acs_t[j:j + 1, :]
                dec = jnp.exp(jnp.where(tril, diff, -jnp.inf))
                ms.append((cb * dec).astype(BF16))
            x2 = xdt[:, col0:col0 + LANES]
            zero = jnp.zeros_like(x2)
            rhs = jnp.concatenate([jnp.where(lo_mask, x2, zero), jnp.where(lo_mask, zero, x2)], axis=0)
            y_parts.append(_mm(jnp.concatenate(ms, axis=1), rhs) + y_off[:, jj * LANES:(jj + 1) * LANES])
    y = jnp.concatenate(y_parts, axis=1)
    y = (y + dsk_ref[...] * xs) * _silu(z)
    outs = []
    for g in range(M_GROUPS):
        yg = y[:, g * hp:(g + 1) * hp]
        outs.append(yg * lax.rsqrt(jnp.mean(yg * yg, axis=-1, keepdims=True) + EPS))
    y = jnp.concatenate(outs, axis=1) * nw_ref[...]
    y_ref[0] = y[:lv].astype(BF16)

    @pl.when(c == nchunk - 1)
    def _():
        ssmo_ref[0] = st_ref[...].T.reshape(M_HEADS, M_HEADDIM, D_STATE)


def _ssd(xbc, z, small, conv0, ssm0, conv_w, conv_b, dt_bias, a_log, d_skip, ssm_norm):
    b, L = xbc.shape[:2]
    lv = SSD_CL if L % SSD_CL == 0 else L
    assert lv == SSD_CL or (L < SSD_CL and L % 8 == 0)
    nchunk = L // lv
    pad = lambda v: jnp.pad(v.astype(F32), (0, SMALL_W - M_HEADS)).reshape(1, SMALL_W)
    rep = lambda v: jnp.repeat(v.astype(F32), M_HEADDIM).reshape(1, D_INNER)
    return pl.pallas_call(
        functools.partial(_ssd_body, lv),
        out_shape=[jax.ShapeDtypeStruct((b, L, D_INNER), BF16),
                   jax.ShapeDtypeStruct((b, D_CONV - 1, CONV_DIM), F32),
                   jax.ShapeDtypeStruct((b, M_HEADS, M_HEADDIM, D_STATE), F32)],
        grid=(b, nchunk),
        in_specs=[pl.BlockSpec((1, lv, CONV_DIM), lambda i, c: (i, c, 0)),
                  pl.BlockSpec((1, lv, D_INNER), lambda i, c: (i, c, 0)),
                  pl.BlockSpec((1, lv, SMALL_W), lambda i, c: (i, c, 0)),
                  pl.BlockSpec((1, D_CONV - 1, CONV_DIM), lambda i, c: (i, 0, 0)),
                  pl.BlockSpec((1, M_HEADS, M_HEADDIM, D_STATE), lambda i, c: (i, 0, 0, 0)),
                  _const_spec((D_CONV, CONV_DIM)), _const_spec((1, CONV_DIM)),
                  _const_spec((1, SMALL_W)), _const_spec((1, SMALL_W)),
                  _const_spec((1, D_INNER)), _const_spec((1, D_INNER)), _const_spec((LANES, D_INNER))],
        out_specs=[pl.BlockSpec((1, lv, D_INNER), lambda i, c: (i, c, 0)),
                   pl.BlockSpec((1, D_CONV - 1, CONV_DIM), lambda i, c: (i, 0, 0)),
                   pl.BlockSpec((1, M_HEADS, M_HEADDIM, D_STATE), lambda i, c: (i, 0, 0, 0))],
        scratch_shapes=[pltpu.VMEM((8 + SSD_CL, CONV_DIM), F32), pltpu.VMEM((D_STATE, D_INNER), F32)],
        compiler_params=_cparams(("arbitrary", "arbitrary")),
        name="ssd",
    )(xbc, z, small, conv0, ssm0, conv_w, conv_b.reshape(1, CONV_DIM), pad(dt_bias), pad(a_log),
      rep(d_skip), ssm_norm.reshape(1, D_INNER),
      (np.arange(LANES)[:, None] == np.arange(D_INNER)[None, :] // M_HEADDIM).astype(BF16))


NO_LIMIT = 2 ** 30


def _bias_table_body(base, rc, cc, lo, hi, ncol_valid, tab_ref, o_ref):
    _, rb, cb = o_ref.shape
    r = pl.program_id(0) * rb + lax.broadcasted_iota(jnp.int32, (rb, cb), 0)
    c = pl.program_id(1) * cb + lax.broadcasted_iota(jnp.int32, (rb, cb), 1)
    d = base + rc * r + cc * c
    valid = (d >= lo) & (d <= hi) & (c < ncol_valid)
    max_exact = NUM_BUCKETS // 2
    dd = jnp.maximum(d, 0)
    df = jnp.maximum(dd, 1).astype(F32)
    large = max_exact + (jnp.log(df / max_exact) / math.log(MAX_DISTANCE / max_exact)
                         * (NUM_BUCKETS - max_exact)).astype(jnp.int32)
    large = jnp.minimum(large, NUM_BUCKETS - 1)
    bucket = jnp.where(dd < max_exact, dd, large)
    for hq in range(Q_HEADS):
        acc = jnp.zeros((rb, cb), F32)
        for bkt in range(NUM_BUCKETS):
            acc = jnp.where(bucket == bkt, tab_ref[bkt, hq], acc)
        o_ref[hq] = jnp.where(valid, acc, NEG)


def _bias_table(rel_bias, rows, cols, base, rc, cc, lo=0, hi=NO_LIMIT, ncol_valid=NO_LIMIT):
    rb = LANES if rows > LANES and rows % LANES == 0 else rows
    cb = 2048 if cols > 2048 and cols % 2048 == 0 else cols
    return pl.pallas_call(
        functools.partial(_bias_table_body, base, rc, cc, lo, hi, ncol_valid),
        out_shape=jax.ShapeDtypeStruct((Q_HEADS, rows, cols), F32),
        grid=(rows // rb, cols // cb),
        in_specs=[pl.BlockSpec(memory_space=pltpu.SMEM)],
        out_specs=pl.BlockSpec((Q_HEADS, rb, cb), lambda i, j: (0, i, j)),
        compiler_params=_cparams(("arbitrary", "arbitrary")),
        name="bias_table",
    )(rel_bias.astype(F32))


def _prompt_bias_tables(rel_bias, L):
    n_off = L // TQ + 1
    n_off_w = WINDOW // TQ + 3
    tab_s = _bias_table(rel_bias, n_off * TQ, TQ, -TQ, 1, -1)
    tab_w = _bias_table(rel_bias, n_off_w * TQ, TQ, -TQ, 1, -1, hi=WINDOW)
    n_c = L // CMP_BLOCK
    tab_ct = _bias_table(rel_bias, n_c, L, -(CMP_BLOCK - 1), -CMP_BLOCK, 1)
    return (tab_s.reshape(KV_HEADS, Q_PER_KV, n_off, TQ, TQ), tab_w.reshape(KV_HEADS, Q_PER_KV, n_off_w, TQ, TQ),
            tab_ct.reshape(KV_HEADS, Q_PER_KV, n_c, L))


def _compress_weights(cmp_pe, cmp_w1, cmp_w2):
    eye = jnp.eye(2, dtype=F32)
    w1 = jnp.einsum("hg,cjdf->cjhdgf", eye, cmp_w1).reshape(2, CMP_BLOCK // 2, 2 * LANES, 2 * CMP_HIDDEN)
    w2 = jnp.einsum("hg,cfd->chfgd", eye, cmp_w2).reshape(2, 2 * CMP_HIDDEN, LANES)
    pe = jnp.tile(cmp_pe, (1, 1, 2))
    return pe.astype(F32), w1.astype(BF16), w2.astype(BF16)


def _compress_rows(lo_ref, hi_ref, nblk, pe_ref, w1_ref, w2_ref, c):
    acc = jnp.zeros((2 * nblk, 2 * CMP_HIDDEN), F32)
    regroup = lambda ref: pltpu.einshape("(nj)l->(jn)l", ref[0:nblk * CMP_BLOCK, :], j=CMP_BLOCK)
    lo_all = regroup(lo_ref)
    hi_all = regroup(hi_ref)
    for jj in range(CMP_BLOCK // 2):
        cols = []
        for j in (2 * jj, 2 * jj + 1):
            take = slice(j * nblk, (j + 1) * nblk)
            pe_j = pe_ref[c, j:j + 1, :]
            cols.append(jnp.concatenate([lo_all[take] + pe_j, hi_all[take] + pe_j], axis=0))
        acc = acc + _mm(jnp.concatenate(cols, axis=1).astype(BF16), w1_ref[c, jj])
    out = _mm(_silu(acc).astype(BF16), w2_ref[c])
    return jnp.concatenate([out[:nblk], out[nblk:]], axis=1)


def _compress_prompt_body(k0_ref, k1_ref, v0_ref, v1_ref, pe_ref, w1_ref, w2_ref, kn_ref, kc_ref, vc_ref):
    nblk = kc_ref.shape[1]
    kc_ref[0] = _head_rms(_compress_rows(k0_ref.at[0], k1_ref.at[0], nblk, pe_ref, w1_ref, w2_ref, 0), kn_ref[...])
    vc_ref[0] = _compress_rows(v0_ref.at[0], v1_ref.at[0], nblk, pe_ref, w1_ref, w2_ref, 1)


def _compress_prompt(paged, pe, w1, w2, kn2):
    b, L = paged.shape[:2]
    n_c = L // CMP_BLOCK
    col = lambda c: pl.BlockSpec((1, L, LANES), lambda i: (i, 0, c))
    return pl.pallas_call(
        _compress_prompt_body,
        out_shape=[jax.ShapeDtypeStruct((b, n_c, KV_W), F32)] * 2,
        grid=(b,),
        in_specs=[col(0), col(1), col(2), col(3),
                  _const_spec(pe.shape), _const_spec(w1.shape), _const_spec(w2.shape), _const_spec((1, LANES))],
        out_specs=[pl.BlockSpec((1, n_c, KV_W), lambda i: (i, 0, 0))] * 2,
        compiler_params=_cparams(("arbitrary",)),
        name="compress_prompt",
    )(paged, paged, paged, paged, pe, w1, w2, kn2)


def _online_softmax_step(s, carry, v):
    m, l, acc = carry
    m_new = jnp.maximum(m, jnp.max(s, axis=-1, keepdims=True))
    alpha = jnp.exp(m - m_new)
    p = jnp.exp(s - m_new)
    return (m_new, alpha * l + jnp.sum(p, axis=-1, keepdims=True), alpha * acc + _mm(p.astype(BF16), v))


def _select_blocks(imp, t, n_blk):
    big = 1e30
    j = lax.broadcasted_iota(jnp.int32, (1, imp.shape[1]), 1)
    cur = t // SEL_BLOCK
    forced = (j == 0) | (j == cur) | (j == cur - 1)
    causal = j * SEL_BLOCK <= t
    score = jnp.where(forced, big, jnp.where(causal, imp, -big))
    rank = jnp.zeros(imp.shape, F32)
    for i in range(n_blk):
        ci = score[:, i:i + 1]
        ahead = (ci > score) | ((ci == score) & (i < j))
        rank = rank + jnp.where(ahead, 1.0, 0.0)
    return jnp.where(rank < SEL_TOPK, 1.0, 0.0)


def _nsa_prompt_body(n_sel, q_ref, ks_ref, vs_ref, kw_ref, vw_ref, kc_ref, vc_ref, sm_ref, tabs_ref, tabw_ref,
                     tabc_ref, o_ref, kaug_scr, vaug_scr, kwin_scr, vwin_scr, s_scr, p_scr):
    h = pl.program_id(0)
    qi = pl.program_id(2)
    rows = Q_PER_KV * TQ
    lane = lax.broadcasted_iota(jnp.int32, (1, LANES), 1)
    mine = (lane // HEAD_DIM) == (h % 2)

    @pl.when(qi == 0)
    def _():
        seq = kaug_scr.shape[0]
        key = lax.broadcasted_iota(jnp.int32, (seq, LANES), 0)
        blk = lax.broadcasted_iota(jnp.int32, (seq, LANES), 1)
        kaug_scr[:, 0:LANES] = ks_ref[0].astype(BF16)
        kaug_scr[:, LANES:2 * LANES] = jnp.where(blk == key // SEL_BLOCK, NEG, 0.0).astype(BF16)
        vaug_scr[...] = jnp.where(mine, vs_ref[0], 1.0).astype(BF16)
        kwin_scr[...] = kw_ref[0].astype(BF16)
        vwin_scr[...] = jnp.where(mine, vw_ref[0], 1.0).astype(BF16)

    qb = q_ref[0]
    q4 = jnp.concatenate([qb[:, g * HEAD_DIM:(g + 1) * HEAD_DIM] for g in range(Q_PER_KV)], axis=0)
    qpad = jnp.where(mine, jnp.concatenate([q4, q4], axis=1), 0.0).astype(BF16)

    sub = 64
    even = (h % 2) == 0
    pick = lambda a: jnp.where(even, a[:, :HEAD_DIM], a[:, HEAD_DIM:])
    other = lambda a: jnp.where(even, a[:, HEAD_DIM:], a[:, :HEAD_DIM])

    def attend(q_lhs, k_slab, v_slab, tab_ref, kt0, n_kt):
        width = n_kt * TQ
        n_half = 2
        hrows = rows // n_half
        for hf in range(n_half):
            s_scr[hf * hrows:(hf + 1) * hrows, 0:width] = _nt(q_lhs[hf * hrows:(hf + 1) * hrows], k_slab)
        accs = []
        for hf in range(n_half):
            for rb in range(hf * hrows // sub, (hf + 1) * hrows // sub):
                g, part = divmod(rb, TQ // sub)
                rr = slice(rb * sub, (rb + 1) * sub)
                tr = slice(part * sub, (part + 1) * sub)
                run = None
                for kt in range(n_kt):
                    cc = slice(kt * TQ, (kt + 1) * TQ)
                    s = s_scr[rr, cc] + tab_ref[0, g, jnp.maximum(qi - (kt0 + kt) + 1, 0), tr, :]
                    s_scr[rr, cc] = s
                    run = s if run is None else jnp.maximum(run, s)
                mb = jnp.broadcast_to(jnp.max(run, axis=-1, keepdims=True), (sub, TQ))
                for kt in range(n_kt):
                    cc = slice(kt * TQ, (kt + 1) * TQ)
                    p_scr[rr, cc] = jnp.exp(s_scr[rr, cc] - mb).astype(BF16)
            accs.append(_mm(p_scr[hf * hrows:(hf + 1) * hrows, 0:width], v_slab))
        acc = jnp.concatenate(accs, axis=0)
        return pick(acc) / jnp.maximum(other(acc), 1e-30)

    n_tiles = kaug_scr.shape[0] // TQ
    n_win = min(WINDOW // TQ + 1, n_tiles)
    w0 = jnp.maximum(qi - (n_win - 1), 0)
    w_rows = pl.ds(pl.multiple_of(w0 * TQ, TQ), n_win * TQ)
    o_w = attend(qpad, kwin_scr[w_rows, :], vwin_scr[w_rows, :], tabw_ref, w0, n_win)

    n_c = kc_ref.shape[1]
    bias_ct = jnp.concatenate([tabc_ref[0, g] for g in range(Q_PER_KV)], axis=1)
    s_ct = _nt(kc_ref[0].astype(BF16), qpad) + bias_ct
    e_ct = jnp.where(bias_ct > 0.5 * NEG, jnp.exp(s_ct - jnp.max(s_ct, axis=0, keepdims=True)), 0.0)
    p_ct = e_ct / jnp.maximum(jnp.sum(e_ct, axis=0, keepdims=True), 1e-30)
    o_c = lax.dot_general(p_ct.astype(BF16), vc_ref[0].astype(BF16), (((0,), (0,)), ((), ())),
                          preferred_element_type=F32)

    imp_c = sum(p_ct[:, g * TQ:(g + 1) * TQ] for g in range(Q_PER_KV))
    nr = -(-n_sel // 8) * 8
    pair = (lax.broadcasted_iota(jnp.int32, (nr, n_c), 0)
            == lax.broadcasted_iota(jnp.int32, (nr, n_c), 1) // (SEL_BLOCK // CMP_BLOCK)).astype(BF16)
    c_hi = imp_c.astype(BF16)
    c_mid = (imp_c - c_hi.astype(F32)).astype(BF16)
    c_lo = (imp_c - c_hi.astype(F32) - c_mid.astype(F32)).astype(BF16)
    imp = _mm(pair, c_hi) + _mm(pair, c_mid) + _mm(pair, c_lo)
    t = qi * TQ + lax.broadcasted_iota(jnp.int32, (1, TQ), 1)
    j = lax.broadcasted_iota(jnp.int32, (nr, 1), 0)
    cur = t // SEL_BLOCK
    forced = (j == 0) | (j == cur) | (j == cur - 1)
    score = jnp.where(forced, 1e30, jnp.where(j * SEL_BLOCK <= t, imp, -1e30))
    rank = jnp.zeros((nr, TQ), F32)
    for i in range(n_sel):
        ri = score[i:i + 1, :]
        rank = rank + jnp.where((ri > score) | ((ri == score) & (i < j)), 1.0, 0.0)
    drop_t = jnp.where(rank < SEL_TOPK, 0.0, 1.0)
    drop_t = jnp.concatenate([drop_t, jnp.ones((LANES - nr, TQ), F32)], axis=0)
    drop = drop_t.T.astype(BF16)
    qaug = jnp.concatenate([qpad, jnp.concatenate([drop] * Q_PER_KV, axis=0)], axis=1)

    ext_step = 2
    extents = [min((e + 1) * ext_step, n_tiles) for e in range(-(-n_tiles // ext_step))]
    o_s = lax.switch(qi // ext_step,
                     [functools.partial(lambda n: attend(qaug, kaug_scr[0:n * TQ, :], vaug_scr[0:n * TQ, :],
                                                         tabs_ref, 0, n), n) for n in extents])
    o_c = pick(o_c)

    gates = jax.nn.sigmoid(pltpu.roll(sm_ref[0], SMALL_W - NSA_G_OFF - h * (3 * Q_PER_KV), axis=1))
    outs = []
    for g in range(Q_PER_KV):
        rs = slice(g * TQ, (g + 1) * TQ)
        outs.append(gates[:, 3 * g:3 * g + 1] * o_c[rs] + gates[:, 3 * g + 1:3 * g + 2] * o_s[rs]
                    + gates[:, 3 * g + 2:3 * g + 3] * o_w[rs])
    o_ref[0] = jnp.concatenate(outs, axis=1).astype(BF16)


def _nsa_prompt(qn, paged, win, kc, vc, small, tab_s, tab_w, tab_c):
    b, L = qn.shape[:2]
    assert L % TKC == 0
    n_c = L // CMP_BLOCK
    rows = Q_PER_KV * TQ
    seq = lambda col: pl.BlockSpec((1, L, LANES), lambda h, i, q: (i, 0, col(h)))
    cmp = pl.BlockSpec((1, n_c, LANES), lambda h, i, q: (i, 0, h // 2))
    return pl.pallas_call(
        functools.partial(_nsa_prompt_body, L // SEL_BLOCK),
        out_shape=jax.ShapeDtypeStruct((b, L, D_ATTN), BF16),
        grid=(KV_HEADS, b, L // TQ),
        in_specs=[pl.BlockSpec((1, TQ, KV_W), lambda h, i, q: (i, q, h)),
                  seq(lambda h: 4 + h // 2), seq(lambda h: 6 + h // 2),
                  seq(lambda h: h // 2), seq(lambda h: 2 + h // 2),
                  cmp, cmp,
                  pl.BlockSpec((1, TQ, SMALL_W), lambda h, i, q: (i, q, 0)),
                  pl.BlockSpec((1,) + tab_s.shape[1:], lambda h, i, q: (h, 0, 0, 0, 0)),
                  pl.BlockSpec((1,) + tab_w.shape[1:], lambda h, i, q: (h, 0, 0, 0, 0)),
                  pl.BlockSpec((1, Q_PER_KV, n_c, TQ), lambda h, i, q: (h, 0, 0, q))],
        out_specs=pl.BlockSpec((1, TQ, KV_W), lambda h, i, q: (i, q, h)),
        scratch_shapes=[pltpu.VMEM((L, 2 * LANES), BF16), pltpu.VMEM((L, LANES), BF16),
                        pltpu.VMEM((L, LANES), BF16), pltpu.VMEM((L, LANES), BF16),
                        pltpu.VMEM((rows, L), F32), pltpu.VMEM((rows, L), BF16)],
        compiler_params=_cparams(("arbitrary", "arbitrary", "arbitrary")),
        name="nsa_prompt",
    )(qn, paged, paged, win, win, kc, vc, small, tab_s, tab_w, tab_c)


def _merge_body(x_ref, y_ref, o_ref, mg_ref, wa_ref, wb_ref, wo_ref, out_ref):
    ya = _mm(y_ref[...], wa_ref[...])
    yb = _mm(o_ref[...], wb_ref[...])
    gm = jax.nn.sigmoid(mg_ref[...])
    mix = (gm[:, :D_MODEL] * ya + gm[:, D_MODEL:] * yb).astype(BF16)
    out_ref[...] = x_ref[...] + _mm(mix, wo_ref[...])


def _merge(x2d, y2d, o2d, mg, wa, wb, wo, tm):
    n = x2d.shape[0]
    row = lambda w: pl.BlockSpec((tm, w), lambda i: (i, 0))
    return pl.pallas_call(
        _merge_body,
        out_shape=jax.ShapeDtypeStruct((n, D_MODEL), F32),
        grid=(n // tm,),
        in_specs=[row(D_MODEL), row(D_INNER), row(D_ATTN), row(2 * D_MODEL),
                  _const_spec((D_INNER, D_MODEL)), _const_spec((D_ATTN, D_MODEL)), _const_spec((D_MODEL, D_MODEL))],
        out_specs=row(D_MODEL),
        compiler_params=_cparams(("arbitrary",)),
        name="merge",
    )(x2d, y2d, o2d, mg, wa.astype(BF16), wb.astype(BF16), wo.astype(BF16))


def _layer_prompt(x, lw, f_bias, tm):
    (ffn1_norm, ffn1_w_in, ffn1_w_out, mix_norm, w_packed, conv_w, conv_b, dt_bias, a_log, d_skip, ssm_norm,
     qk_norm, cmp_prep, w_branch_ssm, w_branch_attn, w_out, ffn2_norm, ffn2_w_in, ffn2_w_out) = lw
    b, L = x.shape[:2]
    n = b * L
    tm_wide = 2 * tm if n % (2 * tm) == 0 else tm
    x1 = _ffn(x.reshape(n, D_MODEL), ffn1_norm, ffn1_w_in, ffn1_w_out, tm_wide)
    z, xbc, qn, paged, win, mg, small = _inproj(x1, mix_norm, w_packed, qk_norm, tm)
    r3 = lambda a: a.reshape(b, L, a.shape[-1])
    conv0 = jnp.zeros((b, D_CONV - 1, CONV_DIM), F32)
    ssm0 = jnp.zeros((b, M_HEADS, M_HEADDIM, D_STATE), F32)
    y, new_conv, new_ssm = _ssd(r3(xbc), r3(z), r3(small), conv0, ssm0, conv_w, conv_b, dt_bias, a_log, d_skip,
                                ssm_norm)
    pe, w1, w2 = cmp_prep
    kn2 = jnp.concatenate([qk_norm[1], qk_norm[1]]).reshape(1, LANES)
    kc, vc = _compress_prompt(r3(paged), pe, w1, w2, kn2)
    tab_s, tab_w, tab_c = _prompt_bias_tables(f_bias, L)
    o = _nsa_prompt(r3(qn), r3(paged), r3(win), kc, vc, r3(small), tab_s, tab_w, tab_c)
    x2 = _merge(x1, y.reshape(n, D_INNER), o.reshape(n, D_ATTN), mg, w_branch_ssm, w_branch_attn, w_out, tm_wide)
    x3 = _ffn(x2, ffn2_norm, ffn2_w_in, ffn2_w_out, tm_wide)
    n_keep = min(WINDOW, L)
    new_kv = paged.reshape(b, L, 4, KV_HEADS, HEAD_DIM)
    new_win = r3(win)[:, L - n_keep:].reshape(b, n_keep, 2, KV_HEADS, HEAD_DIM)
    return x3.reshape(b, L, D_MODEL), new_kv, new_win, new_ssm, new_conv


PAGES_PER_STEP = 32


def _sample_qpad(qb):
    ls = qb.shape[0]
    blocks = []
    for hq in range(Q_HEADS):
        h = hq // Q_PER_KV
        parts = []
        if h > 0:
            parts.append(jnp.zeros((ls, h * HEAD_DIM), F32))
        parts.append(qb[:, hq * HEAD_DIM:(hq + 1) * HEAD_DIM])
        if h < KV_HEADS - 1:
            parts.append(jnp.zeros((ls, (KV_HEADS - 1 - h) * HEAD_DIM), F32))
        blocks.append(jnp.concatenate(parts, axis=1))
    return jnp.concatenate(blocks, axis=0).astype(BF16)


def _sample_head_piece(o, hq, ls):
    h = hq // Q_PER_KV
    return o[hq * ls:(hq + 1) * ls, h * HEAD_DIM:(h + 1) * HEAD_DIM]


def _chunk_copies(pt_ref, cache_ref, buf, sem, cols, width, bb, ii, sl, p):
    page = pt_ref[bb, ii * PAGES_PER_STEP + p]
    row0 = pl.multiple_of(p * PAGE_SIZE, PAGE_SIZE)
    return [pltpu.make_async_copy(cache_ref.at[page, :, pl.ds(c0, width)],
                                  buf.at[sl, k, pl.ds(row0, PAGE_SIZE), :], sem.at[sl])
            for k, c0 in enumerate(cols)]


def _stream_chunks(pt_ref, cache_ref, buf, sem, cols, width):
    b = pl.program_id(0)
    i = pl.program_id(1)
    nch = pl.num_programs(1)
    step = b * nch + i
    slot = step % 2

    def run(bb, ii, sl, start):
        def body(p, carry):
            for cp in _chunk_copies(pt_ref, cache_ref, buf, sem, cols, width, bb, ii, sl, p):
                cp.start() if start else cp.wait()
            return carry
        lax.fori_loop(0, PAGES_PER_STEP, body, 0)

    @pl.when(step == 0)
    def _():
        run(b, i, slot, True)

    nxt = step + 1

    @pl.when(nxt < pl.num_programs(0) * nch)
    def _():
        run(nxt // nch, nxt % nch, nxt % 2, True)

    run(b, i, slot, False)
    return slot


def _cmp_sample_body(past_len, pt_ref, cache_ref, q_ref, biasc_ref, pe_ref, w1_ref, w2_ref, kn_ref,
                     oc_ref, selm_ref, buf, sem, kc_s, vc_s):
    i = pl.program_id(1)
    slot = _stream_chunks(pt_ref, cache_ref, buf, sem, [0], 2 * KV_W)
    lanes = lambda k: buf.at[slot, 0, :, pl.ds(k * LANES, LANES)]
    nblk = PAGES_PER_STEP * PAGE_SIZE // CMP_BLOCK
    rows0 = pl.multiple_of(i * nblk, nblk)
    kc_s[pl.ds(rows0, nblk), :] = _head_rms(
        _compress_rows(lanes(0), lanes(1), nblk, pe_ref, w1_ref, w2_ref, 0), kn_ref[...])
    vc_s[pl.ds(rows0, nblk), :] = _compress_rows(lanes(2), lanes(3), nblk, pe_ref, w1_ref, w2_ref, 1)

    @pl.when(i == pl.num_programs(1) - 1)
    def _():
        ls = q_ref.shape[1]
        n_c = kc_s.shape[0]
        s = _nt(_sample_qpad(q_ref[0]), kc_s[...].astype(BF16)) + biasc_ref[...]
        e = jnp.exp(s - jnp.max(s, axis=-1, keepdims=True))
        p = e / jnp.maximum(jnp.sum(e, axis=-1, keepdims=True), 1e-30)
        oc = _mm(p.astype(BF16), vc_s[...].astype(BF16))
        oc_ref[0] = jnp.concatenate([_sample_head_piece(oc, hq, ls) for hq in range(Q_HEADS)], axis=1)
        imp_c = jnp.sum(p.reshape(KV_HEADS, Q_PER_KV, ls, n_c), axis=1).reshape(KV_HEADS * ls, n_c)
        w = selm_ref.shape[2]
        pair = (lax.broadcasted_iota(jnp.int32, (n_c, w), 0) // (SEL_BLOCK // CMP_BLOCK)
                == lax.broadcasted_iota(jnp.int32, (n_c, w), 1)).astype(F32)
        imp = _mm(imp_c, pair, precision=HIGHEST)
        t = past_len + lax.broadcasted_iota(jnp.int32, (KV_HEADS * ls, 1), 0) % ls
        selm_ref[0] = _select_blocks(imp, t, past_len // SEL_BLOCK + 1)


def _nsa_sample_body(n_chunks, pt_ref, cache_ref, q_ref, selm_ref, oc_ref, sm_ref, pnew_ref, cwin_ref, wnew_ref,
                     biass_ref, biasw_ref, biasn_ref, spread_ref, o_ref, buf, sem, m_s, l_s, a_s):
    i = pl.program_id(1)
    slot = _stream_chunks(pt_ref, cache_ref, buf, sem, [2 * KV_W], 2 * KV_W)
    ls = q_ref.shape[1]
    rows = Q_HEADS * ls
    tk = PAGES_PER_STEP * PAGE_SIZE
    qpad = _sample_qpad(q_ref[0])

    @pl.when(i == 0)
    def _():
        m_s[...] = jnp.full(m_s.shape, NEG, F32)
        l_s[...] = jnp.zeros(l_s.shape, F32)
        a_s[...] = jnp.zeros(a_s.shape, F32)

    sel = selm_ref[0]
    w = sel.shape[1]
    sel = jnp.broadcast_to(sel.reshape(KV_HEADS, 1, ls, w), (KV_HEADS, Q_PER_KV, ls, w)).reshape(rows, w)
    bpc = tk // SEL_BLOCK
    sel_c = sel[:, 0:LANES]
    for k in range(1, n_chunks):
        sel_c = jnp.where(i == k, sel[:, k * bpc:k * bpc + LANES], sel_c)
    keep = _mm(sel_c.astype(BF16), spread_ref[...])
    s = _nt(qpad, buf[slot, 0, :, 0:KV_W].astype(BF16)) + biass_ref[...] + jnp.where(keep > 0.5, 0.0, NEG)
    m, l, a = _online_softmax_step(s, (m_s[...], l_s[...], a_s[...]), buf[slot, 0, :, KV_W:2 * KV_W].astype(BF16))
    m_s[...] = m
    l_s[...] = l
    a_s[...] = a

    @pl.when(i == pl.num_programs(1) - 1)
    def _():
        def pad_rows(x):
            return jnp.concatenate([x, jnp.zeros((LANES - ls, x.shape[1]), F32)], axis=0).astype(BF16)

        pn = pnew_ref[0]
        wn = wnew_ref[0]
        s_n = _nt(qpad, pad_rows(pn[:, 2 * KV_W:3 * KV_W])) + biasn_ref[...]
        _, l2, a2 = _online_softmax_step(s_n, (m, l, a), pad_rows(pn[:, 3 * KV_W:4 * KV_W]))
        o_s = a2 / jnp.maximum(l2, 1e-30)
        cw = cwin_ref[0]
        init = (jnp.full((rows, 1), NEG, F32), jnp.zeros((rows, 1), F32), jnp.zeros((rows, KV_W), F32))
        c1 = _online_softmax_step(_nt(qpad, cw[:, 0:KV_W].astype(BF16)) + biasw_ref[...], init,
                                  cw[:, KV_W:2 * KV_W].astype(BF16))
        s_wn = _nt(qpad, pad_rows(wn[:, 0:KV_W])) + biasn_ref[...]
        _, l3, a3 = _online_softmax_step(s_wn, c1, pad_rows(wn[:, KV_W:2 * KV_W]))
        o_w = a3 / jnp.maximum(l3, 1e-30)
        gates = jax.nn.sigmoid(sm_ref[0])
        oc = oc_ref[0]
        outs = []
        for hq in range(Q_HEADS):
            g0 = NSA_G_OFF + 3 * hq
            outs.append(gates[:, g0:g0 + 1] * oc[:, hq * HEAD_DIM:(hq + 1) * HEAD_DIM]
                        + gates[:, g0 + 1:g0 + 2] * _sample_head_piece(o_s, hq, ls)
                        + gates[:, g0 + 2:g0 + 3] * _sample_head_piece(o_w, hq, ls))
        o_ref[0] = jnp.concatenate(outs, axis=1).astype(BF16)


def _sample_bias_tables(rel_bias, past_len, ls, n_buf):
    rows2d = lambda t: t.reshape(Q_HEADS * ls, t.shape[2])
    bias_s = _bias_table(rel_bias, ls, past_len, past_len, 1, -1)
    n_c = past_len // CMP_BLOCK
    bias_c = _bias_table(rel_bias, ls, n_c, past_len - (CMP_BLOCK - 1), 1, -CMP_BLOCK)
    bias_w = _bias_table(rel_bias, ls, n_buf, n_buf, 1, -1, hi=WINDOW)
    bias_n = _bias_table(rel_bias, ls, LANES, 0, 1, -1, ncol_valid=ls)
    return rows2d(bias_s), rows2d(bias_c), rows2d(bias_w), rows2d(bias_n)


def _nsa_sample(qn, paged, win, small, cache_pages, page_table, cache_win, rel_bias, cmp_prep, kn2):
    b, ls = qn.shape[:2]
    n_pages = page_table.shape[1]
    past_len = n_pages * PAGE_SIZE
    n_buf = cache_win.shape[1]
    assert n_pages % PAGES_PER_STEP == 0 and ls <= SEL_BLOCK and ls % 8 == 0 and n_buf == min(WINDOW, past_len)
    nch = n_pages // PAGES_PER_STEP
    tk = PAGES_PER_STEP * PAGE_SIZE
    n_c = past_len // CMP_BLOCK
    n_sel = past_len // SEL_BLOCK + 1
    w_sel = -(-n_sel // LANES) * LANES
    rows = Q_HEADS * ls
    pe, w1, w2 = cmp_prep
    bias_s, bias_c, bias_w, bias_n = _sample_bias_tables(rel_bias, past_len, ls, n_buf)
    tok = lambda wd: pl.BlockSpec((1, ls, wd), lambda i, c, pt: (i, 0, 0))
    const = lambda shape: pl.BlockSpec(shape, lambda i, c, pt: (0,) * len(shape), pipeline_mode=pl.Buffered(1))
    o_c, selm = pl.pallas_call(
        functools.partial(_cmp_sample_body, past_len),
        out_shape=[jax.ShapeDtypeStruct((b, ls, D_ATTN), F32),
                   jax.ShapeDtypeStruct((b, KV_HEADS * ls, w_sel), F32)],
        grid_spec=pltpu.PrefetchScalarGridSpec(
            num_scalar_prefetch=1, grid=(b, nch),
            in_specs=[pl.BlockSpec(memory_space=pl.ANY), tok(D_ATTN), const((rows, n_c)),
                      const(pe.shape), const(w1.shape), const(w2.shape), const((1, LANES))],
            out_specs=[tok(D_ATTN), pl.BlockSpec((1, KV_HEADS * ls, w_sel), lambda i, c, pt: (i, 0, 0))],
            scratch_shapes=[pltpu.VMEM((2, 1, tk, 2 * KV_W), F32), pltpu.SemaphoreType.DMA((2,)),
                            pltpu.VMEM((n_c, KV_W), F32), pltpu.VMEM((n_c, KV_W), F32)]),
        compiler_params=_cparams(("arbitrary", "arbitrary")),
        name="nsa_sample_compress",
    )(page_table, cache_pages, qn, bias_c, pe, w1, w2, kn2)
    assert (nch - 1) * (tk // SEL_BLOCK) + LANES <= w_sel
    spread = (np.arange(LANES)[:, None] == np.arange(tk)[None, :] // SEL_BLOCK).astype(BF16)
    return pl.pallas_call(
        functools.partial(_nsa_sample_body, nch),
        out_shape=jax.ShapeDtypeStruct((b, ls, D_ATTN), BF16),
        grid_spec=pltpu.PrefetchScalarGridSpec(
            num_scalar_prefetch=1, grid=(b, nch),
            in_specs=[pl.BlockSpec(memory_space=pl.ANY), tok(D_ATTN),
                      pl.BlockSpec((1, KV_HEADS * ls, w_sel), lambda i, c, pt: (i, 0, 0)),
                      tok(D_ATTN), tok(SMALL_W), tok(4 * KV_W),
                      pl.BlockSpec((1, n_buf, 2 * KV_W), lambda i, c, pt: (i, 0, 0)), tok(2 * KV_W),
                      pl.BlockSpec((rows, tk), lambda i, c, pt: (0, c)),
                      const((rows, n_buf)), const((rows, LANES)), const((LANES, tk))],
            out_specs=tok(D_ATTN),
            scratch_shapes=[pltpu.VMEM((2, 1, tk, 2 * KV_W), F32), pltpu.SemaphoreType.DMA((2,)),
                            pltpu.VMEM((rows, 1), F32), pltpu.VMEM((rows, 1), F32), pltpu.VMEM((rows, KV_W), F32)]),
        compiler_params=_cparams(("arbitrary", "arbitrary")),
        name="nsa_sample",
    )(page_table, cache_pages, qn, selm, o_c, small, paged, cache_win, win, bias_s, bias_w, bias_n, spread)


def _layer_sample(x, ssm0, conv0, cache_kv_l, page_table, cache_win_l, lw, f_bias):
    (ffn1_norm, ffn1_w_in, ffn1_w_out, mix_norm, w_packed, conv_w, conv_b, dt_bias, a_log, d_skip, ssm_norm,
     qk_norm, cmp_prep, w_branch_ssm, w_branch_attn, w_out, ffn2_norm, ffn2_w_in, ffn2_w_out) = lw
    b, ls = x.shape[:2]
    n = b * ls
    tm = n
    x1 = _ffn(x.reshape(n, D_MODEL), ffn1_norm, ffn1_w_in, ffn1_w_out, tm)
    z, xbc, qn, paged, win, mg, small = _inproj(x1, mix_norm, w_packed, qk_norm, tm)
    r3 = lambda a: a.reshape(b, ls, a.shape[-1])
    y, new_conv, new_ssm = _ssd(r3(xbc), r3(z), r3(small), conv0, ssm0, conv_w, conv_b, dt_bias, a_log, d_skip,
                                ssm_norm)
    kn2 = jnp.concatenate([qk_norm[1], qk_norm[1]]).reshape(1, LANES)
    n_pool = cache_kv_l.shape[0]
    cache_pages = cache_kv_l.reshape(n_pool, PAGE_SIZE, 4 * KV_W)
    n_buf = cache_win_l.shape[1]
    cwin = cache_win_l.reshape(b, n_buf, 2 * KV_W)
    o = _nsa_sample(r3(qn), r3(paged), r3(win), r3(small), cache_pages, page_table, cwin, f_bias, cmp_prep, kn2)
    x2 = _merge(x1, y.reshape(n, D_INNER), o.reshape(n, D_ATTN), mg, w_branch_ssm, w_branch_attn, w_out, tm)
    x3 = _ffn(x2, ffn2_norm, ffn2_w_in, ffn2_w_out, tm)
    n_keep = min(WINDOW, n_buf + ls)
    new_kv = paged.reshape(b, ls, 4, KV_HEADS, HEAD_DIM)
    win_all = jnp.concatenate([cwin, r3(win)], axis=1)
    new_win = win_all[:, n_buf + ls - n_keep:].reshape(b, n_keep, 2, KV_HEADS, HEAD_DIM)
    return x3.reshape(b, ls, D_MODEL), new_kv, new_win, new_ssm, new_conv


def _layer_weights(l, ffn1_norm, ffn1_w_in, ffn1_w_out, mix_norm, w_in_proj, conv_w, conv_b, dt_bias, a_log,
                   d_skip, ssm_norm, qk_norm, cmp_pe, cmp_w1, cmp_w2, w_branch_ssm, w_branch_attn, w_out,
                   ffn2_norm, ffn2_w_in, ffn2_w_out):
    return (ffn1_norm[l], ffn1_w_in[l], ffn1_w_out[l], mix_norm[l], _pack_inproj_weight(w_in_proj[l]), conv_w[l],
            conv_b[l], dt_bias[l], a_log[l], d_skip[l], ssm_norm[l], qk_norm[l],
            _compress_weights(cmp_pe[l], cmp_w1[l], cmp_w2[l]), w_branch_ssm[l], w_branch_attn[l], w_out[l],
            ffn2_norm[l], ffn2_w_in[l], ffn2_w_out[l])


def kernel(x_prompt, x_sample, cache_kv, cache_win, state_ssm, state_conv, page_table, rel_bias, ffn1_norm, ffn1_w_in, ffn1_w_out, mix_norm, w_in_proj, conv_w, conv_b, dt_bias, a_log, d_skip, ssm_norm, qk_norm, cmp_pe, cmp_w1, cmp_w2, w_branch_ssm, w_branch_attn, w_out, ffn2_norm, ffn2_w_in, ffn2_w_out):
    depth = ffn1_norm.shape[0]
    f_bias = rel_bias
    xp, xs = x_prompt, x_sample
    outs_p, outs_s = [], []
    for l in range(depth):
        lw = _layer_weights(l, ffn1_norm, ffn1_w_in, ffn1_w_out, mix_norm, w_in_proj, conv_w, conv_b, dt_bias,
                            a_log, d_skip, ssm_norm, qk_norm, cmp_pe, cmp_w1, cmp_w2, w_branch_ssm,
                            w_branch_attn, w_out, ffn2_norm, ffn2_w_in, ffn2_w_out)
        xp, *rp = _layer_prompt(xp, lw, f_bias, 256)
        outs_p.append(rp)
        xs, *rs = _layer_sample(xs, state_ssm[l], state_conv[l], cache_kv[l], page_table, cache_win[l], lw, f_bias)
        outs_s.append(rs)
    stack = lambda outs, k: jnp.stack([o[k] for o in outs])
    return ((xp, xs) + tuple(stack(outs_p, k) for k in range(4)) + tuple(stack(outs_s, k) for k in range(4)))
```

```python
import functools
import math

import jax
import jax.numpy as jnp
import numpy as np
from jax import lax
from jax.experimental import pallas as pl
from jax.experimental.pallas import tpu as pltpu

F32 = jnp.float32
BF16 = jnp.bfloat16
HIGHEST = lax.Precision.HIGHEST

D_MODEL = 1024
D_FF = 2816
D_INNER = 2048
M_HEADDIM = 64
M_HEADS = 32
M_GROUPS = 4
D_STATE = 128
D_CONV = 4
CONV_DIM = D_INNER + 2 * M_GROUPS * D_STATE
HEAD_DIM = 64
Q_HEADS = 16
KV_HEADS = 4
Q_PER_KV = 4
D_ATTN = Q_HEADS * HEAD_DIM
KV_W = KV_HEADS * HEAD_DIM
CMP_BLOCK = 32
CMP_HIDDEN = 128
SEL_BLOCK = 64
SEL_TOPK = 16
WINDOW = 512
NUM_BUCKETS = 32
MAX_DISTANCE = 2048
PAGE_SIZE = 128
EPS = 1e-6
NEG = -1e30

LANES = 128
SSD_CL = 128
TQ = 128
TOKEN_TILE = 256
VMEM_LIMIT = 56 * 1024 * 1024
SMALL_W = 128
NSA_G_OFF = M_HEADS


def _cparams(sem):
    return pltpu.CompilerParams(dimension_semantics=sem, vmem_limit_bytes=VMEM_LIMIT)


def _const_spec(shape):
    nd = len(shape)
    return pl.BlockSpec(shape, lambda *a: (0,) * nd, pipeline_mode=pl.Buffered(1))


def _nt(a, b, precision=None):
    return lax.dot_general(a, b, (((1,), (1,)), ((), ())), preferred_element_type=F32, precision=precision)


def _mm(a, b, precision=None):
    return jnp.dot(a, b, preferred_element_type=F32, precision=precision)


def _rms_rows(x, g):
    return x * lax.rsqrt(jnp.mean(x * x, axis=-1, keepdims=True) + EPS) * g


def _silu(x):
    return x * jax.nn.sigmoid(x)


def _head_rms(a, gain2):
    lane = lax.broadcasted_iota(jnp.int32, (1, LANES), 1)
    lo_mask = lane < HEAD_DIM
    outs = []
    for c in range(a.shape[1] // LANES):
        blk = a[:, c * LANES:(c + 1) * LANES]
        sq = blk * blk
        lo = jnp.sum(jnp.where(lo_mask, sq, 0.0), axis=-1, keepdims=True)
        hi = jnp.sum(jnp.where(lo_mask, 0.0, sq), axis=-1, keepdims=True)
        r = jnp.where(lo_mask, lax.rsqrt(lo / HEAD_DIM + EPS), lax.rsqrt(hi / HEAD_DIM + EPS))
        outs.append(blk * r * gain2)
    return outs[0] if len(outs) == 1 else jnp.concatenate(outs, axis=1)


def _ffn_body(x_ref, g_ref, wg_ref, wu_ref, wo_ref, o_ref):
    x = x_ref[...]
    hb = _rms_rows(x, g_ref[...]).astype(BF16)
    gate = _mm(hb, wg_ref[...])
    up = _mm(hb, wu_ref[...])
    act = (_silu(gate) * up).astype(BF16)
    o_ref[...] = x + 0.5 * _mm(act, wo_ref[...])


def _ffn(x2d, norm_g, w_in, w_out, tm):
    n = x2d.shape[0]
    wg = w_in[:, :D_FF].astype(BF16)
    wu = w_in[:, D_FF:].astype(BF16)
    wo = w_out.astype(BF16)
    return pl.pallas_call(
        _ffn_body,
        out_shape=jax.ShapeDtypeStruct((n, D_MODEL), F32),
        grid=(n // tm,),
        in_specs=[pl.BlockSpec((tm, D_MODEL), lambda i: (i, 0)),
                  _const_spec((1, D_MODEL)),
                  _const_spec((D_MODEL, D_FF)), _const_spec((D_MODEL, D_FF)), _const_spec((D_FF, D_MODEL))],
        out_specs=pl.BlockSpec((tm, D_MODEL), lambda i: (i, 0)),
        compiler_params=_cparams(("arbitrary",)),
        name="ffn",
    )(x2d, norm_g.reshape(1, D_MODEL), wg, wu, wo)


def _inproj_body(x_ref, g_ref, w_ref, qkn_ref, z_ref, xbc_ref, q_ref, pg_ref, win_ref, mg_ref, sm_ref):
    hb = _rms_rows(x_ref[...], g_ref[...]).astype(BF16)
    o = 0
    z_ref[...] = _mm(hb, w_ref[:, o:o + D_INNER]); o += D_INNER
    xbc_ref[...] = _mm(hb, w_ref[:, o:o + CONV_DIM]); o += CONV_DIM
    q = _mm(hb, w_ref[:, o:o + D_ATTN]); o += D_ATTN
    q_ref[...] = _head_rms(q, qkn_ref[0:1, :] * (HEAD_DIM ** -0.5))
    kv = _mm(hb, w_ref[:, o:o + 6 * KV_W]); o += 6 * KV_W
    pg_ref[:, 0:2 * KV_W] = kv[:, 0:2 * KV_W]
    pg_ref[:, 2 * KV_W:3 * KV_W] = _head_rms(kv[:, 2 * KV_W:3 * KV_W], qkn_ref[2:3, :])
    pg_ref[:, 3 * KV_W:4 * KV_W] = kv[:, 3 * KV_W:4 * KV_W]
    win_ref[:, 0:KV_W] = _head_rms(kv[:, 4 * KV_W:5 * KV_W], qkn_ref[3:4, :])
    win_ref[:, KV_W:2 * KV_W] = kv[:, 5 * KV_W:6 * KV_W]
    mg_ref[...] = _mm(hb, w_ref[:, o:o + 2 * D_MODEL]); o += 2 * D_MODEL
    sm_ref[...] = _mm(hb, w_ref[:, o:o + SMALL_W])


def _pack_inproj_weight(w):
    c = np.cumsum([0, D_INNER, CONV_DIM, M_HEADS, D_ATTN, 6 * KV_W, 3 * Q_HEADS, 2 * D_MODEL])
    z, xbc, dt, q, kv, ng, mg = [w[:, c[i]:c[i + 1]] for i in range(7)]
    pad = jnp.zeros((D_MODEL, SMALL_W - M_HEADS - 3 * Q_HEADS), w.dtype)
    return jnp.concatenate([z, xbc, q, kv, mg, dt, ng, pad], axis=1).astype(BF16)


def _inproj(x2d, norm_g, w_packed, qk_norm, tm):
    n = x2d.shape[0]
    wtot = w_packed.shape[1]
    qkn2 = jnp.concatenate([qk_norm, qk_norm], axis=1)
    widths = (D_INNER, CONV_DIM, D_ATTN, 4 * KV_W, 2 * KV_W, 2 * D_MODEL, SMALL_W)
    return pl.pallas_call(
        _inproj_body,
        out_shape=[jax.ShapeDtypeStruct((n, w), F32) for w in widths],
        grid=(n // tm,),
        in_specs=[pl.BlockSpec((tm, D_MODEL), lambda i: (i, 0)),
                  _const_spec((1, D_MODEL)), _const_spec((D_MODEL, wtot)), _const_spec((4, LANES))],
        out_specs=[pl.BlockSpec((tm, w), lambda i: (i, 0)) for w in widths],
        compiler_params=_cparams(("arbitrary",)),
        name="inproj",
    )(x2d, norm_g.reshape(1, D_MODEL), w_packed, qkn2)


def _split_bf16(v):
    hi = v.astype(BF16)
    lo = (v - hi.astype(F32)).astype(BF16)
    return hi, lo


def _ssd_body(lv, xbc_ref, z_ref, sm_ref, conv0_ref, ssm0_ref, cw_ref, cb_ref, dtb_ref, alog_ref, dsk_ref,
              nw_ref, expand_ref, y_ref, convo_ref, ssmo_ref, xp_ref, st_ref):
    cl = SSD_CL
    c = pl.program_id(1)
    nchunk = pl.num_programs(1)
    hp = M_HEADS // M_GROUPS * M_HEADDIM

    @pl.when(c == 0)
    def _():
        xp_ref[...] = jnp.zeros(xp_ref.shape, F32)
        xp_ref[8 - (D_CONV - 1):8, :] = conv0_ref[0]
        st_ref[...] = ssm0_ref[0].reshape(M_HEADS * M_HEADDIM, D_STATE).T

    xp_ref[8:8 + lv, :] = xbc_ref[0]
    x_all = xp_ref[...]
    conv = cb_ref[...] + x_all[8:8 + cl] * cw_ref[D_CONV - 1:D_CONV, :]
    for k in range(D_CONV - 1):
        conv = conv + pltpu.roll(x_all, D_CONV - 1 - k, axis=0)[8:8 + cl] * cw_ref[k:k + 1, :]
    tail = xp_ref[8 + lv - (D_CONV - 1):8 + lv, :]
    convo_ref[0] = tail
    xp_ref[8 - (D_CONV - 1):8, :] = tail
    xc = _silu(conv)
    xs = xc[:, :D_INNER]

    sm = sm_ref[0]
    z = z_ref[0]
    if lv < cl:
        sm = jnp.concatenate([sm, jnp.zeros((cl - lv, SMALL_W), F32)], axis=0)
        z = jnp.concatenate([z, jnp.zeros((cl - lv, D_INNER), F32)], axis=0)
    xdt_in = sm + dtb_ref[...]
    dt = jnp.maximum(xdt_in, 0.0) + jnp.log1p(jnp.exp(-jnp.abs(xdt_in)))
    row = lax.broadcasted_iota(jnp.int32, (cl, 1), 0)
    dt = jnp.where(row < lv, dt, 0.0)
    a_neg = -jnp.exp(alog_ref[...])
    da = dt * a_neg
    ri = lax.broadcasted_iota(jnp.int32, (cl, cl), 0)
    ci = lax.broadcasted_iota(jnp.int32, (cl, cl), 1)
    tril = ri >= ci
    acs = _mm(tril.astype(F32), da, precision=HIGHEST)
    alast = acs[cl - 1:cl, :]
    eye = (lax.broadcasted_iota(jnp.int32, (LANES, LANES), 0)
           == lax.broadcasted_iota(jnp.int32, (LANES, LANES), 1)).astype(F32)
    acs_t = _nt(eye, acs, precision=HIGHEST)

    expand = expand_ref[...]
    stack =jnp.concatenate([dt, jnp.exp(acs), dt * jnp.exp(alast - acs)], axis=0)
    s_hi, s_lo = _split_bf16(stack)
    ex = _mm(s_hi, expand) + _mm(s_lo, expand)
    dt_e = ex[0:cl]
    es_e = ex[cl:2 * cl]
    w_e = ex[2 * cl:3 * cl]
    xdt = (xs * dt_e).astype(BF16)
    xw = (xs * w_e).astype(BF16)
    chunk_decay = es_e[cl - 1:cl, :]

    lane = lax.broadcasted_iota(jnp.int32, (1, LANES), 1)
    lo_mask = lane < M_HEADDIM
    eye_b = eye.astype(BF16)
    y_parts = []
    for g in range(M_GROUPS):
        bg = xc[:, D_INNER + g * D_STATE:D_INNER + (g + 1) * D_STATE].astype(BF16)
        cg = xc[:, D_INNER + (M_GROUPS + g) * D_STATE:D_INNER + (M_GROUPS + g + 1) * D_STATE].astype(BF16)
        cb = _nt(cg, bg)
        st_g = st_ref[:, g * hp:(g + 1) * hp]
        y_off = _mm(cg, st_g.astype(BF16)) * es_e[:, g * hp:(g + 1) * hp]
        bg_t = _nt(eye_b, bg).astype(BF16)
        st_ref[:, g * hp:(g + 1) * hp] = (st_g * chunk_decay[:, g * hp:(g + 1) * hp]
                                          + _mm(bg_t, xw[:, g * hp:(g + 1) * hp]))
        for jj in range(hp // LANES):
            col0 = g * hp + jj * LANES
            ms = []
            for j in (2 * (col0 // LANES), 2 * (col0 // LANES) + 1):
                diff = acs[:, j:j + 1] - acs_t[j:j + 1, :]
                dec = jnp.exp(jnp.where(tril, diff, -jnp.inf))
                ms.append((cb * dec).astype(BF16))
            x2 = xdt[:, col0:col0 + LANES]
            zero = jnp.zeros_like(x2)
            rhs = jnp.concatenate([jnp.where(lo_mask, x2, zero), jnp.where(lo_mask, zero, x2)], axis=0)
            y_parts.append(_mm(jnp.concatenate(ms, axis=1), rhs) + y_off[:, jj * LANES:(jj + 1) * LANES])
    y = jnp.concatenate(y_parts, axis=1)
    y = (y + dsk_ref[...] * xs) * _silu(z)
    outs = []
    for g in range(M_GROUPS):
        yg = y[:, g * hp:(g + 1) * hp]
        outs.append(yg * lax.rsqrt(jnp.mean(yg * yg, axis=-1, keepdims=True) + EPS))
    y = jnp.concatenate(outs, axis=1) * nw_ref[...]
    y_ref[0] = y[:lv].astype(BF16)

    @pl.when(c == nchunk - 1)
    def _():
        ssmo_ref[0] = st_ref[...].T.reshape(M_HEADS, M_HEADDIM, D_STATE)


def _ssd(xbc, z, small, conv0, ssm0, conv_w, conv_b, dt_bias, a_log, d_skip, ssm_norm):
    b, L = xbc.shape[:2]
    lv = SSD_CL if L % SSD_CL == 0 else L
    assert lv == SSD_CL or (L < SSD_CL and L % 8 == 0)
    nchunk = L // lv
    pad = lambda v: jnp.pad(v.astype(F32), (0, SMALL_W - M_HEADS)).reshape(1, SMALL_W)
    rep = lambda v: jnp.repeat(v.astype(F32), M_HEADDIM).reshape(1, D_INNER)
    return pl.pallas_call(
        functools.partial(_ssd_body, lv),
        out_shape=[jax.ShapeDtypeStruct((b, L, D_INNER), BF16),
                   jax.ShapeDtypeStruct((b, D_CONV - 1, CONV_DIM), F32),
                   jax.ShapeDtypeStruct((b, M_HEADS, M_HEADDIM, D_STATE), F32)],
        grid=(b, nchunk),
        in_specs=[pl.BlockSpec((1, lv, CONV_DIM), lambda i, c: (i, c, 0)),
                  pl.BlockSpec((1, lv, D_INNER), lambda i, c: (i, c, 0)),
                  pl.BlockSpec((1, lv, SMALL_W), lambda i, c: (i, c, 0)),
                  pl.BlockSpec((1, D_CONV - 1, CONV_DIM), lambda i, c: (i, 0, 0)),
                  pl.BlockSpec((1, M_HEADS, M_HEADDIM, D_STATE), lambda i, c: (i, 0, 0, 0)),
                  _const_spec((D_CONV, CONV_DIM)), _const_spec((1, CONV_DIM)),
                  _const_spec((1, SMALL_W)), _const_spec((1, SMALL_W)),
                  _const_spec((1, D_INNER)), _const_spec((1, D_INNER)), _const_spec((LANES, D_INNER))],
        out_specs=[pl.BlockSpec((1, lv, D_INNER), lambda i, c: (i, c, 0)),
                   pl.BlockSpec((1, D_CONV - 1, CONV_DIM), lambda i, c: (i, 0, 0)),
                   pl.BlockSpec((1, M_HEADS, M_HEADDIM, D_STATE), lambda i, c: (i, 0, 0, 0))],
        scratch_shapes=[pltpu.VMEM((8 + SSD_CL, CONV_DIM), F32), pltpu.VMEM((D_STATE, D_INNER), F32)],
        compiler_params=_cparams(("arbitrary", "arbitrary")),
        name="ssd",
    )(xbc, z, small, conv0, ssm0, conv_w, conv_b.reshape(1, CONV_DIM), pad(dt_bias), pad(a_log),
      rep(d_skip), ssm_norm.reshape(1, D_INNER),
      (np.arange(LANES)[:, None] == np.arange(D_INNER)[None, :] // M_HEADDIM).astype(BF16))


NO_LIMIT = 2 ** 30


def _bias_table_body(base, rc, cc, lo, hi, ncol_valid, tab_ref, o_ref):
    _, rb, cb = o_ref.shape
    r = pl.program_id(0) * rb + lax.broadcasted_iota(jnp.int32, (rb, cb), 0)
    c = pl.program_id(1) * cb + lax.broadcasted_iota(jnp.int32, (rb, cb), 1)
    d = base + rc * r + cc * c
    valid = (d >= lo) & (d <= hi) & (c < ncol_valid)
    max_exact = NUM_BUCKETS // 2
    dd = jnp.maximum(d, 0)
    df = jnp.maximum(dd, 1).astype(F32)
    large = max_exact + (jnp.log(df / max_exact) / math.log(MAX_DISTANCE / max_exact)
                         * (NUM_BUCKETS - max_exact)).astype(jnp.int32)
    large = jnp.minimum(large, NUM_BUCKETS - 1)
    bucket = jnp.where(dd < max_exact, dd, large)
    for hq in range(Q_HEADS):
        acc = jnp.zeros((rb, cb), F32)
        for bkt in range(NUM_BUCKETS):
            acc = jnp.where(bucket == bkt, tab_ref[bkt, hq], acc)
        o_ref[hq] = jnp.where(valid, acc, NEG)


def _bias_table(rel_bias, rows, cols, base, rc, cc, lo=0, hi=NO_LIMIT, ncol_valid=NO_LIMIT):
    rb = LANES if rows > LANES and rows % LANES == 0 else rows
    cb = 2048 if cols > 2048 and cols % 2048 == 0 else cols
    return pl.pallas_call(
        functools.partial(_bias_table_body, base, rc, cc, lo, hi, ncol_valid),
        out_shape=jax.ShapeDtypeStruct((Q_HEADS, rows, cols), F32),
        grid=(rows // rb, cols // cb),
        in_specs=[pl.BlockSpec(memory_space=pltpu.SMEM)],
        out_specs=pl.BlockSpec((Q_HEADS, rb, cb), lambda i, j: (0, i, j)),
        compiler_params=_cparams(("arbitrary", "arbitrary")),
        name="bias_table",
    )(rel_bias.astype(F32))


def _prompt_bias_tables(rel_bias, L):
    n_off = L // TQ + 1
    n_off_w = WINDOW // TQ + 3
    tab_s = _bias_table(rel_bias, n_off * TQ, TQ, -TQ, 1, -1)
    tab_w = _bias_table(rel_bias, n_off_w * TQ, TQ, -TQ, 1, -1, hi=WINDOW)
    n_c = L // CMP_BLOCK
    tab_ct = _bias_table(rel_bias, n_c, L, -(CMP_BLOCK - 1), -CMP_BLOCK, 1)
    return (tab_s.reshape(KV_HEADS, Q_PER_KV, n_off, TQ, TQ), tab_w.reshape(KV_HEADS, Q_PER_KV, n_off_w, TQ, TQ),
            tab_ct.reshape(KV_HEADS, Q_PER_KV, n_c, L))


def _compress_weights(cmp_pe, cmp_w1, cmp_w2):
    eye = jnp.eye(2, dtype=F32)
    w1 = jnp.einsum("hg,cjdf->cjhdgf", eye, cmp_w1).reshape(2, CMP_BLOCK // 2, 2 * LANES, 2 * CMP_HIDDEN)
    w2 = jnp.einsum("hg,cfd->chfgd", eye, cmp_w2).reshape(2, 2 * CMP_HIDDEN, LANES)
    pe = jnp.tile(cmp_pe, (1, 1, 2))
    return pe.astype(F32), w1.astype(BF16), w2.astype(BF16)


def _compress_rows(lo_ref, hi_ref, nblk, pe_ref, w1_ref, w2_ref, c):
    acc = jnp.zeros((2 * nblk, 2 * CMP_HIDDEN), F32)
    regroup = lambda ref: pltpu.einshape("(nj)l->(jn)l", ref[0:nblk * CMP_BLOCK, :], j=CMP_BLOCK)
    lo_all = regroup(lo_ref)
    hi_all = regroup(hi_ref)
    for jj in range(CMP_BLOCK // 2):
        cols = []
        for j in (2 * jj, 2 * jj + 1):
            take = slice(j * nblk, (j + 1) * nblk)
            pe_j = pe_ref[c, j:j + 1, :]
            cols.append(jnp.concatenate([lo_all[take] + pe_j, hi_all[take] + pe_j], axis=0))
        acc = acc + _mm(jnp.concatenate(cols, axis=1).astype(BF16), w1_ref[c, jj])
    out = _mm(_silu(acc).astype(BF16), w2_ref[c])
    return jnp.concatenate([out[:nblk], out[nblk:]], axis=1)


def _compress_prompt_body(k0_ref, k1_ref, v0_ref, v1_ref, pe_ref, w1_ref, w2_ref, kn_ref, kc_ref, vc_ref):
    nblk = kc_ref.shape[1]
    kc_ref[0] = _head_rms(_compress_rows(k0_ref.at[0], k1_ref.at[0], nblk, pe_ref, w1_ref, w2_ref, 0), kn_ref[...])
    vc_ref[0] = _compress_rows(v0_ref.at[0], v1_ref.at[0], nblk, pe_ref, w1_ref, w2_ref, 1)


def _compress_prompt(paged, pe, w1, w2, kn2):
    b, L = paged.shape[:2]
    n_c = L // CMP_BLOCK
    col = lambda c: pl.BlockSpec((1, L, LANES), lambda i: (i, 0, c))
    return pl.pallas_call(
        _compress_prompt_body,
        out_shape=[jax.ShapeDtypeStruct((b, n_c, KV_W), F32)] * 2,
        grid=(b,),
        in_specs=[col(0), col(1), col(2), col(3),
                  _const_spec(pe.shape), _const_spec(w1.shape), _const_spec(w2.shape), _const_spec((1, LANES))],
        out_specs=[pl.BlockSpec((1, n_c, KV_W), lambda i: (i, 0, 0))] * 2,
        compiler_params=_cparams(("arbitrary",)),
        name="compress_prompt",
    )(paged, paged, paged, paged, pe, w1, w2, kn2)


def _online_softmax_step(s, carry, v):
    m, l, acc = carry
    m_new = jnp.maximum(m, jnp.max(s, axis=-1, keepdims=True))
    alpha = jnp.exp(m - m_new)
    p = jnp.exp(s - m_new)
    return (m_new, alpha * l + jnp.sum(p, axis=-1, keepdims=True), alpha * acc + _mm(p.astype(BF16), v))


def _select_blocks(imp, t, n_blk):
    big = 1e30
    j = lax.broadcasted_iota(jnp.int32, (1, imp.shape[1]), 1)
    cur = t // SEL_BLOCK
    forced = (j == 0) | (j == cur) | (j == cur - 1)
    causal = j * SEL_BLOCK <= t
    score = jnp.where(forced, big, jnp.where(causal, imp, -big))
    rank = jnp.zeros(imp.shape, F32)
    for i in range(n_blk):
        ci = score[:, i:i + 1]
        ahead = (ci > score) | ((ci == score) & (i < j))
        rank = rank + jnp.where(ahead, 1.0, 0.0)
    return jnp.where(rank < SEL_TOPK, 1.0, 0.0)


def _nsa_prompt_body(n_sel, q_ref, ks_ref, vs_ref, kw_ref, vw_ref, kc_ref, vc_ref, sm_ref, tabs_ref, tabw_ref,
                     tabc_ref, o_ref, kaug_scr, vaug_scr, kwin_scr, vwin_scr, s_scr, p_scr):
    h = pl.program_id(0)
    qi = pl.program_id(2)
    rows = Q_PER_KV * TQ
    lane = lax.broadcasted_iota(jnp.int32, (1, LANES), 1)
    mine = (lane // HEAD_DIM) == (h % 2)

    @pl.when(qi == 0)
    def _():
        seq = kaug_scr.shape[0]
        key = lax.broadcasted_iota(jnp.int32, (seq, LANES), 0)
        blk = lax.broadcasted_iota(jnp.int32, (seq, LANES), 1)
        kaug_scr[:, 0:LANES] = ks_ref[0].astype(BF16)
        kaug_scr[:, LANES:2 * LANES] = jnp.where(blk == key // SEL_BLOCK, NEG, 0.0).astype(BF16)
        vaug_scr[...] = jnp.where(mine, vs_ref[0], 1.0).astype(BF16)
        kwin_scr[...] = kw_ref[0].astype(BF16)
        vwin_scr[...] = jnp.where(mine, vw_ref[0], 1.0).astype(BF16)

    qb = q_ref[0]
    q4 = jnp.concatenate([qb[:, g * HEAD_DIM:(g + 1) * HEAD_DIM] for g in range(Q_PER_KV)], axis=0)
    qpad = jnp.where(mine, jnp.concatenate([q4, q4], axis=1), 0.0).astype(BF16)

    sub = 64
    even = (h % 2) == 0
    pick = lambda a: jnp.where(even, a[:, :HEAD_DIM], a[:, HEAD_DIM:])
    other = lambda a: jnp.where(even, a[:, HEAD_DIM:], a[:, :HEAD_DIM])

    def attend(q_lhs, k_slab, v_slab, tab_ref, kt0, n_kt):
        width = n_kt * TQ
        n_half = 2
        hrows = rows // n_half
        for hf in range(n_half):
            s_scr[hf * hrows:(hf + 1) * hrows, 0:width] = _nt(q_lhs[hf * hrows:(hf + 1) * hrows], k_slab)
        accs = []
        for hf in range(n_half):
            for rb in range(hf * hrows // sub, (hf + 1) * hrows // sub):
                g, part = divmod(rb, TQ // sub)
                rr = slice(rb * sub, (rb + 1) * sub)
                tr = slice(part * sub, (part + 1) * sub)
                run = None
                for kt in range(n_kt):
                    cc = slice(kt * TQ, (kt + 1) * TQ)
                    s = s_scr[rr, cc] + tab_ref[0, g, jnp.maximum(qi - (kt0 + kt) + 1, 0), tr, :]
                    s_scr[rr, cc] = s
                    run = s if run is None else jnp.maximum(run, s)
                mb = jnp.broadcast_to(jnp.max(run, axis=-1, keepdims=True), (sub, TQ))
                for kt in range(n_kt):
                    cc = slice(kt * TQ, (kt + 1) * TQ)
                    p_scr[rr, cc] = jnp.exp(s_scr[rr, cc] - mb).astype(BF16)
            accs.append(_mm(p_scr[hf * hrows:(hf + 1) * hrows, 0:width], v_slab))
        acc = jnp.concatenate(accs, axis=0)
        return pick(acc) / jnp.maximum(other(acc), 1e-30)

    n_tiles = kaug_scr.shape[0] // TQ
    n_win = min(WINDOW // TQ + 1, n_tiles)
    w0 = jnp.maximum(qi - (n_win - 1), 0)
    w_rows = pl.ds(pl.multiple_of(w0 * TQ, TQ), n_win * TQ)
    o_w = attend(qpad, kwin_scr[w_rows, :], vwin_scr[w_rows, :], tabw_ref, w0, n_win)

    n_c = kc_ref.shape[1]
    bias_ct = jnp.concatenate([tabc_ref[0, g] for g in range(Q_PER_KV)], axis=1)
    s_ct = _nt(kc_ref[0].astype(BF16), qpad) + bias_ct
    e_ct = jnp.where(bias_ct > 0.5 * NEG, jnp.exp(s_ct - jnp.max(s_ct, axis=0, keepdims=True)), 0.0)
    p_ct = e_ct / jnp.maximum(jnp.sum(e_ct, axis=0, keepdims=True), 1e-30)
    o_c = lax.dot_general(p_ct.astype(BF16), vc_ref[0].astype(BF16), (((0,), (0,)), ((), ())),
                          preferred_element_type=F32)

    imp_c = sum(p_ct[:, g * TQ:(g + 1) * TQ] for g in range(Q_PER_KV))
    nr = -(-n_sel // 8) * 8
    pair = (lax.broadcasted_iota(jnp.int32, (nr, n_c), 0)
            == lax.broadcasted_iota(jnp.int32, (nr, n_c), 1) // (SEL_BLOCK // CMP_BLOCK)).astype(BF16)
    c_hi = imp_c.astype(BF16)
    c_mid = (imp_c - c_hi.astype(F32)).astype(BF16)
    c_lo = (imp_c - c_hi.astype(F32) - c_mid.astype(F32)).astype(BF16)
    imp = _mm(pair, c_hi) + _mm(pair, c_mid) + _mm(pair, c_lo)
    t = qi * TQ + lax.broadcasted_iota(jnp.int32, (1, TQ), 1)
    j = lax.broadcasted_iota(jnp.int32, (nr, 1), 0)
    cur = t // SEL_BLOCK
    forced = (j == 0) | (j == cur) | (j == cur - 1)
    score = jnp.where(forced, 1e30, jnp.where(j * SEL_BLOCK <= t, imp, -1e30))
    rank = jnp.zeros((nr, TQ), F32)
    for i in range(n_sel):
        ri = score[i:i + 1, :]
        rank = rank + jnp.where((ri > score) | ((ri == score) & (i < j)), 1.0, 0.0)
    drop_t = jnp.where(rank < SEL_TOPK, 0.0, 1.0)
    drop_t = jnp.concatenate([drop_t, jnp.ones((LANES - nr, TQ), F32)], axis=0)
    drop = drop_t.T.astype(BF16)
    qaug = jnp.concatenate([qpad, jnp.concatenate([drop] * Q_PER_KV, axis=0)], axis=1)

    ext_step = 2
    extents = [min((e + 1) * ext_step, n_tiles) for e in range(-(-n_tiles // ext_step))]
    o_s = lax.switch(qi // ext_step,
                     [functools.partial(lambda n: attend(qaug, kaug_scr[0:n * TQ, :], vaug_scr[0:n * TQ, :],
                                                         tabs_ref, 0, n), n) for n in extents])
    o_c = pick(o_c)

    gates = jax.nn.sigmoid(pltpu.roll(sm_ref[0], SMALL_W - NSA_G_OFF - h * (3 * Q_PER_KV), axis=1))
    outs = []
    for g in range(Q_PER_KV):
        rs = slice(g * TQ, (g + 1) * TQ)
        outs.append(gates[:, 3 * g:3 * g + 1] * o_c[rs] + gates[:, 3 * g + 1:3 * g + 2] * o_s[rs]
                    + gates[:, 3 * g + 2:3 * g + 3] * o_w[rs])
    o_ref[0] = jnp.concatenate(outs, axis=1).astype(BF16)


def _nsa_prompt(qn, paged, win, kc, vc, small, tab_s, tab_w, tab_c):
    b, L = qn.shape[:2]
    assert L % TQ == 0
    n_c = L // CMP_BLOCK
    rows = Q_PER_KV * TQ
    seq = lambda col: pl.BlockSpec((1, L, LANES), lambda h, i, q: (i, 0, col(h)))
    cmp = pl.BlockSpec((1, n_c, LANES), lambda h, i, q: (i, 0, h // 2))
    return pl.pallas_call(
        functools.partial(_nsa_prompt_body, L // SEL_BLOCK),
        out_shape=jax.ShapeDtypeStruct((b, L, D_ATTN), BF16),
        grid=(KV_HEADS, b, L // TQ),
        in_specs=[pl.BlockSpec((1, TQ, KV_W), lambda h, i, q: (i, q, h)),
                  seq(lambda h: 4 + h // 2), seq(lambda h: 6 + h // 2),
                  seq(lambda h: h // 2), seq(lambda h: 2 + h // 2),
                  cmp, cmp,
                  pl.BlockSpec((1, TQ, SMALL_W), lambda h, i, q: (i, q, 0)),
                  pl.BlockSpec((1,) + tab_s.shape[1:], lambda h, i, q: (h, 0, 0, 0, 0)),
                  pl.BlockSpec((1,) + tab_w.shape[1:], lambda h, i, q: (h, 0, 0, 0, 0)),
                  pl.BlockSpec((1, Q_PER_KV, n_c, TQ), lambda h, i, q: (h, 0, 0, q))],
        out_specs=pl.BlockSpec((1, TQ, KV_W), lambda h, i, q: (i, q, h)),
        scratch_shapes=[pltpu.VMEM((L, 2 * LANES), BF16), pltpu.VMEM((L, LANES), BF16),
                        pltpu.VMEM((L, LANES), BF16), pltpu.VMEM((L, LANES), BF16),
                        pltpu.VMEM((rows, L), F32), pltpu.VMEM((rows, L), BF16)],
        compiler_params=_cparams(("arbitrary", "arbitrary", "arbitrary")),
        name="nsa_prompt",
    )(qn, paged, paged, win, win, kc, vc, small, tab_s, tab_w, tab_c)


def _merge_body(x_ref, y_ref, o_ref, mg_ref, wa_ref, wb_ref, wo_ref, out_ref):
    ya = _mm(y_ref[...], wa_ref[...])
    yb = _mm(o_ref[...], wb_ref[...])
    gm = jax.nn.sigmoid(mg_ref[...])
    mix = (gm[:, :D_MODEL] * ya + gm[:, D_MODEL:] * yb).astype(BF16)
    out_ref[...] = x_ref[...] + _mm(mix, wo_ref[...])


def _merge(x2d, y2d, o2d, mg, wa, wb, wo, tm):
    n = x2d.shape[0]
    row = lambda w: pl.BlockSpec((tm, w), lambda i: (i, 0))
    return pl.pallas_call(
        _merge_body,
        out_shape=jax.ShapeDtypeStruct((n, D_MODEL), F32),
        grid=(n // tm,),
        in_specs=[row(D_MODEL), row(D_INNER), row(D_ATTN), row(2 * D_MODEL),
                  _const_spec((D_INNER, D_MODEL)), _const_spec((D_ATTN, D_MODEL)), _const_spec((D_MODEL, D_MODEL))],
        out_specs=row(D_MODEL),
        compiler_params=_cparams(("arbitrary",)),
        name="merge",
    )(x2d, y2d, o2d, mg, wa.astype(BF16), wb.astype(BF16), wo.astype(BF16))


def _layer_prompt(x, lw, rel_bias, tm):
    (ffn1_norm, ffn1_w_in, ffn1_w_out, mix_norm, w_packed, conv_w, conv_b, dt_bias, a_log, d_skip, ssm_norm,
     qk_norm, cmp_prep, w_branch_ssm, w_branch_attn, w_out, ffn2_norm, ffn2_w_in, ffn2_w_out) = lw
    b, L = x.shape[:2]
    n = b * L
    tm_wide = 2 * tm if n % (2 * tm) == 0 else tm
    x1 = _ffn(x.reshape(n, D_MODEL), ffn1_norm, ffn1_w_in, ffn1_w_out, tm_wide)
    z, xbc, qn, paged, win, mg, small = _inproj(x1, mix_norm, w_packed, qk_norm, tm)
    r3 = lambda a: a.reshape(b, L, a.shape[-1])
    conv0 = jnp.zeros((b, D_CONV - 1, CONV_DIM), F32)
    ssm0 = jnp.zeros((b, M_HEADS, M_HEADDIM, D_STATE), F32)
    y, new_conv, new_ssm = _ssd(r3(xbc), r3(z), r3(small), conv0, ssm0, conv_w, conv_b, dt_bias, a_log, d_skip,
                                ssm_norm)
    pe, w1, w2 = cmp_prep
    kn2 = jnp.concatenate([qk_norm[1], qk_norm[1]]).reshape(1, LANES)
    kc, vc = _compress_prompt(r3(paged), pe, w1, w2, kn2)
    tab_s, tab_w, tab_c = _prompt_bias_tables(rel_bias, L)
    o = _nsa_prompt(r3(qn), r3(paged), r3(win), kc, vc, r3(small), tab_s, tab_w, tab_c)
    x2 = _merge(x1, y.reshape(n, D_INNER), o.reshape(n, D_ATTN), mg, w_branch_ssm, w_branch_attn, w_out, tm_wide)
    x3 = _ffn(x2, ffn2_norm, ffn2_w_in, ffn2_w_out, tm_wide)
    n_keep = min(WINDOW, L)
    new_kv = paged.reshape(b, L, 4, KV_HEADS, HEAD_DIM)
    new_win = r3(win)[:, L - n_keep:].reshape(b, n_keep, 2, KV_HEADS, HEAD_DIM)
    return x3.reshape(b, L, D_MODEL), new_kv, new_win, new_ssm, new_conv


PAGES_PER_STEP = 32


def _sample_qpad(qb):
    ls = qb.shape[0]
    blocks = []
    for hq in range(Q_HEADS):
        h = hq // Q_PER_KV
        parts = []
        if h > 0:
            parts.append(jnp.zeros((ls, h * HEAD_DIM), F32))
        parts.append(qb[:, hq * HEAD_DIM:(hq + 1) * HEAD_DIM])
        if h < KV_HEADS - 1:
            parts.append(jnp.zeros((ls, (KV_HEADS - 1 - h) * HEAD_DIM), F32))
        blocks.append(jnp.concatenate(parts, axis=1))
    return jnp.concatenate(blocks, axis=0).astype(BF16)


def _sample_head_piece(o, hq, ls):
    h = hq // Q_PER_KV
    return o[hq * ls:(hq + 1) * ls, h * HEAD_DIM:(h + 1) * HEAD_DIM]


def _chunk_copies(pt_ref, cache_ref, buf, sem, cols, width, bb, ii, sl, p):
    page = pt_ref[bb, ii * PAGES_PER_STEP + p]
    row0 = pl.multiple_of(p * PAGE_SIZE, PAGE_SIZE)
    return [pltpu.make_async_copy(cache_ref.at[page, :, pl.ds(c0, width)],
                                  buf.at[sl, k, pl.ds(row0, PAGE_SIZE), :], sem.at[sl])
            for k, c0 in enumerate(cols)]


def _stream_chunks(pt_ref, cache_ref, buf, sem, cols, width):
    b = pl.program_id(0)
    i = pl.program_id(1)
    nch = pl.num_programs(1)
    step = b * nch + i
    slot = step % 2

    def run(bb, ii, sl, start):
        def body(p, carry):
            for cp in _chunk_copies(pt_ref, cache_ref, buf, sem, cols, width, bb, ii, sl, p):
                cp.start() if start else cp.wait()
            return carry
        lax.fori_loop(0, PAGES_PER_STEP, body, 0)

    @pl.when(step == 0)
    def _():
        run(b, i, slot, True)

    nxt = step + 1

    @pl.when(nxt < pl.num_programs(0) * nch)
    def _():
        run(nxt // nch, nxt % nch, nxt % 2, True)

    run(b, i, slot, False)
    return slot


def _cmp_sample_body(past_len, pt_ref, cache_ref, q_ref, biasc_ref, pe_ref, w1_ref, w2_ref, kn_ref,
                     oc_ref, selm_ref, buf, sem, kc_s, vc_s):
    i = pl.program_id(1)
    slot = _stream_chunks(pt_ref, cache_ref, buf, sem, [0], 2 * KV_W)
    lanes = lambda k: buf.at[slot, 0, :, pl.ds(k * LANES, LANES)]
    nblk = PAGES_PER_STEP * PAGE_SIZE // CMP_BLOCK
    rows0 = pl.multiple_of(i * nblk, nblk)
    kc_s[pl.ds(rows0, nblk), :] = _head_rms(
        _compress_rows(lanes(0), lanes(1), nblk, pe_ref, w1_ref, w2_ref, 0), kn_ref[...])
    vc_s[pl.ds(rows0, nblk), :] = _compress_rows(lanes(2), lanes(3), nblk, pe_ref, w1_ref, w2_ref, 1)

    @pl.when(i == pl.num_programs(1) - 1)
    def _():
        ls = q_ref.shape[1]
        n_c = kc_s.shape[0]
        s = _nt(_sample_qpad(q_ref[0]), kc_s[...].astype(BF16)) + biasc_ref[...]
        e = jnp.exp(s - jnp.max(s, axis=-1, keepdims=True))
        p = e / jnp.maximum(jnp.sum(e, axis=-1, keepdims=True), 1e-30)
        oc = _mm(p.astype(BF16), vc_s[...].astype(BF16))
        oc_ref[0] = jnp.concatenate([_sample_head_piece(oc, hq, ls) for hq in range(Q_HEADS)], axis=1)
        imp_c = jnp.sum(p.reshape(KV_HEADS, Q_PER_KV, ls, n_c), axis=1).reshape(KV_HEADS * ls, n_c)
        w = selm_ref.shape[2]
        pair = (lax.broadcasted_iota(jnp.int32, (n_c, w), 0) // (SEL_BLOCK // CMP_BLOCK)
                == lax.broadcasted_iota(jnp.int32, (n_c, w), 1)).astype(F32)
        imp = _mm(imp_c, pair, precision=HIGHEST)
        t = past_len + lax.broadcasted_iota(jnp.int32, (KV_HEADS * ls, 1), 0) % ls
        selm_ref[0] = _select_blocks(imp, t, past_len // SEL_BLOCK + 1)


def _nsa_sample_body(n_chunks, pt_ref, cache_ref, q_ref, selm_ref, oc_ref, sm_ref, pnew_ref, cwin_ref, wnew_ref,
                     biass_ref, biasw_ref, biasn_ref, spread_ref, o_ref, buf, sem, m_s, l_s, a_s):
    i = pl.program_id(1)
    slot = _stream_chunks(pt_ref, cache_ref, buf, sem, [2 * KV_W], 2 * KV_W)
    ls = q_ref.shape[1]
    rows = Q_HEADS * ls
    tk = PAGES_PER_STEP * PAGE_SIZE
    qpad = _sample_qpad(q_ref[0])

    @pl.when(i == 0)
    def _():
        m_s[...] = jnp.full(m_s.shape, NEG, F32)
        l_s[...] = jnp.zeros(l_s.shape, F32)
        a_s[...] = jnp.zeros(a_s.shape, F32)

    sel = selm_ref[0]
    w = sel.shape[1]
    sel = jnp.broadcast_to(sel.reshape(KV_HEADS, 1, ls, w), (KV_HEADS, Q_PER_KV, ls, w)).reshape(rows, w)
    bpc = tk // SEL_BLOCK
    sel_c = sel[:, 0:LANES]
    for k in range(1, n_chunks):
        sel_c = jnp.where(i == k, sel[:, k * bpc:k * bpc + LANES], sel_c)
    keep = _mm(sel_c.astype(BF16), spread_ref[...])
    s = _nt(qpad, buf[slot, 0, :, 0:KV_W].astype(BF16)) + biass_ref[...] + jnp.where(keep > 0.5, 0.0, NEG)
    m, l, a = _online_softmax_step(s, (m_s[...], l_s[...], a_s[...]), buf[slot, 0, :, KV_W:2 * KV_W].astype(BF16))
    m_s[...] = m
    l_s[...] = l
    a_s[...] = a

    @pl.when(i == pl.num_programs(1) - 1)
    def _():
        def pad_rows(x):
            return jnp.concatenate([x, jnp.zeros((LANES - ls, x.shape[1]), F32)], axis=0).astype(BF16)

        pn = pnew_ref[0]
        wn = wnew_ref[0]
        s_n = _nt(qpad, pad_rows(pn[:, 2 * KV_W:3 * KV_W])) + biasn_ref[...]
        _, l2, a2 = _online_softmax_step(s_n, (m, l, a), pad_rows(pn[:, 3 * KV_W:4 * KV_W]))
        o_s = a2 / jnp.maximum(l2, 1e-30)
        cw = cwin_ref[0]
        init = (jnp.full((rows, 1), NEG, F32), jnp.zeros((rows, 1), F32), jnp.zeros((rows, KV_W), F32))
        c1 = _online_softmax_step(_nt(qpad, cw[:, 0:KV_W].astype(BF16)) + biasw_ref[...], init,
                                  cw[:, KV_W:2 * KV_W].astype(BF16))
        s_wn = _nt(qpad, pad_rows(wn[:, 0:KV_W])) + biasn_ref[...]
        _, l3, a3 = _online_softmax_step(s_wn, c1, pad_rows(wn[:, KV_W:2 * KV_W]))
        o_w = a3 / jnp.maximum(l3, 1e-30)
        gates = jax.nn.sigmoid(sm_ref[0])
        oc = oc_ref[0]
        outs = []
        for hq in range(Q_HEADS):
            g0 = NSA_G_OFF + 3 * hq
            outs.append(gates[:, g0:g0 + 1] * oc[:, hq * HEAD_DIM:(hq + 1) * HEAD_DIM]
                        + gates[:, g0 + 1:g0 + 2] * _sample_head_piece(o_s, hq, ls)
                        + gates[:, g0 + 2:g0 + 3] * _sample_head_piece(o_w, hq, ls))
        o_ref[0] = jnp.concatenate(outs, axis=1).astype(BF16)


def _sample_bias_tables(rel_bias, past_len, ls, n_buf):
    rows2d = lambda t: t.reshape(Q_HEADS * ls, t.shape[2])
    bias_s = _bias_table(rel_bias, ls, past_len, past_len, 1, -1)
    n_c = past_len // CMP_BLOCK
    bias_c = _bias_table(rel_bias, ls, n_c, past_len - (CMP_BLOCK - 1), 1, -CMP_BLOCK)
    bias_w = _bias_table(rel_bias, ls, n_buf, n_buf, 1, -1, hi=WINDOW)
    bias_n = _bias_table(rel_bias, ls, LANES, 0, 1, -1, ncol_valid=ls)
    return rows2d(bias_s), rows2d(bias_c), rows2d(bias_w), rows2d(bias_n)


def _nsa_sample(qn, paged, win, small, cache_pages, page_table, cache_win, rel_bias, cmp_prep, kn2):
    b, ls = qn.shape[:2]
    n_pages = page_table.shape[1]
    past_len = n_pages * PAGE_SIZE
    n_buf = cache_win.shape[1]
    assert n_pages % PAGES_PER_STEP == 0 and ls <= SEL_BLOCK and ls % 8 == 0 and n_buf == min(WINDOW, past_len)
    nch = n_pages // PAGES_PER_STEP
    tk = PAGES_PER_STEP * PAGE_SIZE
    n_c = past_len // CMP_BLOCK
    n_sel = past_len // SEL_BLOCK + 1
    w_sel = -(-n_sel // LANES) * LANES
    rows = Q_HEADS * ls
    pe, w1, w2 = cmp_prep
    bias_s, bias_c, bias_w, bias_n = _sample_bias_tables(rel_bias, past_len, ls, n_buf)
    tok = lambda wd: pl.BlockSpec((1, ls, wd), lambda i, c, pt: (i, 0, 0))
    const = lambda shape: pl.BlockSpec(shape, lambda i, c, pt: (0,) * len(shape), pipeline_mode=pl.Buffered(1))
    o_c, selm = pl.pallas_call(
        functools.partial(_cmp_sample_body, past_len),
        out_shape=[jax.ShapeDtypeStruct((b, ls, D_ATTN), F32),
                   jax.ShapeDtypeStruct((b, KV_HEADS * ls, w_sel), F32)],
        grid_spec=pltpu.PrefetchScalarGridSpec(
            num_scalar_prefetch=1, grid=(b, nch),
            in_specs=[pl.BlockSpec(memory_space=pl.ANY), tok(D_ATTN), const((rows, n_c)),
                      const(pe.shape), const(w1.shape), const(w2.shape), const((1, LANES))],
            out_specs=[tok(D_ATTN), pl.BlockSpec((1, KV_HEADS * ls, w_sel), lambda i, c, pt: (i, 0, 0))],
            scratch_shapes=[pltpu.VMEM((2, 1, tk, 2 * KV_W), F32), pltpu.SemaphoreType.DMA((2,)),
                            pltpu.VMEM((n_c, KV_W), F32), pltpu.VMEM((n_c, KV_W), F32)]),
        compiler_params=_cparams(("arbitrary", "arbitrary")),
        name="nsa_sample_compress",
    )(page_table, cache_pages, qn, bias_c, pe, w1, w2, kn2)
    assert (nch - 1) * (tk // SEL_BLOCK) + LANES <= w_sel
    spread = (np.arange(LANES)[:, None] == np.arange(tk)[None, :] // SEL_BLOCK).astype(BF16)
    return pl.pallas_call(
        functools.partial(_nsa_sample_body, nch),
        out_shape=jax.ShapeDtypeStruct((b, ls, D_ATTN), BF16),
        grid_spec=pltpu.PrefetchScalarGridSpec(
            num_scalar_prefetch=1, grid=(b, nch),
            in_specs=[pl.BlockSpec(memory_space=pl.ANY), tok(D_ATTN),
                      pl.BlockSpec((1, KV_HEADS * ls, w_sel), lambda i, c, pt: (i, 0, 0)),
                      tok(D_ATTN), tok(SMALL_W), tok(4 * KV_W),
                      pl.BlockSpec((1, n_buf, 2 * KV_W), lambda i, c, pt: (i, 0, 0)), tok(2 * KV_W),
                      pl.BlockSpec((rows, tk), lambda i, c, pt: (0, c)),
                      const((rows, n_buf)), const((rows, LANES)), const((LANES, tk))],
            out_specs=tok(D_ATTN),
            scratch_shapes=[pltpu.VMEM((2, 1, tk, 2 * KV_W), F32), pltpu.SemaphoreType.DMA((2,)),
                            pltpu.VMEM((rows, 1), F32), pltpu.VMEM((rows, 1), F32), pltpu.VMEM((rows, KV_W), F32)]),
        compiler_params=_cparams(("arbitrary", "arbitrary")),
        name="nsa_sample",
    )(page_table, cache_pages, qn, selm, o_c, small, paged, cache_win, win, bias_s, bias_w, bias_n, spread)


def _layer_sample(x, ssm0, conv0, cache_kv_l, page_table, cache_win_l, lw, rel_bias):
    (ffn1_norm, ffn1_w_in, ffn1_w_out, mix_norm, w_packed, conv_w, conv_b, dt_bias, a_log, d_skip, ssm_norm,
     qk_norm, cmp_prep, w_branch_ssm, w_branch_attn, w_out, ffn2_norm, ffn2_w_in, ffn2_w_out) = lw
    b, ls = x.shape[:2]
    n = b * ls
    tm = n
    x1 = _ffn(x.reshape(n, D_MODEL), ffn1_norm, ffn1_w_in, ffn1_w_out, tm)
    z, xbc, qn, paged, win, mg, small = _inproj(x1, mix_norm, w_packed, qk_norm, tm)
    r3 = lambda a: a.reshape(b, ls, a.shape[-1])
    y, new_conv, new_ssm = _ssd(r3(xbc), r3(z), r3(small), conv0, ssm0, conv_w, conv_b, dt_bias, a_log, d_skip,
                                ssm_norm)
    kn2 = jnp.concatenate([qk_norm[1], qk_norm[1]]).reshape(1, LANES)
    n_pool = cache_kv_l.shape[0]
    cache_pages = cache_kv_l.reshape(n_pool, PAGE_SIZE, 4 * KV_W)
    n_buf = cache_win_l.shape[1]
    cwin = cache_win_l.reshape(b, n_buf, 2 * KV_W)
    o = _nsa_sample(r3(qn), r3(paged), r3(win), r3(small), cache_pages, page_table, cwin, rel_bias, cmp_prep, kn2)
    x2 = _merge(x1, y.reshape(n, D_INNER), o.reshape(n, D_ATTN), mg, w_branch_ssm, w_branch_attn, w_out, tm)
    x3 = _ffn(x2, ffn2_norm, ffn2_w_in, ffn2_w_out, tm)
    n_keep = min(WINDOW, n_buf + ls)
    new_kv = paged.reshape(b, ls, 4, KV_HEADS, HEAD_DIM)
    win_all = jnp.concatenate([cwin, r3(win)], axis=1)
    new_win = win_all[:, n_buf + ls - n_keep:].reshape(b, n_keep, 2, KV_HEADS, HEAD_DIM)
    return x3.reshape(b, ls, D_MODEL), new_kv, new_win, new_ssm, new_conv


def _layer_weights(l, ffn1_norm, ffn1_w_in, ffn1_w_out, mix_norm, w_in_proj, conv_w, conv_b, dt_bias, a_log,
                   d_skip, ssm_norm, qk_norm, cmp_pe, cmp_w1, cmp_w2, w_branch_ssm, w_branch_attn, w_out,
                   ffn2_norm, ffn2_w_in, ffn2_w_out):
    return (ffn1_norm[l], ffn1_w_in[l], ffn1_w_out[l], mix_norm[l], _pack_inproj_weight(w_in_proj[l]), conv_w[l],
            conv_b[l], dt_bias[l], a_log[l], d_skip[l], ssm_norm[l], qk_norm[l],
            _compress_weights(cmp_pe[l], cmp_w1[l], cmp_w2[l]), w_branch_ssm[l], w_branch_attn[l], w_out[l],
            ffn2_norm[l], ffn2_w_in[l], ffn2_w_out[l])


def kernel(x_prompt, x_sample, cache_kv, cache_win, state_ssm, state_conv, page_table, rel_bias, ffn1_norm, ffn1_w_in, ffn1_w_out, mix_norm, w_in_proj, conv_w, conv_b, dt_bias, a_log, d_skip, ssm_norm, qk_norm, cmp_pe, cmp_w1, cmp_w2, w_branch_ssm, w_branch_attn, w_out, ffn2_norm, ffn2_w_in, ffn2_w_out):
    depth = ffn1_norm.shape[0]
    xp, xs = x_prompt, x_sample
    outs_p, outs_s = [], []
    for l in range(depth):
        lw = _layer_weights(l, ffn1_norm, ffn1_w_in, ffn1_w_out, mix_norm, w_in_proj, conv_w, conv_b, dt_bias,
                            a_log, d_skip, ssm_norm, qk_norm, cmp_pe, cmp_w1, cmp_w2, w_branch_ssm,
                            w_branch_attn, w_out, ffn2_norm, ffn2_w_in, ffn2_w_out)
        xp, *rp = _layer_prompt(xp, lw, rel_bias, TOKEN_TILE)
        outs_p.append(rp)
        xs, *rs = _layer_sample(xs, state_ssm[l], state_conv[l], cache_kv[l], page_table, cache_win[l], lw, rel_bias)
        outs_s.append(rs)
    stack = lambda outs, k: jnp.stack([o[k] for o in outs])
    return ((xp, xs) + tuple(stack(outs_p, k) for k in range(4)) + tuple(stack(outs_s, k) for k in range(4)))
```

```python
import functools
import math

import jax
import jax.numpy as jnp
import numpy as np
from jax import lax
from jax.experimental import pallas as pl
from jax.experimental.pallas import tpu as pltpu

F32 = jnp.float32
BF16 = jnp.bfloat16
HIGHEST = lax.Precision.HIGHEST

D_MODEL = 1024
D_FF = 2816
D_INNER = 2048
M_HEADDIM = 64
M_HEADS = 32
M_GROUPS = 4
D_STATE = 128
D_CONV = 4
CONV_DIM = D_INNER + 2 * M_GROUPS * D_STATE
HEAD_DIM = 64
Q_HEADS = 16
KV_HEADS = 4
Q_PER_KV = 4
D_ATTN = Q_HEADS * HEAD_DIM
KV_W = KV_HEADS * HEAD_DIM
CMP_BLOCK = 32
CMP_HIDDEN = 128
SEL_BLOCK = 64
SEL_TOPK = 16
WINDOW = 512
NUM_BUCKETS = 32
MAX_DISTANCE = 2048
PAGE_SIZE = 128
EPS = 1e-6
NEG = -1e30

LANES = 128
SSD_CL = 128
TQ = 128
TOKEN_TILE = 256
VMEM_LIMIT = 56 * 1024 * 1024
SMALL_W = 128
NSA_G_OFF = M_HEADS


def _cparams(sem):
    return pltpu.CompilerParams(dimension_semantics=sem, vmem_limit_bytes=VMEM_LIMIT)


def _const_spec(shape):
    nd = len(shape)
    return pl.BlockSpec(shape, lambda *a: (0,) * nd, pipeline_mode=pl.Buffered(1))


def _nt(a, b, precision=None):
    return lax.dot_general(a, b, (((1,), (1,)), ((), ())), preferred_element_type=F32, precision=precision)


def _mm(a, b, precision=None):
    return jnp.dot(a, b, preferred_element_type=F32, precision=precision)


def _rms_rows(x, g):
    return x * lax.rsqrt(jnp.mean(x * x, axis=-1, keepdims=True) + EPS) * g


def _silu(x):
    return x * jax.nn.sigmoid(x)


def _head_rms(a, gain2):
    lane = lax.broadcasted_iota(jnp.int32, (1, LANES), 1)
    lo_mask = lane < HEAD_DIM
    outs = []
    for c in range(a.shape[1] // LANES):
        blk = a[:, c * LANES:(c + 1) * LANES]
        sq = blk * blk
        lo = jnp.sum(jnp.where(lo_mask, sq, 0.0), axis=-1, keepdims=True)
        hi = jnp.sum(jnp.where(lo_mask, 0.0, sq), axis=-1, keepdims=True)
        r = jnp.where(lo_mask, lax.rsqrt(lo / HEAD_DIM + EPS), lax.rsqrt(hi / HEAD_DIM + EPS))
        outs.append(blk * r * gain2)
    return outs[0] if len(outs) == 1 else jnp.concatenate(outs, axis=1)


def _ffn_body(x_ref, g_ref, wg_ref, wu_ref, wo_ref, o_ref):
    x = x_ref[...]
    hb = _rms_rows(x, g_ref[...]).astype(BF16)
    gate = _mm(hb, wg_ref[...])
    up = _mm(hb, wu_ref[...])
    act = (_silu(gate) * up).astype(BF16)
    o_ref[...] = x + 0.5 * _mm(act, wo_ref[...])


def _ffn(x2d, norm_g, w_in, w_out, tm):
    n = x2d.shape[0]
    wg = w_in[:, :D_FF].astype(BF16)
    wu = w_in[:, D_FF:].astype(BF16)
    wo = w_out.astype(BF16)
    return pl.pallas_call(
        _ffn_body,
        out_shape=jax.ShapeDtypeStruct((n, D_MODEL), F32),
        grid=(n // tm,),
        in_specs=[pl.BlockSpec((tm, D_MODEL), lambda i: (i, 0)),
                  _const_spec((1, D_MODEL)),
                  _const_spec((D_MODEL, D_FF)), _const_spec((D_MODEL, D_FF)), _const_spec((D_FF, D_MODEL))],
        out_specs=pl.BlockSpec((tm, D_MODEL), lambda i: (i, 0)),
        compiler_params=_cparams(("arbitrary",)),
        name="ffn",
    )(x2d, norm_g.reshape(1, D_MODEL), wg, wu, wo)


def _inproj_body(x_ref, g_ref, w_ref, qkn_ref, z_ref, xbc_ref, q_ref, pg_ref, win_ref, mg_ref, sm_ref):
    hb = _rms_rows(x_ref[...], g_ref[...]).astype(BF16)
    o = 0
    z_ref[...] = _mm(hb, w_ref[:, o:o + D_INNER]); o += D_INNER
    xbc_ref[...] = _mm(hb, w_ref[:, o:o + CONV_DIM]); o += CONV_DIM
    q = _mm(hb, w_ref[:, o:o + D_ATTN]); o += D_ATTN
    q_ref[...] = _head_rms(q, qkn_ref[0:1, :] * (HEAD_DIM ** -0.5))
    kv = _mm(hb, w_ref[:, o:o + 6 * KV_W]); o += 6 * KV_W
    pg_ref[:, 0:2 * KV_W] = kv[:, 0:2 * KV_W]
    pg_ref[:, 2 * KV_W:3 * KV_W] = _head_rms(kv[:, 2 * KV_W:3 * KV_W], qkn_ref[2:3, :])
    pg_ref[:, 3 * KV_W:4 * KV_W] = kv[:, 3 * KV_W:4 * KV_W]
    win_ref[:, 0:KV_W] = _head_rms(kv[:, 4 * KV_W:5 * KV_W], qkn_ref[3:4, :])
    win_ref[:, KV_W:2 * KV_W] = kv[:, 5 * KV_W:6 * KV_W]
    mg_ref[...] = _mm(hb, w_ref[:, o:o + 2 * D_MODEL]); o += 2 * D_MODEL
    sm_ref[...] = _mm(hb, w_ref[:, o:o + SMALL_W])


def _pack_inproj_weight(w):
    c = np.cumsum([0, D_INNER, CONV_DIM, M_HEADS, D_ATTN, 6 * KV_W, 3 * Q_HEADS, 2 * D_MODEL])
    z, xbc, dt, q, kv, ng, mg = [w[:, c[i]:c[i + 1]] for i in range(7)]
    pad = jnp.zeros((D_MODEL, SMALL_W - M_HEADS - 3 * Q_HEADS), w.dtype)
    return jnp.concatenate([z, xbc, q, kv, mg, dt, ng, pad], axis=1).astype(BF16)


def _inproj(x2d, norm_g, w_packed, qk_norm, tm):
    n = x2d.shape[0]
    wtot = w_packed.shape[1]
    qkn2 = jnp.concatenate([qk_norm, qk_norm], axis=1)
    widths = (D_INNER, CONV_DIM, D_ATTN, 4 * KV_W, 2 * KV_W, 2 * D_MODEL, SMALL_W)
    return pl.pallas_call(
        _inproj_body,
        out_shape=[jax.ShapeDtypeStruct((n, w), F32) for w in widths],
        grid=(n // tm,),
        in_specs=[pl.BlockSpec((tm, D_MODEL), lambda i: (i, 0)),
                  _const_spec((1, D_MODEL)), _const_spec((D_MODEL, wtot)), _const_spec((4, LANES))],
        out_specs=[pl.BlockSpec((tm, w), lambda i: (i, 0)) for w in widths],
        compiler_params=_cparams(("arbitrary",)),
        name="inproj",
    )(x2d, norm_g.reshape(1, D_MODEL), w_packed, qkn2)


def _split_bf16(v):
    hi = v.astype(BF16)
    lo = (v - hi.astype(F32)).astype(BF16)
    return hi, lo


def _ssd_body(lv, xbc_ref, z_ref, sm_ref, conv0_ref, ssm0_ref, cw_ref, cb_ref, dtb_ref, alog_ref, dsk_ref,
              nw_ref, expand_ref, y_ref, convo_ref, ssmo_ref, xp_ref, st_ref):
    cl = SSD_CL
    c = pl.program_id(1)
    nchunk = pl.num_programs(1)
    hp = M_HEADS // M_GROUPS * M_HEADDIM

    @pl.when(c == 0)
    def _():
        xp_ref[...] = jnp.zeros(xp_ref.shape, F32)
        xp_ref[8 - (D_CONV - 1):8, :] = conv0_ref[0]
        st_ref[...] = ssm0_ref[0].reshape(M_HEADS * M_HEADDIM, D_STATE).T

    xp_ref[8:8 + lv, :] = xbc_ref[0]
    x_all = xp_ref[...]
    conv = cb_ref[...] + x_all[8:8 + cl] * cw_ref[D_CONV - 1:D_CONV, :]
    for k in range(D_CONV - 1):
        conv = conv + pltpu.roll(x_all, D_CONV - 1 - k, axis=0)[8:8 + cl] * cw_ref[k:k + 1, :]
    tail = xp_ref[8 + lv - (D_CONV - 1):8 + lv, :]
    convo_ref[0] = tail
    xp_ref[8 - (D_CONV - 1):8, :] = tail
    xc = _silu(conv)
    xs = xc[:, :D_INNER]

    sm = sm_ref[0]
    z = z_ref[0]
    if lv < cl:
        sm = jnp.concatenate([sm, jnp.zeros((cl - lv, SMALL_W), F32)], axis=0)
        z = jnp.concatenate([z, jnp.zeros((cl - lv, D_INNER), F32)], axis=0)
    xdt_in = sm + dtb_ref[...]
    dt = jnp.maximum(xdt_in, 0.0) + jnp.log1p(jnp.exp(-jnp.abs(xdt_in)))
    row = lax.broadcasted_iota(jnp.int32, (cl, 1), 0)
    dt = jnp.where(row < lv, dt, 0.0)
    a_neg = -jnp.exp(alog_ref[...])
    da = dt * a_neg
    ri = lax.broadcasted_iota(jnp.int32, (cl, cl), 0)
    ci = lax.broadcasted_iota(jnp.int32, (cl, cl), 1)
    tril = ri >= ci
    acs = _mm(tril.astype(F32), da, precision=HIGHEST)
    alast = acs[cl - 1:cl, :]
    eye = (lax.broadcasted_iota(jnp.int32, (LANES, LANES), 0)
           == lax.broadcasted_iota(jnp.int32, (LANES, LANES), 1)).astype(F32)
    acs_t = _nt(eye, acs, precision=HIGHEST)

    expand = expand_ref[...]
    stack =jnp.concatenate([dt, jnp.exp(acs), dt * jnp.exp(alast - acs)], axis=0)
    s_hi, s_lo = _split_bf16(stack)
    ex = _mm(s_hi, expand) + _mm(s_lo, expand)
    dt_e = ex[0:cl]
    es_e = ex[cl:2 * cl]
    w_e = ex[2 * cl:3 * cl]
    xdt = (xs * dt_e).astype(BF16)
    xw = (xs * w_e).astype(BF16)
    chunk_decay = es_e[cl - 1:cl, :]

    lane = lax.broadcasted_iota(jnp.int32, (1, LANES), 1)
    lo_mask = lane < M_HEADDIM
    eye_b = eye.astype(BF16)
    y_parts = []
    for g in range(M_GROUPS):
        bg = xc[:, D_INNER + g * D_STATE:D_INNER + (g + 1) * D_STATE].astype(BF16)
        cg = xc[:, D_INNER + (M_GROUPS + g) * D_STATE:D_INNER + (M_GROUPS + g + 1) * D_STATE].astype(BF16)
        cb = _nt(cg, bg)
        st_g = st_ref[:, g * hp:(g + 1) * hp]
        y_off = _mm(cg, st_g.astype(BF16)) * es_e[:, g * hp:(g + 1) * hp]
        bg_t = _nt(eye_b, bg).astype(BF16)
        st_ref[:, g * hp:(g + 1) * hp] = (st_g * chunk_decay[:, g * hp:(g + 1) * hp]
                                          + _mm(bg_t, xw[:, g * hp:(g + 1) * hp]))
        for jj in range(hp // LANES):
            col0 = g * hp + jj * LANES
            ms = []
            for j in (2 * (col0 // LANES), 2 * (col0 // LANES) + 1):
                diff = acs[:, j:j + 1] - acs_t[j:j + 1, :]
                dec = jnp.exp(jnp.where(tril, diff, -jnp.inf))
                ms.append((cb * dec).astype(BF16))
            x2 = xdt[:, col0:col0 + LANES]
            zero = jnp.zeros_like(x2)
            rhs = jnp.concatenate([jnp.where(lo_mask, x2, zero), jnp.where(lo_mask, zero, x2)], axis=0)
            y_parts.append(_mm(jnp.concatenate(ms, axis=1), rhs) + y_off[:, jj * LANES:(jj + 1) * LANES])
    y = jnp.concatenate(y_parts, axis=1)
    y = (y + dsk_ref[...] * xs) * _silu(z)
    outs = []
    for g in range(M_GROUPS):
        yg = y[:, g * hp:(g + 1) * hp]
        outs.append(yg * lax.rsqrt(jnp.mean(yg * yg, axis=-1, keepdims=True) + EPS))
    y = jnp.concatenate(outs, axis=1) * nw_ref[...]
    y_ref[0] = y[:lv].astype(BF16)

    @pl.when(c == nchunk - 1)
    def _():
        ssmo_ref[0] = st_ref[...].T.reshape(M_HEADS, M_HEADDIM, D_STATE)


def _ssd(xbc, z, small, conv0, ssm0, conv_w, conv_b, dt_bias, a_log, d_skip, ssm_norm):
    b, L = xbc.shape[:2]
    lv = SSD_CL if L % SSD_CL == 0 else L
    assert lv == SSD_CL or (L < SSD_CL and L % 8 == 0)
    nchunk = L // lv
    pad = lambda v: jnp.pad(v.astype(F32), (0, SMALL_W - M_HEADS)).reshape(1, SMALL_W)
    rep = lambda v: jnp.repeat(v.astype(F32), M_HEADDIM).reshape(1, D_INNER)
    return pl.pallas_call(
        functools.partial(_ssd_body, lv),
        out_shape=[jax.ShapeDtypeStruct((b, L, D_INNER), BF16),
                   jax.ShapeDtypeStruct((b, D_CONV - 1, CONV_DIM), F32),
                   jax.ShapeDtypeStruct((b, M_HEADS, M_HEADDIM, D_STATE), F32)],
        grid=(b, nchunk),
        in_specs=[pl.BlockSpec((1, lv, CONV_DIM), lambda i, c: (i, c, 0)),
                  pl.BlockSpec((1, lv, D_INNER), lambda i, c: (i, c, 0)),
                  pl.BlockSpec((1, lv, SMALL_W), lambda i, c: (i, c, 0)),
                  pl.BlockSpec((1, D_CONV - 1, CONV_DIM), lambda i, c: (i, 0, 0)),
                  pl.BlockSpec((1, M_HEADS, M_HEADDIM, D_STATE), lambda i, c: (i, 0, 0, 0)),
                  _const_spec((D_CONV, CONV_DIM)), _const_spec((1, CONV_DIM)),
                  _const_spec((1, SMALL_W)), _const_spec((1, SMALL_W)),
                  _const_spec((1, D_INNER)), _const_spec((1, D_INNER)), _const_spec((LANES, D_INNER))],
        out_specs=[pl.BlockSpec((1, lv, D_INNER), lambda i, c: (i, c, 0)),
                   pl.BlockSpec((1, D_CONV - 1, CONV_DIM), lambda i, c: (i, 0, 0)),
                   pl.BlockSpec((1, M_HEADS, M_HEADDIM, D_STATE), lambda i, c: (i, 0, 0, 0))],
        scratch_shapes=[pltpu.VMEM((8 + SSD_CL, CONV_DIM), F32), pltpu.VMEM((D_STATE, D_INNER), F32)],
        compiler_params=_cparams(("arbitrary", "arbitrary")),
        name="ssd",
    )(xbc, z, small, conv0, ssm0, conv_w, conv_b.reshape(1, CONV_DIM), pad(dt_bias), pad(a_log),
      rep(d_skip), ssm_norm.reshape(1, D_INNER),
      (np.arange(LANES)[:, None] == np.arange(D_INNER)[None, :] // M_HEADDIM).astype(BF16))


NO_LIMIT = 2 ** 30


def _bias_table_body(base, rc, cc, lo, hi, ncol_valid, tab_ref, o_ref):
    _, rb, cb = o_ref.shape
    r = pl.program_id(0) * rb + lax.broadcasted_iota(jnp.int32, (rb, cb), 0)
    c = pl.program_id(1) * cb + lax.broadcasted_iota(jnp.int32, (rb, cb), 1)
    d = base + rc * r + cc * c
    valid = (d >= lo) & (d <= hi) & (c < ncol_valid)
    max_exact = NUM_BUCKETS // 2
    dd = jnp.maximum(d, 0)
    df = jnp.maximum(dd, 1).astype(F32)
    large = max_exact + (jnp.log(df / max_exact) / math.log(MAX_DISTANCE / max_exact)
                         * (NUM_BUCKETS - max_exact)).astype(jnp.int32)
    large = jnp.minimum(large, NUM_BUCKETS - 1)
    bucket = jnp.where(dd < max_exact, dd, large)
    for hq in range(Q_HEADS):
        acc = jnp.zeros((rb, cb), F32)
        for bkt in range(NUM_BUCKETS):
            acc = jnp.where(bucket == bkt, tab_ref[bkt, hq], acc)
        o_ref[hq] = jnp.where(valid, acc, NEG)


def _bias_table(rel_bias, rows, cols, base, rc, cc, lo=0, hi=NO_LIMIT, ncol_valid=NO_LIMIT):
    rb = LANES if rows > LANES and rows % LANES == 0 else rows
    cb = 2048 if cols > 2048 and cols % 2048 == 0 else cols
    return pl.pallas_call(
        functools.partial(_bias_table_body, base, rc, cc, lo, hi, ncol_valid),
        out_shape=jax.ShapeDtypeStruct((Q_HEADS, rows, cols), F32),
        grid=(rows // rb, cols // cb),
        in_specs=[pl.BlockSpec(memory_space=pltpu.SMEM)],
        out_specs=pl.BlockSpec((Q_HEADS, rb, cb), lambda i, j: (0, i, j)),
        compiler_params=_cparams(("arbitrary", "arbitrary")),
        name="bias_table",
    )(rel_bias.astype(F32))


def _prompt_bias_tables(rel_bias, L):
    n_off = L // TQ + 1
    n_off_w = WINDOW // TQ + 3
    tab_s = _bias_table(rel_bias, n_off * TQ, TQ, -TQ, 1, -1)
    tab_w = _bias_table(rel_bias, n_off_w * TQ, TQ, -TQ, 1, -1, hi=WINDOW)
    n_c = L // CMP_BLOCK
    tab_ct = _bias_table(rel_bias, n_c, L, -(CMP_BLOCK - 1), -CMP_BLOCK, 1)
    return (tab_s.reshape(KV_HEADS, Q_PER_KV, n_off, TQ, TQ), tab_w.reshape(KV_HEADS, Q_PER_KV, n_off_w, TQ, TQ),
            tab_ct.reshape(KV_HEADS, Q_PER_KV, n_c, L))


def _compress_weights(cmp_pe, cmp_w1, cmp_w2):
    eye = jnp.eye(2, dtype=F32)
    w1 = jnp.einsum("hg,cjdf->cjhdgf", eye, cmp_w1).reshape(2, CMP_BLOCK // 2, 2 * LANES, 2 * CMP_HIDDEN)
    w2 = jnp.einsum("hg,cfd->chfgd", eye, cmp_w2).reshape(2, 2 * CMP_HIDDEN, LANES)
    pe = jnp.tile(cmp_pe, (1, 1, 2))
    return pe.astype(F32), w1.astype(BF16), w2.astype(BF16)


def _compress_rows(lo_ref, hi_ref, nblk, pe_ref, w1_ref, w2_ref, c):
    acc = jnp.zeros((2 * nblk, 2 * CMP_HIDDEN), F32)
    regroup = lambda ref: pltpu.einshape("(nj)l->(jn)l", ref[0:nblk * CMP_BLOCK, :], j=CMP_BLOCK)
    lo_all = regroup(lo_ref)
    hi_all = regroup(hi_ref)
    for jj in range(CMP_BLOCK // 2):
        cols = []
        for j in (2 * jj, 2 * jj + 1):
            take = slice(j * nblk, (j + 1) * nblk)
            pe_j = pe_ref[c, j:j + 1, :]
            cols.append(jnp.concatenate([lo_all[take] + pe_j, hi_all[take] + pe_j], axis=0))
        acc = acc + _mm(jnp.concatenate(cols, axis=1).astype(BF16), w1_ref[c, jj])
    out = _mm(_silu(acc).astype(BF16), w2_ref[c])
    return jnp.concatenate([out[:nblk], out[nblk:]], axis=1)


def _compress_prompt_body(k0_ref, k1_ref, v0_ref, v1_ref, pe_ref, w1_ref, w2_ref, kn_ref, kc_ref, vc_ref):
    nblk = kc_ref.shape[1]
    kc_ref[0] = _head_rms(_compress_rows(k0_ref.at[0], k1_ref.at[0], nblk, pe_ref, w1_ref, w2_ref, 0), kn_ref[...])
    vc_ref[0] = _compress_rows(v0_ref.at[0], v1_ref.at[0], nblk, pe_ref, w1_ref, w2_ref, 1)


def _compress_prompt(paged, pe, w1, w2, kn2):
    b, L = paged.shape[:2]
    n_c = L // CMP_BLOCK
    col = lambda c: pl.BlockSpec((1, L, LANES), lambda i: (i, 0, c))
    return pl.pallas_call(
        _compress_prompt_body,
        out_shape=[jax.ShapeDtypeStruct((b, n_c, KV_W), F32)] * 2,
        grid=(b,),
        in_specs=[col(0), col(1), col(2), col(3),
                  _const_spec(pe.shape), _const_spec(w1.shape), _const_spec(w2.shape), _const_spec((1, LANES))],
        out_specs=[pl.BlockSpec((1, n_c, KV_W), lambda i: (i, 0, 0))] * 2,
        compiler_params=_cparams(("arbitrary",)),
        name="compress_prompt",
    )(paged, paged, paged, paged, pe, w1, w2, kn2)


def _online_softmax_step(s, carry, v):
    m, l, acc = carry
    m_new = jnp.maximum(m, jnp.max(s, axis=-1, keepdims=True))
    alpha = jnp.exp(m - m_new)
    p = jnp.exp(s - m_new)
    return (m_new, alpha * l + jnp.sum(p, axis=-1, keepdims=True), alpha * acc + _mm(p.astype(BF16), v))


def _select_blocks(imp, t, n_blk):
    big = 1e30
    j = lax.broadcasted_iota(jnp.int32, (1, imp.shape[1]), 1)
    cur = t // SEL_BLOCK
    forced = (j == 0) | (j == cur) | (j == cur - 1)
    causal = j * SEL_BLOCK <= t
    score = jnp.where(forced, big, jnp.where(causal, imp, -big))
    rank = jnp.zeros(imp.shape, F32)
    for i in range(n_blk):
        ci = score[:, i:i + 1]
        ahead = (ci > score) | ((ci == score) & (i < j))
        rank = rank + jnp.where(ahead, 1.0, 0.0)
    return jnp.where(rank < SEL_TOPK, 1.0, 0.0)


def _nsa_prompt_body(n_sel, q_ref, ks_ref, vs_ref, kw_ref, vw_ref, kc_ref, vc_ref, sm_ref, tabs_ref, tabw_ref,
                     tabc_ref, o_ref, kaug_scr, vaug_scr, kwin_scr, vwin_scr, s_scr, p_scr):
    h = pl.program_id(0)
    qi = pl.program_id(2)
    rows = Q_PER_KV * TQ
    lane = lax.broadcasted_iota(jnp.int32, (1, LANES), 1)
    mine = (lane // HEAD_DIM) == (h % 2)

    @pl.when(qi == 0)
    def _():
        seq = kaug_scr.shape[0]
        key = lax.broadcasted_iota(jnp.int32, (seq, LANES), 0)
        blk = lax.broadcasted_iota(jnp.int32, (seq, LANES), 1)
        kaug_scr[:, 0:LANES] = ks_ref[0].astype(BF16)
        kaug_scr[:, LANES:2 * LANES] = jnp.where(blk == key // SEL_BLOCK, NEG, 0.0).astype(BF16)
        vaug_scr[...] = jnp.where(mine, vs_ref[0], 1.0).astype(BF16)
        kwin_scr[...] = kw_ref[0].astype(BF16)
        vwin_scr[...] = jnp.where(mine, vw_ref[0], 1.0).astype(BF16)

    qb = q_ref[0]
    q4 = jnp.concatenate([qb[:, g * HEAD_DIM:(g + 1) * HEAD_DIM] for g in range(Q_PER_KV)], axis=0)
    qpad = jnp.where(mine, jnp.concatenate([q4, q4], axis=1), 0.0).astype(BF16)

    sub = 64
    even = (h % 2) == 0
    pick = lambda a: jnp.where(even, a[:, :HEAD_DIM], a[:, HEAD_DIM:])
    other = lambda a: jnp.where(even, a[:, HEAD_DIM:], a[:, :HEAD_DIM])

    def attend(q_lhs, k_slab, v_slab, tab_ref, kt0, n_kt):
        width = n_kt * TQ
        n_half = 2
        hrows = rows // n_half
        for hf in range(n_half):
            s_scr[hf * hrows:(hf + 1) * hrows, 0:width] = _nt(q_lhs[hf * hrows:(hf + 1) * hrows], k_slab)
        accs = []
        for hf in range(n_half):
            for rb in range(hf * hrows // sub, (hf + 1) * hrows // sub):
                g, part = divmod(rb, TQ // sub)
                rr = slice(rb * sub, (rb + 1) * sub)
                tr = slice(part * sub, (part + 1) * sub)
                run = None
                for kt in range(n_kt):
                    cc = slice(kt * TQ, (kt + 1) * TQ)
                    s = s_scr[rr, cc] + tab_ref[0, g, jnp.maximum(qi - (kt0 + kt) + 1, 0), tr, :]
                    s_scr[rr, cc] = s
                    run = s if run is None else jnp.maximum(run, s)
                mb = jnp.broadcast_to(jnp.max(run, axis=-1, keepdims=True), (sub, TQ))
                for kt in range(n_kt):
                    cc = slice(kt * TQ, (kt + 1) * TQ)
                    p_scr[rr, cc] = jnp.exp(s_scr[rr, cc] - mb).astype(BF16)
            accs.append(_mm(p_scr[hf * hrows:(hf + 1) * hrows, 0:width], v_slab))
        acc = jnp.concatenate(accs, axis=0)
        return pick(acc) / jnp.maximum(other(acc), 1e-30)

    n_tiles = kaug_scr.shape[0] // TQ
    n_win = min(WINDOW // TQ + 1, n_tiles)
    w0 = jnp.maximum(qi - (n_win - 1), 0)
    w_rows = pl.ds(pl.multiple_of(w0 * TQ, TQ), n_win * TQ)
    o_w = attend(qpad, kwin_scr[w_rows, :], vwin_scr[w_rows, :], tabw_ref, w0, n_win)

    n_c = kc_ref.shape[1]
    bias_ct = jnp.concatenate([tabc_ref[0, g] for g in range(Q_PER_KV)], axis=1)
    s_ct = _nt(kc_ref[0].astype(BF16), qpad) + bias_ct
    e_ct = jnp.where(bias_ct > 0.5 * NEG, jnp.exp(s_ct - jnp.max(s_ct, axis=0, keepdims=True)), 0.0)
    p_ct = e_ct / jnp.maximum(jnp.sum(e_ct, axis=0, keepdims=True), 1e-30)
    o_c = lax.dot_general(p_ct.astype(BF16), vc_ref[0].astype(BF16), (((0,), (0,)), ((), ())),
                          preferred_element_type=F32)

    imp_c = sum(p_ct[:, g * TQ:(g + 1) * TQ] for g in range(Q_PER_KV))
    nr = -(-n_sel // 8) * 8
    pair = (lax.broadcasted_iota(jnp.int32, (nr, n_c), 0)
            == lax.broadcasted_iota(jnp.int32, (nr, n_c), 1) // (SEL_BLOCK // CMP_BLOCK)).astype(BF16)
    c_hi = imp_c.astype(BF16)
    c_mid = (imp_c - c_hi.astype(F32)).astype(BF16)
    c_lo = (imp_c - c_hi.astype(F32) - c_mid.astype(F32)).astype(BF16)
    imp = _mm(pair, c_hi) + _mm(pair, c_mid) + _mm(pair, c_lo)
    t = qi * TQ + lax.broadcasted_iota(jnp.int32, (1, TQ), 1)
    j = lax.broadcasted_iota(jnp.int32, (nr, 1), 0)
    cur = t // SEL_BLOCK
    forced = (j == 0) | (j == cur) | (j == cur - 1)
    score = jnp.where(forced, 1e30, jnp.where(j * SEL_BLOCK <= t, imp, -1e30))
    rank = jnp.zeros((nr, TQ), F32)
    for i in range(n_sel):
        ri = score[i:i + 1, :]
        rank = rank + jnp.where((ri > score) | ((ri == score) & (i < j)), 1.0, 0.0)
    drop_t = jnp.where(rank < SEL_TOPK, 0.0, 1.0)
    drop_t = jnp.concatenate([drop_t, jnp.ones((LANES - nr, TQ), F32)], axis=0)
    drop = drop_t.T.astype(BF16)
    qaug = jnp.concatenate([qpad, jnp.concatenate([drop] * Q_PER_KV, axis=0)], axis=1)

    ext_step = 2
    extents = [min((e + 1) * ext_step, n_tiles) for e in range(-(-n_tiles // ext_step))]
    o_s = lax.switch(qi // ext_step,
                     [functools.partial(lambda n: attend(qaug, kaug_scr[0:n * TQ, :], vaug_scr[0:n * TQ, :],
                                                         tabs_ref, 0, n), n) for n in extents])
    o_c = pick(o_c)

    gates = jax.nn.sigmoid(pltpu.roll(sm_ref[0], SMALL_W - NSA_G_OFF - h * (3 * Q_PER_KV), axis=1))
    outs = []
    for g in range(Q_PER_KV):
        rs = slice(g * TQ, (g + 1) * TQ)
        outs.append(gates[:, 3 * g:3 * g + 1] * o_c[rs] + gates[:, 3 * g + 1:3 * g + 2] * o_s[rs]
                    + gates[:, 3 * g + 2:3 * g + 3] * o_w[rs])
    o_ref[0] = jnp.concatenate(outs, axis=1).astype(BF16)


def _nsa_prompt(qn, paged, win, kc, vc, small, tab_s, tab_w, tab_c):
    b, L = qn.shape[:2]
    assert L % TQ == 0
    n_c = L // CMP_BLOCK
    rows = Q_PER_KV * TQ
    seq = lambda col: pl.BlockSpec((1, L, LANES), lambda h, i, q: (i, 0, col(h)))
    cmp = pl.BlockSpec((1, n_c, LANES), lambda h, i, q: (i, 0, h // 2))
    return pl.pallas_call(
        functools.partial(_nsa_prompt_body, L // SEL_BLOCK),
        out_shape=jax.ShapeDtypeStruct((b, L, D_ATTN), BF16),
        grid=(KV_HEADS, b, L // TQ),
        in_specs=[pl.BlockSpec((1, TQ, KV_W), lambda h, i, q: (i, q, h)),
                  seq(lambda h: 4 + h // 2), seq(lambda h: 6 + h // 2),
                  seq(lambda h: h // 2), seq(lambda h: 2 + h // 2),
                  cmp, cmp,
                  pl.BlockSpec((1, TQ, SMALL_W), lambda h, i, q: (i, q, 0)),
                  pl.BlockSpec((1,) + tab_s.shape[1:], lambda h, i, q: (h, 0, 0, 0, 0)),
                  pl.BlockSpec((1,) + tab_w.shape[1:], lambda h, i, q: (h, 0, 0, 0, 0)),
                  pl.BlockSpec((1, Q_PER_KV, n_c, TQ), lambda h, i, q: (h, 0, 0, q))],
        out_specs=pl.BlockSpec((1, TQ, KV_W), lambda h, i, q: (i, q, h)),
        scratch_shapes=[pltpu.VMEM((L, 2 * LANES), BF16), pltpu.VMEM((L, LANES), BF16),
                        pltpu.VMEM((L, LANES), BF16), pltpu.VMEM((L, LANES), BF16),
                        pltpu.VMEM((rows, L), F32), pltpu.VMEM((rows, L), BF16)],
        compiler_params=_cparams(("arbitrary", "arbitrary", "arbitrary")),
        name="nsa_prompt",
    )(qn, paged, paged, win, win, kc, vc, small, tab_s, tab_w, tab_c)


def _merge_body(x_ref, y_ref, o_ref, mg_ref, wa_ref, wb_ref, wo_ref, out_ref):
    ya = _mm(y_ref[...], wa_ref[...])
    yb = _mm(o_ref[...], wb_ref[...])
    gm = jax.nn.sigmoid(mg_ref[...])
    mix = (gm[:, :D_MODEL] * ya + gm[:, D_MODEL:] * yb).astype(BF16)
    out_ref[...] = x_ref[...] + _mm(mix, wo_ref[...])


def _merge(x2d, y2d, o2d, mg, wa, wb, wo, tm):
    n = x2d.shape[0]
    row = lambda w: pl.BlockSpec((tm, w), lambda i: (i, 0))
    return pl.pallas_call(
        _merge_body,
        out_shape=jax.ShapeDtypeStruct((n, D_MODEL), F32),
        grid=(n // tm,),
        in_specs=[row(D_MODEL), row(D_INNER), row(D_ATTN), row(2 * D_MODEL),
                  _const_spec((D_INNER, D_MODEL)), _const_spec((D_ATTN, D_MODEL)), _const_spec((D_MODEL, D_MODEL))],
        out_specs=row(D_MODEL),
        compiler_params=_cparams(("arbitrary",)),
        name="merge",
    )(x2d, y2d, o2d, mg, wa.astype(BF16), wb.astype(BF16), wo.astype(BF16))


def _layer_prompt(x, lw, rel_bias, tm):
    (ffn1_norm, ffn1_w_in, ffn1_w_out, mix_norm, w_packed, conv_w, conv_b, dt_bias, a_log, d_skip, ssm_norm,
     qk_norm, cmp_prep, w_branch_ssm, w_branch_attn, w_out, ffn2_norm, ffn2_w_in, ffn2_w_out) = lw
    b, L = x.shape[:2]
    n = b * L
    tm_wide = 2 * tm if n % (2 * tm) == 0 else tm
    x1 = _ffn(x.reshape(n, D_MODEL), ffn1_norm, ffn1_w_in, ffn1_w_out, tm_wide)
    z, xbc, qn, paged, win, mg, small = _inproj(x1, mix_norm, w_packed, qk_norm, tm)
    r3 = lambda a: a.reshape(b, L, a.shape[-1])
    conv0 = jnp.zeros((b, D_CONV - 1, CONV_DIM), F32)
    ssm0 = jnp.zeros((b, M_HEADS, M_HEADDIM, D_STATE), F32)
    y, new_conv, new_ssm = _ssd(r3(xbc), r3(z), r3(small), conv0, ssm0, conv_w, conv_b, dt_bias, a_log, d_skip,
                                ssm_norm)
    pe, w1, w2 = cmp_prep
    kn2 = jnp.concatenate([qk_norm[1], qk_norm[1]]).reshape(1, LANES)
    kc, vc = _compress_prompt(r3(paged), pe, w1, w2, kn2)
    tab_s, tab_w, tab_c = _prompt_bias_tables(rel_bias, L)
    o = _nsa_prompt(r3(qn), r3(paged), r3(win), kc, vc, r3(small), tab_s, tab_w, tab_c)
    x2 = _merge(x1, y.reshape(n, D_INNER), o.reshape(n, D_ATTN), mg, w_branch_ssm, w_branch_attn, w_out, tm_wide)
    x3 = _ffn(x2, ffn2_norm, ffn2_w_in, ffn2_w_out, tm_wide)
    n_keep = min(WINDOW, L)
    new_kv = paged.reshape(b, L, 4, KV_HEADS, HEAD_DIM)
    new_win = r3(win)[:, L - n_keep:].reshape(b, n_keep, 2, KV_HEADS, HEAD_DIM)
    return x3.reshape(b, L, D_MODEL), new_kv, new_win, new_ssm, new_conv


PAGES_PER_STEP = 32
N_SLOTS = 3


def _sample_qpad(qb):
    ls = qb.shape[0]
    blocks = []
    for hq in range(Q_HEADS):
        h = hq // Q_PER_KV
        parts = []
        if h > 0:
            parts.append(jnp.zeros((ls, h * HEAD_DIM), F32))
        parts.append(qb[:, hq * HEAD_DIM:(hq + 1) * HEAD_DIM])
        if h < KV_HEADS - 1:
            parts.append(jnp.zeros((ls, (KV_HEADS - 1 - h) * HEAD_DIM), F32))
        blocks.append(jnp.concatenate(parts, axis=1))
    return jnp.concatenate(blocks, axis=0).astype(BF16)


def _sample_head_piece(o, hq, ls):
    h = hq // Q_PER_KV
    return o[hq * ls:(hq + 1) * ls, h * HEAD_DIM:(h + 1) * HEAD_DIM]


def _chunk_copies(pt_ref, cache_ref, buf, sem, cols, width, bb, ii, sl, p):
    page = pt_ref[bb, ii * PAGES_PER_STEP + p]
    row0 = pl.multiple_of(p * PAGE_SIZE, PAGE_SIZE)
    return [pltpu.make_async_copy(cache_ref.at[page, :, pl.ds(c0, width)],
                                  buf.at[sl, k, pl.ds(row0, PAGE_SIZE), :], sem.at[sl])
            for k, c0 in enumerate(cols)]


def _stream_chunks(pt_ref, cache_ref, buf, sem, cols, width):
    b = pl.program_id(0)
    i = pl.program_id(1)
    nch = pl.num_programs(1)
    total = pl.num_programs(0) * nch
    step = b * nch + i
    slot = step % N_SLOTS

    def run(bb, ii, sl, start):
        def body(pp, carry):
            for prio in range(2):
                for cp in _chunk_copies(pt_ref, cache_ref, buf, sem, cols, width, bb, ii, sl, 2 * pp + prio):
                    cp.start(priority=prio) if start else cp.wait()
            return carry
        lax.fori_loop(0, PAGES_PER_STEP // 2, body, 0)

    def start_chunk(s):
        @pl.when(s < total)
        def _():
            run(s // nch, s % nch, s % N_SLOTS, True)

    @pl.when(step == 0)
    def _():
        for ahead in range(N_SLOTS - 1):
            start_chunk(step + ahead)

    start_chunk(step + N_SLOTS - 1)
    run(b, i, slot, False)
    return slot


def _cmp_sample_body(past_len, pt_ref, cache_ref, q_ref, biasc_ref, pe_ref, w1_ref, w2_ref, kn_ref,
                     oc_ref, selm_ref, buf, sem, kc_s, vc_s):
    i = pl.program_id(1)
    slot = _stream_chunks(pt_ref, cache_ref, buf, sem, [0], 2 * KV_W)
    lanes = lambda k: buf.at[slot, 0, :, pl.ds(k * LANES, LANES)]
    nblk = PAGES_PER_STEP * PAGE_SIZE // CMP_BLOCK
    rows0 = pl.multiple_of(i * nblk, nblk)
    kc_s[pl.ds(rows0, nblk), :] = _head_rms(
        _compress_rows(lanes(0), lanes(1), nblk, pe_ref, w1_ref, w2_ref, 0), kn_ref[...])
    vc_s[pl.ds(rows0, nblk), :] = _compress_rows(lanes(2), lanes(3), nblk, pe_ref, w1_ref, w2_ref, 1)

    @pl.when(i == pl.num_programs(1) - 1)
    def _():
        ls = q_ref.shape[1]
        n_c = kc_s.shape[0]
        s = _nt(_sample_qpad(q_ref[0]), kc_s[...].astype(BF16)) + biasc_ref[...]
        e = jnp.exp(s - jnp.max(s, axis=-1, keepdims=True))
        p = e / jnp.maximum(jnp.sum(e, axis=-1, keepdims=True), 1e-30)
        oc = _mm(p.astype(BF16), vc_s[...].astype(BF16))
        oc_ref[0] = jnp.concatenate([_sample_head_piece(oc, hq, ls) for hq in range(Q_HEADS)], axis=1)
        imp_c = jnp.sum(p.reshape(KV_HEADS, Q_PER_KV, ls, n_c), axis=1).reshape(KV_HEADS * ls, n_c)
        w = selm_ref.shape[2]
        pair = (lax.broadcasted_iota(jnp.int32, (n_c, w), 0) // (SEL_BLOCK // CMP_BLOCK)
                == lax.broadcasted_iota(jnp.int32, (n_c, w), 1)).astype(F32)
        imp = _mm(imp_c, pair, precision=HIGHEST)
        t = past_len + lax.broadcasted_iota(jnp.int32, (KV_HEADS * ls, 1), 0) % ls
        selm_ref[0] = _select_blocks(imp, t, past_len // SEL_BLOCK + 1)


def _nsa_sample_body(n_chunks, pt_ref, cache_ref, q_ref, selm_ref, oc_ref, sm_ref, pnew_ref, cwin_ref, wnew_ref,
                     biass_ref, biasw_ref, biasn_ref, spread_ref, o_ref, buf, sem, m_s, l_s, a_s):
    i = pl.program_id(1)
    slot = _stream_chunks(pt_ref, cache_ref, buf, sem, [2 * KV_W], 2 * KV_W)
    ls = q_ref.shape[1]
    rows = Q_HEADS * ls
    tk = PAGES_PER_STEP * PAGE_SIZE
    qpad = _sample_qpad(q_ref[0])

    @pl.when(i == 0)
    def _():
        m_s[...] = jnp.full(m_s.shape, NEG, F32)
        l_s[...] = jnp.zeros(l_s.shape, F32)
        a_s[...] = jnp.zeros(a_s.shape, F32)

    sel = selm_ref[0]
    w = sel.shape[1]
    sel = jnp.broadcast_to(sel.reshape(KV_HEADS, 1, ls, w), (KV_HEADS, Q_PER_KV, ls, w)).reshape(rows, w)
    bpc = tk // SEL_BLOCK
    sel_c = sel[:, 0:LANES]
    for k in range(1, n_chunks):
        sel_c = jnp.where(i == k, sel[:, k * bpc:k * bpc + LANES], sel_c)
    keep = _mm(sel_c.astype(BF16), spread_ref[...])
    s = _nt(qpad, buf[slot, 0, :, 0:KV_W].astype(BF16)) + biass_ref[...] + jnp.where(keep > 0.5, 0.0, NEG)
    m, l, a = _online_softmax_step(s, (m_s[...], l_s[...], a_s[...]), buf[slot, 0, :, KV_W:2 * KV_W].astype(BF16))
    m_s[...] = m
    l_s[...] = l
    a_s[...] = a

    @pl.when(i == pl.num_programs(1) - 1)
    def _():
        def pad_rows(x):
            return jnp.concatenate([x, jnp.zeros((LANES - ls, x.shape[1]), F32)], axis=0).astype(BF16)

        pn = pnew_ref[0]
        wn = wnew_ref[0]
        s_n = _nt(qpad, pad_rows(pn[:, 2 * KV_W:3 * KV_W])) + biasn_ref[...]
        _, l2, a2 = _online_softmax_step(s_n, (m, l, a), pad_rows(pn[:, 3 * KV_W:4 * KV_W]))
        o_s = a2 / jnp.maximum(l2, 1e-30)
        cw = cwin_ref[0]
        init = (jnp.full((rows, 1), NEG, F32), jnp.zeros((rows, 1), F32), jnp.zeros((rows, KV_W), F32))
        c1 = _online_softmax_step(_nt(qpad, cw[:, 0:KV_W].astype(BF16)) + biasw_ref[...], init,
                                  cw[:, KV_W:2 * KV_W].astype(BF16))
        s_wn = _nt(qpad, pad_rows(wn[:, 0:KV_W])) + biasn_ref[...]
        _, l3, a3 = _online_softmax_step(s_wn, c1, pad_rows(wn[:, KV_W:2 * KV_W]))
        o_w = a3 / jnp.maximum(l3, 1e-30)
        gates = jax.nn.sigmoid(sm_ref[0])
        oc = oc_ref[0]
        outs = []
        for hq in range(Q_HEADS):
            g0 = NSA_G_OFF + 3 * hq
            outs.append(gates[:, g0:g0 + 1] * oc[:, hq * HEAD_DIM:(hq + 1) * HEAD_DIM]
                        + gates[:, g0 + 1:g0 + 2] * _sample_head_piece(o_s, hq, ls)
                        + gates[:, g0 + 2:g0 + 3] * _sample_head_piece(o_w, hq, ls))
        o_ref[0] = jnp.concatenate(outs, axis=1).astype(BF16)


def _sample_bias_tables(rel_bias, past_len, ls, n_buf):
    rows2d = lambda t: t.reshape(Q_HEADS * ls, t.shape[2])
    bias_s = _bias_table(rel_bias, ls, past_len, past_len, 1, -1)
    n_c = past_len // CMP_BLOCK
    bias_c = _bias_table(rel_bias, ls, n_c, past_len - (CMP_BLOCK - 1), 1, -CMP_BLOCK)
    bias_w = _bias_table(rel_bias, ls, n_buf, n_buf, 1, -1, hi=WINDOW)
    bias_n = _bias_table(rel_bias, ls, LANES, 0, 1, -1, ncol_valid=ls)
    return rows2d(bias_s), rows2d(bias_c), rows2d(bias_w), rows2d(bias_n)


def _nsa_sample(qn, paged, win, small, cache_pages, page_table, cache_win, rel_bias, cmp_prep, kn2):
    b, ls = qn.shape[:2]
    n_pages = page_table.shape[1]
    past_len = n_pages * PAGE_SIZE
    n_buf = cache_win.shape[1]
    assert n_pages % PAGES_PER_STEP == 0 and ls <= SEL_BLOCK and ls % 8 == 0 and n_buf == min(WINDOW, past_len)
    nch = n_pages // PAGES_PER_STEP
    tk = PAGES_PER_STEP * PAGE_SIZE
    n_c = past_len // CMP_BLOCK
    n_sel = past_len // SEL_BLOCK + 1
    w_sel = -(-n_sel // LANES) * LANES
    rows = Q_HEADS * ls
    pe, w1, w2 = cmp_prep
    bias_s, bias_c, bias_w, bias_n = _sample_bias_tables(rel_bias, past_len, ls, n_buf)
    tok = lambda wd: pl.BlockSpec((1, ls, wd), lambda i, c, pt: (i, 0, 0))
    const = lambda shape: pl.BlockSpec(shape, lambda i, c, pt: (0,) * len(shape), pipeline_mode=pl.Buffered(1))
    o_c, selm = pl.pallas_call(
        functools.partial(_cmp_sample_body, past_len),
        out_shape=[jax.ShapeDtypeStruct((b, ls, D_ATTN), F32),
                   jax.ShapeDtypeStruct((b, KV_HEADS * ls, w_sel), F32)],
        grid_spec=pltpu.PrefetchScalarGridSpec(
            num_scalar_prefetch=1, grid=(b, nch),
            in_specs=[pl.BlockSpec(memory_space=pl.ANY), tok(D_ATTN), const((rows, n_c)),
                      const(pe.shape), const(w1.shape), const(w2.shape), const((1, LANES))],
            out_specs=[tok(D_ATTN), pl.BlockSpec((1, KV_HEADS * ls, w_sel), lambda i, c, pt: (i, 0, 0))],
            scratch_shapes=[pltpu.VMEM((N_SLOTS, 1, tk, 2 * KV_W), F32), pltpu.SemaphoreType.DMA((N_SLOTS,)),
                            pltpu.VMEM((n_c, KV_W), F32), pltpu.VMEM((n_c, KV_W), F32)]),
        compiler_params=_cparams(("arbitrary", "arbitrary")),
        name="nsa_sample_compress",
    )(page_table, cache_pages, qn, bias_c, pe, w1, w2, kn2)
    assert (nch - 1) * (tk // SEL_BLOCK) + LANES <= w_sel
    spread = (np.arange(LANES)[:, None] == np.arange(tk)[None, :] // SEL_BLOCK).astype(BF16)
    return pl.pallas_call(
        functools.partial(_nsa_sample_body, nch),
        out_shape=jax.ShapeDtypeStruct((b, ls, D_ATTN), BF16),
        grid_spec=pltpu.PrefetchScalarGridSpec(
            num_scalar_prefetch=1, grid=(b, nch),
            in_specs=[pl.BlockSpec(memory_space=pl.ANY), tok(D_ATTN),
                      pl.BlockSpec((1, KV_HEADS * ls, w_sel), lambda i, c, pt: (i, 0, 0)),
                      tok(D_ATTN), tok(SMALL_W), tok(4 * KV_W),
                      pl.BlockSpec((1, n_buf, 2 * KV_W), lambda i, c, pt: (i, 0, 0)), tok(2 * KV_W),
                      pl.BlockSpec((rows, tk), lambda i, c, pt: (0, c)),
                      const((rows, n_buf)), const((rows, LANES)), const((LANES, tk))],
            out_specs=tok(D_ATTN),
            scratch_shapes=[pltpu.VMEM((N_SLOTS, 1, tk, 2 * KV_W), F32), pltpu.SemaphoreType.DMA((N_SLOTS,)),
                            pltpu.VMEM((rows, 1), F32), pltpu.VMEM((rows, 1), F32), pltpu.VMEM((rows, KV_W), F32)]),
        compiler_params=_cparams(("arbitrary", "arbitrary")),
        name="nsa_sample",
    )(page_table, cache_pages, qn, selm, o_c, small, paged, cache_win, win, bias_s, bias_w, bias_n, spread)


def _layer_sample(x, ssm0, conv0, cache_kv_l, page_table, cache_win_l, lw, rel_bias):
    (ffn1_norm, ffn1_w_in, ffn1_w_out, mix_norm, w_packed, conv_w, conv_b, dt_bias, a_log, d_skip, ssm_norm,
     qk_norm, cmp_prep, w_branch_ssm, w_branch_attn, w_out, ffn2_norm, ffn2_w_in, ffn2_w_out) = lw
    b, ls = x.shape[:2]
    n = b * ls
    tm = n
    x1 = _ffn(x.reshape(n, D_MODEL), ffn1_norm, ffn1_w_in, ffn1_w_out, tm)
    z, xbc, qn, paged, win, mg, small = _inproj(x1, mix_norm, w_packed, qk_norm, tm)
    r3 = lambda a: a.reshape(b, ls, a.shape[-1])
    y, new_conv, new_ssm = _ssd(r3(xbc), r3(z), r3(small), conv0, ssm0, conv_w, conv_b, dt_bias, a_log, d_skip,
                                ssm_norm)
    kn2 = jnp.concatenate([qk_norm[1], qk_norm[1]]).reshape(1, LANES)
    n_pool = cache_kv_l.shape[0]
    cache_pages = cache_kv_l.reshape(n_pool, PAGE_SIZE, 4 * KV_W)
    n_buf = cache_win_l.shape[1]
    cwin = cache_win_l.reshape(b, n_buf, 2 * KV_W)
    o = _nsa_sample(r3(qn), r3(paged), r3(win), r3(small), cache_pages, page_table, cwin, rel_bias, cmp_prep, kn2)
    x2 = _merge(x1, y.reshape(n, D_INNER), o.reshape(n, D_ATTN), mg, w_branch_ssm, w_branch_attn, w_out, tm)
    x3 = _ffn(x2, ffn2_norm, ffn2_w_in, ffn2_w_out, tm)
    n_keep = min(WINDOW, n_buf + ls)
    new_kv = paged.reshape(b, ls, 4, KV_HEADS, HEAD_DIM)
    win_all = jnp.concatenate([cwin, r3(win)], axis=1)
    new_win = win_all[:, n_buf + ls - n_keep:].reshape(b, n_keep, 2, KV_HEADS, HEAD_DIM)
    return x3.reshape(b, ls, D_MODEL), new_kv, new_win, new_ssm, new_conv


def _layer_weights(l, ffn1_norm, ffn1_w_in, ffn1_w_out, mix_norm, w_in_proj, conv_w, conv_b, dt_bias, a_log,
                   d_skip, ssm_norm, qk_norm, cmp_pe, cmp_w1, cmp_w2, w_branch_ssm, w_branch_attn, w_out,
                   ffn2_norm, ffn2_w_in, ffn2_w_out):
    return (ffn1_norm[l], ffn1_w_in[l], ffn1_w_out[l], mix_norm[l], _pack_inproj_weight(w_in_proj[l]), conv_w[l],
            conv_b[l], dt_bias[l], a_log[l], d_skip[l], ssm_norm[l], qk_norm[l],
            _compress_weights(cmp_pe[l], cmp_w1[l], cmp_w2[l]), w_branch_ssm[l], w_branch_attn[l], w_out[l],
            ffn2_norm[l], ffn2_w_in[l], ffn2_w_out[l])


def kernel(x_prompt, x_sample, cache_kv, cache_win, state_ssm, state_conv, page_table, rel_bias, ffn1_norm, ffn1_w_in, ffn1_w_out, mix_norm, w_in_proj, conv_w, conv_b, dt_bias, a_log, d_skip, ssm_norm, qk_norm, cmp_pe, cmp_w1, cmp_w2, w_branch_ssm, w_branch_attn, w_out, ffn2_norm, ffn2_w_in, ffn2_w_out):
    depth = ffn1_norm.shape[0]
    xp, xs = x_prompt, x_sample
    outs_p, outs_s = [], []
    for l in range(depth):
        lw = _layer_weights(l, ffn1_norm, ffn1_w_in, ffn1_w_out, mix_norm, w_in_proj, conv_w, conv_b, dt_bias,
                            a_log, d_skip, ssm_norm, qk_norm, cmp_pe, cmp_w1, cmp_w2, w_branch_ssm,
                            w_branch_attn, w_out, ffn2_norm, ffn2_w_in, ffn2_w_out)
        xp, *rp = _layer_prompt(xp, lw, rel_bias, TOKEN_TILE)
        outs_p.append(rp)
        xs, *rs = _layer_sample(xs, state_ssm[l], state_conv[l], cache_kv[l], page_table, cache_win[l], lw, rel_bias)
        outs_s.append(rs)
    stack = lambda outs, k: jnp.stack([o[k] for o in outs])
    return ((xp, xs) + tuple(stack(outs_p, k) for k in range(4)) + tuple(stack(outs_s, k) for k in range(4)))
```
